```python
import jax, jax.numpy as jnp
from jax import lax
import numpy as np

D_MODEL = 4096
BATCH = 2
SEQ = 4096
DEPTH = 1
DEC_BATCH = 32
DEC_SEQ = 1
PAST_LEN = 8192
PAGE_SIZE = 128

MIX = D_MODEL
HEAD_DIM = 128
ATT_HEADS = MIX // 4 // HEAD_DIM
ATT_WIDTH = ATT_HEADS * HEAD_DIM
MLSTM_WIDTH = MIX - ATT_WIDTH
MLSTM_DV = 512
MLSTM_HEADS = MLSTM_WIDTH // MLSTM_DV
MLSTM_DQK = MLSTM_DV // 2
MLSTM_CHUNK = 64
GATE_CAP = 15.0
DILATED_CONFIGS = ((128, 1), (512, 4), (2048, 16))
WIN_MAX = 2048
ROPE_THETA = 10000.0
D_FF = 4 * D_MODEL
Q_BLOCK = 128
EPS = 1e-6
SPLIT_SIZES = (ATT_WIDTH, ATT_WIDTH, ATT_WIDTH,
               MLSTM_HEADS * MLSTM_DQK, MLSTM_HEADS * MLSTM_DQK,
               MLSTM_WIDTH, MLSTM_WIDTH, MLSTM_HEADS, MLSTM_HEADS)
N_IN = 3 * ATT_WIDTH + 2 * MLSTM_HEADS * MLSTM_DQK + 2 * MLSTM_WIDTH + 2 * MLSTM_HEADS

kernel_name = 'hymba_dilated_attn_mlstm_step'


def rmsnorm(x, g):
    x32 = x.astype(jnp.float32)
    y = x32 * lax.rsqrt(jnp.mean(x32 * x32, axis=-1, keepdims=True) + EPS)
    return (y * g.astype(jnp.float32)).astype(x.dtype)


def rope(x, pos):
    half = HEAD_DIM // 2
    inv = ROPE_THETA ** (-jnp.arange(half, dtype=jnp.float32) / half)
    ang = pos.astype(jnp.float32)[:, None] * inv[None, :]
    cos = jnp.cos(ang)[None, :, None, :]
    sin = jnp.sin(ang)[None, :, None, :]
    x32 = x.astype(jnp.float32)
    x1, x2 = x32[..., :half], x32[..., half:]
    return jnp.concatenate([x1 * cos - x2 * sin, x2 * cos + x1 * sin], axis=-1).astype(x.dtype)


def soft_cap(a):
    return GATE_CAP * jnp.tanh(a / GATE_CAP)


def in_proj(xn, w_in, b_if):
    B, T, _ = xn.shape
    z = jnp.einsum('btd,dn->btn', xn, w_in)
    offs = [0]
    for s in SPLIT_SIZES:
        offs.append(offs[-1] + s)
    qa, ka, va, qm, km, vm, om, ig, fg = [z[..., offs[i]:offs[i + 1]] for i in range(len(SPLIT_SIZES))]
    qa = qa.reshape(B, T, ATT_HEADS, HEAD_DIM)
    ka = ka.reshape(B, T, ATT_HEADS, HEAD_DIM)
    va = va.reshape(B, T, ATT_HEADS, HEAD_DIM)
    qm = qm.reshape(B, T, MLSTM_HEADS, MLSTM_DQK)
    km = km.reshape(B, T, MLSTM_HEADS, MLSTM_DQK)
    vm = vm.reshape(B, T, MLSTM_HEADS, MLSTM_DV)
    ig = soft_cap(ig.astype(jnp.float32) + b_if[:MLSTM_HEADS].astype(jnp.float32))
    fg = soft_cap(fg.astype(jnp.float32) + b_if[MLSTM_HEADS:].astype(jnp.float32))
    return qa, ka, va, qm, km, vm, om, ig, fg


def dilated_attention(q, k_src, v_src, q_pos, src_offset):
    S = k_src.shape[1]
    q = q * (HEAD_DIM ** -0.5)
    maxes, dens, nums = [], [], []
    for (w, r) in DILATED_CONFIGS:
        j = jnp.arange(w // r + 1, dtype=jnp.int32)
        pos = q_pos[:, None] - j[None, :] * r
        valid = pos >= 0
        row = jnp.clip(pos - src_offset, 0, S - 1)
        kg = jnp.take(k_src, row, axis=1)
        vg = jnp.take(v_src, row, axis=1)
        s = jnp.einsum('bqhd,bqjhd->bhqj', q, kg, preferred_element_type=jnp.float32)
        s = jnp.where(valid[None, None], s, -jnp.inf)
        m = jnp.max(s, axis=-1, keepdims=True)
        p = jnp.exp(s - m)
        maxes.append(m)
        dens.append(jnp.sum(p, axis=-1))
        nums.append(jnp.einsum('bhqj,bqjhd->bhqd', p, vg.astype(jnp.float32)))
    m_all = maxes[0]
    for m in maxes[1:]:
        m_all = jnp.maximum(m_all, m)
    num = 0.0
    den = 0.0
    for m, d, n in zip(maxes, dens, nums):
        wgt = jnp.exp(m - m_all)
        num = num + wgt * n
        den = den + wgt[..., 0] * d
    out = num / den[..., None]
    return out.transpose(0, 2, 1, 3)


def mlstm_chunkwise(q, k, v, ig, fg, C0, n0, m0):
    f32 = jnp.float32
    B, T, H, DK = q.shape
    DV = v.shape[-1]
    L = MLSTM_CHUNK if T % MLSTM_CHUNK == 0 else T
    NC = T // L

    def chunks(a):
        return a.astype(f32).reshape(B, NC, L, H, -1).transpose(1, 0, 3, 2, 4)

    qc = chunks(q) * (DK ** -0.5)
    kc = chunks(k)
    vc = chunks(v)
    li = ig.astype(f32).reshape(B, NC, L, H).transpose(1, 0, 3, 2)
    lf = jax.nn.log_sigmoid(fg.astype(f32)).reshape(B, NC, L, H).transpose(1, 0, 3, 2)
    causal = jnp.tril(jnp.ones((L, L), dtype=bool))

    def step(carry, xs):
        C, n, m = carry
        q_, k_, v_, li_, lf_ = xs
        b = jnp.cumsum(lf_, axis=-1)
        dlog = jnp.where(causal, b[..., :, None] - b[..., None, :] + li_[..., None, :], -jnp.inf)
        inter = b + m[..., None]
        m_t = jnp.maximum(inter, jnp.max(dlog, axis=-1))
        sw = jnp.einsum('bhtd,bhsd->bhts', q_, k_) * jnp.exp(dlog - m_t[..., None])
        a = jnp.exp(inter - m_t)
        num = jnp.einsum('bhts,bhsv->bhtv', sw, v_) + a[..., None] * jnp.einsum('bhtd,bhdv->bhtv', q_, C)
        den = jnp.sum(sw, axis=-1) + a * jnp.einsum('bhtd,bhd->bht', q_, n)
        h = num / jnp.maximum(jnp.abs(den), jnp.exp(-m_t))[..., None]
        m_new = m_t[..., -1]
        ws = jnp.exp(b[..., -1:] - b + li_ - m_new[..., None])
        decay = jnp.exp(b[..., -1] + m - m_new)
        C_new = decay[..., None, None] * C + jnp.einsum('bhs,bhsd,bhsv->bhdv', ws, k_, v_)
        n_new = decay[..., None] * n + jnp.einsum('bhs,bhsd->bhd', ws, k_)
        return (C_new, n_new, m_new), h

    (C, n, m), h = lax.scan(step, (C0.astype(f32), n0.astype(f32), m0.astype(f32)), (qc, kc, vc, li, lf))
    h = h.transpose(1, 0, 3, 2, 4).reshape(B, T, H, DV)
    return h, C, n, m


def merge_and_ffn(x, att, hm, om, mh_g, w_out, n2_g, w_up, w_down):
    B, T, _ = x.shape
    f32 = jnp.float32
    hn = hm * lax.rsqrt(jnp.mean(hm * hm, axis=-1, keepdims=True) + EPS)
    hn = hn.reshape(B, T, MLSTM_WIDTH) * mh_g.astype(f32) * jax.nn.sigmoid(om.astype(f32))
    mix = jnp.concatenate([att.reshape(B, T, ATT_WIDTH).astype(x.dtype), hn.astype(x.dtype)], axis=-1)
    h = x + jnp.einsum('btm,md->btd', mix, w_out)
    u = jnp.square(jax.nn.relu(jnp.einsum('btd,df->btf', rmsnorm(h, n2_g), w_up)))
    return h + jnp.einsum('btf,fd->btd', u, w_down)


def setup_inputs(seed: int = 0) -> dict:
    key = jax.random.key(seed)
    ks = jax.random.split(key, 18)
    f32 = jnp.float32
    wbuf = min(WIN_MAX, PAST_LEN)
    nrm = jax.random.normal
    x_prompt = nrm(ks[0], (BATCH, SEQ, D_MODEL), f32)
    x_sample = nrm(ks[1], (DEC_BATCH, DEC_SEQ, D_MODEL), f32)
    cache_k = nrm(ks[2], (DEPTH, DEC_BATCH, wbuf, ATT_HEADS, HEAD_DIM), f32)
    cache_v = nrm(ks[3], (DEPTH, DEC_BATCH, wbuf, ATT_HEADS, HEAD_DIM), f32)
    state_C = 0.1 * nrm(ks[4], (DEPTH, DEC_BATCH, MLSTM_HEADS, MLSTM_DQK, MLSTM_DV), f32)
    state_n = 0.5 * nrm(ks[5], (DEPTH, DEC_BATCH, MLSTM_HEADS, MLSTM_DQK), f32)
    state_m = nrm(ks[6], (DEPTH, DEC_BATCH, MLSTM_HEADS), f32)
    norm1_g = 1.0 + 0.02 * nrm(ks[7], (DEPTH, D_MODEL), f32)
    w_in = nrm(ks[8], (DEPTH, D_MODEL, N_IN), f32) * (D_MODEL ** -0.5)
    b_i = 0.1 * nrm(ks[9], (DEPTH, MLSTM_HEADS), f32)
    b_f = 3.0 + 0.1 * nrm(ks[10], (DEPTH, MLSTM_HEADS), f32)
    b_if = jnp.concatenate([b_i, b_f], axis=-1)
    mh_norm_g = 1.0 + 0.02 * nrm(ks[11], (DEPTH, MLSTM_WIDTH), f32)
    w_out = nrm(ks[12], (DEPTH, MIX, D_MODEL), f32) * (MIX ** -0.5)
    norm2_g = 1.0 + 0.02 * nrm(ks[13], (DEPTH, D_MODEL), f32)
    w_up = nrm(ks[14], (DEPTH, D_MODEL, D_FF), f32) * (D_MODEL ** -0.5)
    w_down = nrm(ks[15], (DEPTH, D_FF, D_MODEL), f32) * (D_FF ** -0.5)
    final_g = 1.0 + 0.02 * nrm(ks[16], (D_MODEL,), f32)
    return {'x_prompt': x_prompt, 'x_sample': x_sample, 'cache_k': cache_k, 'cache_v': cache_v,
            'state_C': state_C, 'state_n': state_n, 'state_m': state_m,
            'norm1_g': norm1_g, 'w_in': w_in, 'b_if': b_if, 'mh_norm_g': mh_norm_g,
            'w_out': w_out, 'norm2_g': norm2_g, 'w_up': w_up, 'w_down': w_down, 'final_g': final_g}


def reference(x_prompt, x_sample, cache_k, cache_v, state_C, state_n, state_m,
              norm1_g, w_in, b_if, mh_norm_g, w_out, norm2_g, w_up, w_down, final_g):
    wbuf = cache_k.shape[2]
    src_off = PAST_LEN - wbuf
    keep_p = min(WIN_MAX, SEQ)
    n_blocks = SEQ // Q_BLOCK
    pos_p = jnp.arange(SEQ, dtype=jnp.int32)
    pos_s = PAST_LEN + jnp.arange(DEC_SEQ, dtype=jnp.int32)
    xp, xs = x_prompt, x_sample
    kp_l, vp_l, Cp_l, np_l, mp_l = [], [], [], [], []
    ks_l, vs_l, Cs_l, ns_l, ms_l = [], [], [], [], []
    for l in range(DEPTH):
        qa, ka, va, qm, km, vm, om, ig, fg = in_proj(rmsnorm(xp, norm1_g[l]), w_in[l], b_if[l])
        qa = rope(qa, pos_p)
        ka = rope(ka, pos_p)
        qb = qa.reshape(BATCH, n_blocks, Q_BLOCK, ATT_HEADS, HEAD_DIM).swapaxes(0, 1)
        pb = pos_p.reshape(n_blocks, Q_BLOCK)
        att = lax.map(lambda a: dilated_attention(a[0], ka, va, a[1], 0), (qb, pb))
        att = att.swapaxes(0, 1).reshape(BATCH, SEQ, ATT_HEADS, HEAD_DIM)
        C0 = jnp.zeros((BATCH, MLSTM_HEADS, MLSTM_DQK, MLSTM_DV), jnp.float32)
        n0 = jnp.zeros((BATCH, MLSTM_HEADS, MLSTM_DQK), jnp.float32)
        m0 = jnp.zeros((BATCH, MLSTM_HEADS), jnp.float32)
        hm, Cp, n_p, mp = mlstm_chunkwise(qm, km, vm, ig, fg, C0, n0, m0)
        xp = merge_and_ffn(xp, att, hm, om, mh_norm_g[l], w_out[l], norm2_g[l], w_up[l], w_down[l])
        kp_l.append(ka[:, SEQ - keep_p:])
        vp_l.append(va[:, SEQ - keep_p:])
        Cp_l.append(Cp)
        np_l.append(n_p)
        mp_l.append(mp)
        qa, ka, va, qm, km, vm, om, ig, fg = in_proj(rmsnorm(xs, norm1_g[l]), w_in[l], b_if[l])
        qa = rope(qa, pos_s)
        ka = rope(ka, pos_s)
        k_src = jnp.concatenate([cache_k[l].astype(ka.dtype), ka], axis=1)
        v_src = jnp.concatenate([cache_v[l].astype(va.dtype), va], axis=1)
        att = dilated_attention(qa, k_src, v_src, pos_s, src_off)
        hm, Cs, n_s, ms = mlstm_chunkwise(qm, km, vm, ig, fg, state_C[l], state_n[l], state_m[l])
        xs = merge_and_ffn(xs, att, hm, om, mh_norm_g[l], w_out[l], norm2_g[l], w_up[l], w_down[l])
        ks_l.append(ka)
        vs_l.append(va)
        Cs_l.append(Cs)
        ns_l.append(n_s)
        ms_l.append(ms)
    y_prompt = rmsnorm(xp, final_g)
    y_sample = rmsnorm(xs, final_g)
    k_prompt = jnp.stack(kp_l, axis=0)
    v_prompt = jnp.stack(vp_l, axis=0)
    C_prompt = jnp.stack(Cp_l, axis=0)
    n_prompt = jnp.stack(np_l, axis=0)
    m_prompt = jnp.stack(mp_l, axis=0)
    k_sample = jnp.stack(ks_l, axis=0)
    v_sample = jnp.stack(vs_l, axis=0)
    C_sample = jnp.stack(Cs_l, axis=0)
    n_sample = jnp.stack(ns_l, axis=0)
    m_sample = jnp.stack(ms_l, axis=0)
    return (y_prompt, y_sample, k_prompt, v_prompt, C_prompt, n_prompt, m_prompt,
            k_sample, v_sample, C_sample, n_sample, m_sample)
```

```python
import functools

import jax
import jax.numpy as jnp
from jax import lax
from jax.experimental import pallas as pl
from jax.experimental.pallas import tpu as pltpu

F32 = jnp.float32
BF16 = jnp.bfloat16

D_MODEL = 4096
HEAD_DIM = 128
ATT_HEADS = 8
ATT_WIDTH = ATT_HEADS * HEAD_DIM
MLSTM_HEADS = 6
MLSTM_DQK = 256
MLSTM_DV = 512
MLSTM_WIDTH = MLSTM_HEADS * MLSTM_DV
N_MAIN = 3 * ATT_WIDTH + 2 * MLSTM_HEADS * MLSTM_DQK + 2 * MLSTM_WIDTH
GATE_CAP = 15.0
DILATED_CONFIGS = ((128, 1), (512, 4), (2048, 16))
WIN_MAX = 2048
ROPE_THETA = 10000.0
EPS = 1e-6
PAST_LEN = 8192
NEG_INF = float("-inf")

LANES = 128
V7X_VMEM_BYTES = 64 * 1024 * 1024
Q_TILE = 128


def _vmem_limit(block_bytes, scratch_bytes, temp_bytes):
    need = 2 * sum(block_bytes) + scratch_bytes + temp_bytes
    return int(min(max(need, 16 * 1024 * 1024), V7X_VMEM_BYTES - 4 * 1024 * 1024))


def _nbytes(shape, dtype):
    n = 1
    for s in shape:
        n *= s
    return n * jnp.dtype(dtype).itemsize


def _log_sigmoid(x):
    return jnp.minimum(x, 0.0) - jnp.log1p(jnp.exp(-jnp.abs(x)))


def _sigmoid(x):
    return 1.0 / (1.0 + jnp.exp(-x))


def _rms_scale(x):
    return lax.rsqrt(jnp.mean(x * x, axis=-1, keepdims=True) + EPS)


def _inproj_kernel(x_ref, g_ref, w_ref, wg_ref, bias_ref, cos_ref, sin_ref,
                   z_ref, gate_ref, xn_ref, *, n_rope_tiles, tn):
    j = pl.program_id(1)

    @pl.when(j == 0)
    def _():
        x = x_ref[...]
        xn = (x * _rms_scale(x) * g_ref[...]).astype(BF16)
        xn_ref[...] = xn
        pre = jnp.dot(xn, wg_ref[...], preferred_element_type=F32) + bias_ref[...]
        gate_ref[...] = GATE_CAP * jnp.tanh(pre / GATE_CAP)

    acc = jnp.dot(xn_ref[...], w_ref[...], preferred_element_type=F32)

    @pl.when(j < n_rope_tiles)
    def _():
        c = cos_ref[...]
        s = sin_ref[...]
        for t in range(tn // HEAD_DIM):
            a = acc[:, t * HEAD_DIM:(t + 1) * HEAD_DIM]
            z_ref[:, t * HEAD_DIM:(t + 1) * HEAD_DIM] = a * c + pltpu.roll(a, HEAD_DIM // 2, 1) * s

    @pl.when(j >= n_rope_tiles)
    def _():
        z_ref[...] = acc


def _in_proj(x2d, gain, w_main, w_gate, bias, cosf, sinf, *, tm, tn, pos_tiles):
    M, D = x2d.shape
    N = w_main.shape[1]
    blocks = [_nbytes((tm, D), F32), _nbytes((D, tn), BF16), _nbytes((D, LANES), BF16),
              2 * _nbytes((tm, LANES), F32), _nbytes((tm, tn), F32), _nbytes((tm, LANES), F32)]
    return pl.pallas_call(
        functools.partial(_inproj_kernel, n_rope_tiles=2 * ATT_WIDTH // tn, tn=tn),
        grid=(M // tm, N // tn),
        in_specs=[
            pl.BlockSpec((tm, D), lambda i, j: (i, 0)),
            pl.BlockSpec((1, D), lambda i, j: (0, 0)),
            pl.BlockSpec((D, tn), lambda i, j: (0, j)),
            pl.BlockSpec((D, LANES), lambda i, j: (0, 0)),
            pl.BlockSpec((1, LANES), lambda i, j: (0, 0)),
            pl.BlockSpec((tm, LANES), lambda i, j: (i % pos_tiles, 0)),
            pl.BlockSpec((tm, LANES), lambda i, j: (i % pos_tiles, 0)),
        ],
        out_specs=[pl.BlockSpec((tm, tn), lambda i, j: (i, j)),
                   pl.BlockSpec((tm, LANES), lambda i, j: (i, 0))],
        out_shape=[jax.ShapeDtypeStruct((M, N), F32), jax.ShapeDtypeStruct((M, LANES), F32)],
        scratch_shapes=[pltpu.VMEM((tm, D), BF16)],
        compiler_params=pltpu.CompilerParams(
            dimension_semantics=("arbitrary", "arbitrary"),
            vmem_limit_bytes=_vmem_limit(blocks, _nbytes((tm, D), BF16),
                                         _nbytes((tm, D), F32) + 2 * _nbytes((tm, tn), F32))),
        name="in_proj",
    )(x2d, gain, w_main, w_gate, bias, cosf, sinf)


def _attn_prompt_kernel(q_ref, k_ref, v_ref, o_ref, m_s, l_s, acc_s, *, T):
    scale = HEAD_DIM ** -0.5
    row = lax.broadcasted_iota(jnp.int32, (Q_TILE, Q_TILE), 0)
    col = lax.broadcasted_iota(jnp.int32, (Q_TILE, Q_TILE), 1)
    diag_mask = col <= row
    prev_mask = col >= row
    nt = (((1,), (1,)), ((), ()))

    for ci, (_, r) in enumerate(DILATED_CONFIGS):
        tiles_per_class = T // (r * Q_TILE)
        shift = tiles_per_class.bit_length() - 1

        def rows(ref, start, r=r):
            if r == 1:
                return ref[pl.ds(start, Q_TILE), :]
            return ref[pl.ds(start, Q_TILE, stride=r), :]

        def put(ref, start, val, r=r):
            if r == 1:
                ref[pl.ds(start, Q_TILE), :] = val
            else:
                ref[pl.ds(start, Q_TILE, stride=r), :] = val

        def body(t, carry, r=r, ci=ci, tpc=tiles_per_class, shift=shift, rows=rows, put=put):
            cls = lax.shift_right_logical(t, shift)
            u0 = (t & (tpc - 1)) * Q_TILE
            start = cls + r * u0
            pstart = cls + r * jnp.maximum(u0 - Q_TILE, 0)
            has_prev = u0 > 0
            q = (rows(q_ref, start) * scale).astype(BF16)
            kd = rows(k_ref, start).astype(BF16)
            kp = rows(k_ref, pstart).astype(BF16)
            vd = rows(v_ref, start).astype(BF16)
            vp = rows(v_ref, pstart).astype(BF16)
            sd = lax.dot_general(q, kd, nt, preferred_element_type=F32)
            sp = lax.dot_general(q, kp, nt, preferred_element_type=F32)
            sd = jnp.where(diag_mask, sd, NEG_INF)
            sp = jnp.where(jnp.logical_and(prev_mask, has_prev), sp, NEG_INF)
            mt = jnp.maximum(jnp.max(sd, axis=1, keepdims=True), jnp.max(sp, axis=1, keepdims=True))
            if ci == 0:
                m_new = jnp.broadcast_to(mt, (Q_TILE, LANES))
            else:
                m_old = rows(m_s, start)
                m_new = jnp.maximum(m_old, mt)
                alpha = jnp.exp(m_old - m_new)
            pd = jnp.exp(sd - m_new)
            pp = jnp.exp(sp - m_new)
            lt = jnp.sum(pd, axis=1, keepdims=True) + jnp.sum(pp, axis=1, keepdims=True)
            ot = (jnp.dot(pd.astype(BF16), vd, preferred_element_type=F32)
                  + jnp.dot(pp.astype(BF16), vp, preferred_element_type=F32))
            if ci == 0:
                l_new = jnp.broadcast_to(lt, (Q_TILE, LANES))
            else:
                l_new = alpha * rows(l_s, start) + lt
                ot = alpha * rows(acc_s, start) + ot
            put(m_s, start, m_new)
            put(l_s, start, l_new)
            put(acc_s, start, ot)
            return carry

        lax.fori_loop(0, T // Q_TILE, body, 0)

    o_ref[...] = (acc_s[...] / l_s[...]).astype(o_ref.dtype)


def _attn_prompt(z3):
    B, T, _ = z3.shape
    assert T % (DILATED_CONFIGS[-1][1] * Q_TILE) == 0
    blk = (None, T, HEAD_DIM)
    blocks = [3 * _nbytes((T, HEAD_DIM), F32), _nbytes((T, HEAD_DIM), BF16)]
    return pl.pallas_call(
        functools.partial(_attn_prompt_kernel, T=T),
        grid=(B, ATT_HEADS),
        in_specs=[pl.BlockSpec(blk, lambda b, h: (b, 0, h)),
                  pl.BlockSpec(blk, lambda b, h: (b, 0, ATT_HEADS + h)),
                  pl.BlockSpec(blk, lambda b, h: (b, 0, 2 * ATT_HEADS + h))],
        out_specs=pl.BlockSpec(blk, lambda b, h: (b, 0, h)),
        out_shape=jax.ShapeDtypeStruct((B, T, ATT_WIDTH), BF16),
        scratch_shapes=[pltpu.VMEM((T, LANES), F32)] * 3,
        compiler_params=pltpu.CompilerParams(
            dimension_semantics=("arbitrary", "arbitrary"),
            vmem_limit_bytes=_vmem_limit(blocks, 3 * _nbytes((T, LANES), F32),
                                         2 * _nbytes((T, LANES), F32))),
        name="attn_prompt",
    )(z3, z3, z3)


def _mlstm_prompt_kernel(q_ref, k_ref, v_ref, om_ref, lir_ref, fgr_ref, lic_ref, fgc_ref, mhg_ref,
                         hn_ref, c_out, n_out, m_out, c_s, n_s, m_s, *, L):
    c = pl.program_id(2)

    @pl.when(c == 0)
    def _():
        c_s[...] = jnp.zeros_like(c_s)
        n_s[...] = jnp.zeros_like(n_s)
        m_s[...] = jnp.zeros_like(m_s)

    li_row = lir_ref[...]
    li_col = lic_ref[...]
    lf_row = _log_sigmoid(fgr_ref[...])
    lf_col = _log_sigmoid(fgc_ref[...])
    r_idx = lax.broadcasted_iota(jnp.int32, (L, L), 0)
    c_idx = lax.broadcasted_iota(jnp.int32, (L, L), 1)
    causal = c_idx <= r_idx
    b_col = jnp.sum(jnp.where(causal, lf_row, 0.0), axis=1, keepdims=True)
    b_row = jnp.sum(jnp.where(r_idx <= c_idx, lf_col, 0.0), axis=0, keepdims=True)

    m_prev = m_s[...]
    dlog = jnp.where(causal, b_col - b_row + li_row, NEG_INF)
    inter = b_col + m_prev
    m_t = jnp.maximum(inter, jnp.max(dlog, axis=1, keepdims=True))
    dmat = jnp.exp(dlog - m_t)
    a = jnp.exp(inter - m_t)

    qf = q_ref[...] * (MLSTM_DQK ** -0.5)
    qb = qf.astype(BF16)
    kf = k_ref[...]
    kb = kf.astype(BF16)
    vb = v_ref[...].astype(BF16)
    s = lax.dot_general(qb, kb, (((1,), (1,)), ((), ())), preferred_element_type=F32)
    sw = s * dmat
    cb = c_s[...].astype(BF16)
    num = (jnp.dot(sw.astype(BF16), vb, preferred_element_type=F32)
           + a * jnp.dot(qb, cb, preferred_element_type=F32))
    den = (jnp.sum(sw, axis=1, keepdims=True)
           + a * jnp.sum(qf * n_s[...], axis=1, keepdims=True))
    h = num / jnp.maximum(jnp.abs(den), jnp.exp(-m_t))

    m_new = m_t[L - 1:L, :]
    b_last = b_col[L - 1:L, :]
    ws = jnp.exp(b_last - b_col + li_col - m_new)
    decay = jnp.exp(b_last + m_prev - m_new)
    kw = kf * ws
    kwt = kw.T.astype(BF16)
    c_s[...] = decay * c_s[...] + jnp.dot(kwt, vb, preferred_element_type=F32)
    n_s[...] = decay * n_s[...] + jnp.sum(kw, axis=0, keepdims=True)
    m_s[...] = m_new

    hn = h * _rms_scale(h) * mhg_ref[...] * _sigmoid(om_ref[...])
    hn_ref[...] = hn.astype(hn_ref.dtype)

    @pl.when(c == pl.num_programs(2) - 1)
    def _():
        c_out[...] = c_s[...]
        n_out[...] = n_s[...]
        m_out[...] = m_s[...]


def _mlstm_prompt(z3, li_row, fg_row, li_col, fg_col, mhg, *, L):
    B, T, _ = z3.shape
    H = MLSTM_HEADS
    q_off = 3 * ATT_WIDTH // MLSTM_DQK
    k_off = q_off + H
    v_off = (3 * ATT_WIDTH + 2 * H * MLSTM_DQK) // MLSTM_DV
    o_off = v_off + H
    blocks = [2 * _nbytes((L, MLSTM_DQK), F32), 2 * _nbytes((L, MLSTM_DV), F32),
              4 * _nbytes((L, LANES), F32), _nbytes((L, MLSTM_DV), BF16),
              _nbytes((MLSTM_DQK, MLSTM_DV), F32)]
    return pl.pallas_call(
        functools.partial(_mlstm_prompt_kernel, L=L),
        grid=(B, H, T // L),
        in_specs=[
            pl.BlockSpec((None, L, MLSTM_DQK), lambda b, h, c: (b, c, q_off + h)),
            pl.BlockSpec((None, L, MLSTM_DQK), lambda b, h, c: (b, c, k_off + h)),
            pl.BlockSpec((None, L, MLSTM_DV), lambda b, h, c: (b, c, v_off + h)),
            pl.BlockSpec((None, L, MLSTM_DV), lambda b, h, c: (b, c, o_off + h)),
            pl.BlockSpec((None, None, 1, L), lambda b, h, c: (b, h, 0, c)),
            pl.BlockSpec((None, None, 1, L), lambda b, h, c: (b, h, 0, c)),
            pl.BlockSpec((None, None, L, 1), lambda b, h, c: (b, h, c, 0)),
            pl.BlockSpec((None, None, L, 1), lambda b, h, c: (b, h, c, 0)),
            pl.BlockSpec((1, MLSTM_DV), lambda b, h, c: (0, h)),
        ],
        out_specs=[
            pl.BlockSpec((None, L, MLSTM_DV), lambda b, h, c: (b, c, h)),
            pl.BlockSpec((None, None, MLSTM_DQK, MLSTM_DV), lambda b, h, c: (b, h, 0, 0)),
            pl.BlockSpec((None, None, 1, MLSTM_DQK), lambda b, h, c: (b, h, 0, 0)),
            pl.BlockSpec((None, None, 1, 1), lambda b, h, c: (b, h, 0, 0)),
        ],
        out_shape=[
            jax.ShapeDtypeStruct((B, T, MLSTM_WIDTH), BF16),
            jax.ShapeDtypeStruct((B, H, MLSTM_DQK, MLSTM_DV), F32),
            jax.ShapeDtypeStruct((B, H, 1, MLSTM_DQK), F32),
            jax.ShapeDtypeStruct((B, H, 1, 1), F32),
        ],
        scratch_shapes=[pltpu.VMEM((MLSTM_DQK, MLSTM_DV), F32),
                        pltpu.VMEM((1, MLSTM_DQK), F32),
                        pltpu.VMEM((1, 1), F32)],
        compiler_params=pltpu.CompilerParams(
            dimension_semantics=("arbitrary", "arbitrary", "arbitrary"),
            vmem_limit_bytes=_vmem_limit(blocks, _nbytes((MLSTM_DQK, MLSTM_DV), F32),
                                         8 * _nbytes((L, MLSTM_DV), F32) + 8 * _nbytes((L, L), F32))),
        name="mlstm_prompt",
    )(z3, z3, z3, z3, li_row, fg_row, li_col, fg_col, mhg)


def _outproj_kernel(att_ref, hn_ref, wa_ref, wm_ref, x_ref, h_ref):
    h_ref[...] = (x_ref[...]
                  + jnp.dot(att_ref[...], wa_ref[...], preferred_element_type=F32)
                  + jnp.dot(hn_ref[...], wm_ref[...], preferred_element_type=F32))


def _out_proj(att2d, hn2d, wa, wm, x2d, *, tm, tn):
    M, D = x2d.shape
    blocks = [_nbytes((tm, ATT_WIDTH), BF16), _nbytes((tm, MLSTM_WIDTH), BF16),
              _nbytes((ATT_WIDTH, tn), BF16), _nbytes((MLSTM_WIDTH, tn), BF16),
              2 * _nbytes((tm, tn), F32)]
    return pl.pallas_call(
        _outproj_kernel,
        grid=(M // tm, D // tn),
        in_specs=[
            pl.BlockSpec((tm, ATT_WIDTH), lambda i, j: (i, 0)),
            pl.BlockSpec((tm, MLSTM_WIDTH), lambda i, j: (i, 0)),
            pl.BlockSpec((ATT_WIDTH, tn), lambda i, j: (0, j)),
            pl.BlockSpec((MLSTM_WIDTH, tn), lambda i, j: (0, j)),
            pl.BlockSpec((tm, tn), lambda i, j: (i, j)),
        ],
        out_specs=pl.BlockSpec((tm, tn), lambda i, j: (i, j)),
        out_shape=jax.ShapeDtypeStruct((M, D), F32),
        compiler_params=pltpu.CompilerParams(
            dimension_semantics=("arbitrary", "arbitrary"),
            vmem_limit_bytes=_vmem_limit(blocks, 0, 2 * _nbytes((tm, tn), F32))),
        name="out_proj",
    )(att2d, hn2d, wa, wm, x2d)


def _upproj_kernel(x_ref, g_ref, w_ref, u_ref, xn_ref):
    @pl.when(pl.program_id(1) == 0)
    def _():
        x = x_ref[...]
        xn_ref[...] = (x * _rms_scale(x) * g_ref[...]).astype(BF16)

    acc = jnp.dot(xn_ref[...], w_ref[...], preferred_element_type=F32)
    u_ref[...] = jnp.square(jnp.maximum(acc, 0.0)).astype(u_ref.dtype)


def _up_proj(h2d, gain, w_up, *, tm, tn):
    M, D = h2d.shape
    N = w_up.shape[1]
    blocks = [_nbytes((tm, D), F32), _nbytes((D, tn), BF16), _nbytes((tm, tn), BF16)]
    return pl.pallas_call(
        _upproj_kernel,
        grid=(M // tm, N // tn),
        in_specs=[pl.BlockSpec((tm, D), lambda i, j: (i, 0)),
                  pl.BlockSpec((1, D), lambda i, j: (0, 0)),
                  pl.BlockSpec((D, tn), lambda i, j: (0, j))],
        out_specs=pl.BlockSpec((tm, tn), lambda i, j: (i, j)),
        out_shape=jax.ShapeDtypeStruct((M, N), BF16),
        scratch_shapes=[pltpu.VMEM((tm, D), BF16)],
        compiler_params=pltpu.CompilerParams(
            dimension_semantics=("arbitrary", "arbitrary"),
            vmem_limit_bytes=_vmem_limit(blocks, _nbytes((tm, D), BF16),
                                         _nbytes((tm, D), F32) + 2 * _nbytes((tm, tn), F32))),
        name="up_proj",
    )(h2d, gain, w_up)


def _downproj_kernel(u_ref, w_ref, h_ref, g_ref, y_ref):
    k = pl.program_id(1)
    d = jnp.dot(u_ref[...], w_ref[...], preferred_element_type=F32)

    @pl.when(k == 0)
    def _():
        y_ref[...] = h_ref[...] + d

    @pl.when(k > 0)
    def _():
        y_ref[...] += d

    @pl.when(k == pl.num_programs(1) - 1)
    def _():
        y = y_ref[...]
        y_ref[...] = y * _rms_scale(y) * g_ref[...]


def _down_proj(u2d, w_down, h2d, gain, *, tm, tk):
    M, D = h2d.shape
    K = u2d.shape[1]
    blocks = [_nbytes((tm, tk), BF16), _nbytes((tk, D), BF16), 2 * _nbytes((tm, D), F32)]
    return pl.pallas_call(
        _downproj_kernel,
        grid=(M // tm, K // tk),
        in_specs=[pl.BlockSpec((tm, tk), lambda i, k: (i, k)),
                  pl.BlockSpec((tk, D), lambda i, k: (k, 0)),
                  pl.BlockSpec((tm, D), lambda i, k: (i, 0)),
                  pl.BlockSpec((1, D), lambda i, k: (0, 0))],
        out_specs=pl.BlockSpec((tm, D), lambda i, k: (i, 0)),
        out_shape=jax.ShapeDtypeStruct((M, D), F32),
        compiler_params=pltpu.CompilerParams(
            dimension_semantics=("arbitrary", "arbitrary"),
            vmem_limit_bytes=_vmem_limit(blocks, 0, 2 * _nbytes((tm, D), F32))),
        name="down_proj",
    )(u2d, w_down, h2d, gain)


def _attn_sample_kernel(q_ref, kn_ref, vn_ref, k1_ref, k4_ref, k16_ref, v1_ref, v4_ref, v16_ref, o_ref):
    scale = HEAD_DIM ** -0.5
    head_r = lax.broadcasted_iota(jnp.int32, (ATT_HEADS, ATT_WIDTH), 0)
    head_c = lax.broadcasted_iota(jnp.int32, (ATT_HEADS, ATT_WIDTH), 1) // HEAD_DIM
    own = head_r == head_c
    qr = (q_ref[...] * scale).astype(BF16).astype(F32)
    q_bd = jnp.where(own, qr, 0.0).astype(BF16)
    knb = kn_ref[...].astype(BF16).astype(F32)
    vnb = vn_ref[...].astype(BF16).astype(F32)
    s0 = jnp.sum(jnp.where(own, qr * knb, 0.0), axis=1, keepdims=True)
    nt = (((1,), (1,)), ((), ()))

    ms, dens, nums = [], [], []
    for kt_ref, vt_ref in ((k1_ref, v1_ref), (k4_ref, v4_ref), (k16_ref, v16_ref)):
        kt = kt_ref[...].astype(BF16)
        vt = vt_ref[...].astype(BF16)
        s = lax.dot_general(q_bd, kt, nt, preferred_element_type=F32)
        m = jnp.maximum(jnp.max(s, axis=1, keepdims=True), s0)
        p = jnp.exp(s - m)
        p0 = jnp.exp(s0 - m)
        dens.append(jnp.sum(p, axis=1, keepdims=True) + p0)
        nums.append(jnp.dot(p.astype(BF16), vt, preferred_element_type=F32)
                    + p0.astype(BF16).astype(F32) * vnb)
        ms.append(m)
    m_all = jnp.maximum(jnp.maximum(ms[0], ms[1]), ms[2])
    num = jnp.zeros((ATT_HEADS, ATT_WIDTH), F32)
    den = jnp.zeros((ATT_HEADS, 1), F32)
    for m, d, n in zip(ms, dens, nums):
        wgt = jnp.exp(m - m_all)
        num = num + wgt * n
        den = den + wgt * d
    out = jnp.sum(jnp.where(own, num / den, 0.0), axis=0, keepdims=True)
    o_ref[...] = out.astype(o_ref.dtype)


def _attn_sample(zs3, cache_k, cache_v):
    Bd, W, _ = cache_k.shape
    in_specs = [pl.BlockSpec((None, 1, ATT_WIDTH), lambda b: (b, 0, 0)),
                pl.BlockSpec((None, 1, ATT_WIDTH), lambda b: (b, 0, 1)),
                pl.BlockSpec((None, 1, ATT_WIDTH), lambda b: (b, 0, 2))]
    views_k, views_v = [], []
    for (w, r) in DILATED_CONFIGS:
        assert w // r == Q_TILE and W % w == 0 and PAST_LEN >= W >= w
        views_k.append(cache_k.reshape(Bd, W // r, r * ATT_WIDTH))
        views_v.append(cache_v.reshape(Bd, W // r, r * ATT_WIDTH))
    for _ in range(2):
        for (w, r) in DILATED_CONFIGS:
            last = W // w - 1
            in_specs.append(pl.BlockSpec((None, Q_TILE, ATT_WIDTH), lambda b, last=last: (b, last, 0)))
    blocks = [3 * _nbytes((1, ATT_WIDTH), F32), 6 * _nbytes((Q_TILE, ATT_WIDTH), F32)]
    return pl.pallas_call(
        _attn_sample_kernel,
        grid=(Bd,),
        in_specs=in_specs,
        out_specs=pl.BlockSpec((None, 1, ATT_WIDTH), lambda b: (b, 0, 0)),
        out_shape=jax.ShapeDtypeStruct((Bd, 1, ATT_WIDTH), BF16),
        compiler_params=pltpu.CompilerParams(
            dimension_semantics=("arbitrary",),
            vmem_limit_bytes=_vmem_limit(blocks, 0, 8 * _nbytes((Q_TILE, ATT_WIDTH), F32))),
        name="attn_sample",
    )(zs3, zs3, zs3, *views_k, *views_v)


def _mlstm_sample_kernel(q_ref, k_ref, v_ref, om_ref, g_ref, c0_ref, n0_ref, m0_ref, mhg_ref,
                         hn_ref, c_out, n_out, m_out):
    H = MLSTM_HEADS
    for h in range(H):
        qf = q_ref[h:h + 1, :] * (MLSTM_DQK ** -0.5)
        kf = k_ref[h:h + 1, :]
        vf = v_ref[h:h + 1, :]
        li = g_ref[:, h:h + 1]
        lf = _log_sigmoid(g_ref[:, H + h:H + h + 1])
        m0 = m0_ref[:, h:h + 1]
        c0 = c0_ref[h]
        n0 = n0_ref[h:h + 1, :]

        inter = lf + m0
        m_t = jnp.maximum(inter, li)
        qb = qf.astype(BF16)
        qk = jnp.sum(qb.astype(F32) * kf.astype(BF16).astype(F32), axis=1, keepdims=True)
        sw = qk * jnp.exp(li - m_t)
        a = jnp.exp(inter - m_t)
        qc = jnp.dot(jnp.broadcast_to(qb, (8, MLSTM_DQK)), c0.astype(BF16),
                     preferred_element_type=F32)[0:1, :]
        num = sw * vf + a * qc
        den = sw + a * jnp.sum(qf * n0, axis=1, keepdims=True)
        hh = num / jnp.maximum(jnp.abs(den), jnp.exp(-m_t))

        ws = jnp.exp(li - m_t)
        decay = jnp.exp(inter - m_t)
        kw = kf * ws
        kcol = jnp.broadcast_to(kw, (LANES, MLSTM_DQK)).T[:, 0:1]
        c_out[h] = decay * c0 + kcol * vf
        n_out[h:h + 1, :] = decay * n0 + kw
        m_out[:, h:h + 1] = m_t

        hn = hh * _rms_scale(hh) * mhg_ref[h:h + 1, :] * _sigmoid(om_ref[h:h + 1, :])
        hn_ref[h:h + 1, :] = hn.astype(hn_ref.dtype)


def _mlstm_sample(qs, ks, vs, oms, gs, c0, n0, m0, mhg):
    Bd = qs.shape[0]
    H = MLSTM_HEADS
    blocks = [2 * _nbytes((8, MLSTM_DQK), F32), 2 * _nbytes((8, MLSTM_DV), F32),
              2 * _nbytes((H, MLSTM_DQK, MLSTM_DV), F32), 4 * _nbytes((8, MLSTM_DV), F32)]
    return pl.pallas_call(
        _mlstm_sample_kernel,
        grid=(Bd,),
        in_specs=[
            pl.BlockSpec((None, H, MLSTM_DQK), lambda b: (b, 0, 0)),
            pl.BlockSpec((None, H, MLSTM_DQK), lambda b: (b, 0, 0)),
            pl.BlockSpec((None, H, MLSTM_DV), lambda b: (b, 0, 0)),
            pl.BlockSpec((None, H, MLSTM_DV), lambda b: (b, 0, 0)),
            pl.BlockSpec((None, 1, LANES), lambda b: (b, 0, 0)),
            pl.BlockSpec((None, H, MLSTM_DQK, MLSTM_DV), lambda b: (b, 0, 0, 0)),
            pl.BlockSpec((None, H, MLSTM_DQK), lambda b: (b, 0, 0)),
            pl.BlockSpec((None, 1, H), lambda b: (b, 0, 0)),
            pl.BlockSpec((H, MLSTM_DV), lambda b: (0, 0)),
        ],
        out_specs=[
            pl.BlockSpec((None, H, MLSTM_DV), lambda b: (b, 0, 0)),
            pl.BlockSpec((None, H, MLSTM_DQK, MLSTM_DV), lambda b: (b, 0, 0, 0)),
            pl.BlockSpec((None, H, MLSTM_DQK), lambda b: (b, 0, 0)),
            pl.BlockSpec((None, 1, H), lambda b: (b, 0, 0)),
        ],
        out_shape=[
            jax.ShapeDtypeStruct((Bd, H, MLSTM_DV), BF16),
            jax.ShapeDtypeStruct((Bd, H, MLSTM_DQK, MLSTM_DV), F32),
            jax.ShapeDtypeStruct((Bd, H, MLSTM_DQK), F32),
            jax.ShapeDtypeStruct((Bd, 1, H), F32),
        ],
        compiler_params=pltpu.CompilerParams(
            dimension_semantics=("arbitrary",),
            vmem_limit_bytes=_vmem_limit(blocks, 0, 4 * _nbytes((MLSTM_DQK, MLSTM_DV), F32))),
        name="mlstm_sample",
    )(qs, ks, vs, oms, gs, c0, n0, m0, mhg)


def _rope_tables(pos):
    half = HEAD_DIM // 2
    inv = ROPE_THETA ** (-jnp.arange(half, dtype=F32) / half)
    ang = pos.astype(F32)[:, None] * inv[None, :]
    cos = jnp.cos(ang)
    sin = jnp.sin(ang)
    return jnp.concatenate([cos, cos], axis=-1), jnp.concatenate([-sin, sin], axis=-1)


def _row_tile(M, cap):
    tm = min(M, cap)
    assert M % tm == 0
    return tm


def kernel(x_prompt, x_sample, cache_k, cache_v, state_C, state_n, state_m,
           norm1_g, w_in, b_if, mh_norm_g, w_out, norm2_g, w_up, w_down, final_g):
    B, T, D = x_prompt.shape
    Bd, Td, _ = x_sample.shape
    depth = w_in.shape[0]
    assert depth == 1 and Td == 1 and D == D_MODEL
    wbuf = cache_k.shape[2]
    keep = min(WIN_MAX, T)
    H = MLSTM_HEADS

    w_main = w_in[0, :, :N_MAIN].astype(BF16)
    w_gate = jnp.pad(w_in[0, :, N_MAIN:], ((0, 0), (0, LANES - 2 * H))).astype(BF16)
    bias = jnp.pad(b_if[0], (0, LANES - 2 * H)).reshape(1, LANES)
    wo_a = w_out[0, :ATT_WIDTH].astype(BF16)
    wo_m = w_out[0, ATT_WIDTH:].astype(BF16)
    wu = w_up[0].astype(BF16)
    wd = w_down[0].astype(BF16)
    g1 = norm1_g[0].reshape(1, D)
    g2 = norm2_g[0].reshape(1, D)
    gf = final_g.reshape(1, D)
    mhg = mh_norm_g[0]

    cos_p, sin_p = _rope_tables(jnp.arange(T, dtype=jnp.int32))
    cos_s, sin_s = _rope_tables(jnp.full((Bd,), PAST_LEN, dtype=jnp.int32))

    xp2 = x_prompt.reshape(B * T, D)
    tm = _row_tile(T, 512)
    z, gates = _in_proj(xp2, g1, w_main, w_gate, bias, cos_p, sin_p, tm=tm, tn=512, pos_tiles=T // tm)
    z3 = z.reshape(B, T, N_MAIN)
    att = _attn_prompt(z3)

    g3 = gates.reshape(B, T, LANES)
    li_row = jnp.swapaxes(g3[:, :, :H], 1, 2).reshape(B, H, 1, T)
    fg_row = jnp.swapaxes(g3[:, :, H:2 * H], 1, 2).reshape(B, H, 1, T)
    li_col = li_row.reshape(B, H, T, 1)
    fg_col = fg_row.reshape(B, H, T, 1)
    hn, c_p, n_p, m_p = _mlstm_prompt(z3, li_row, fg_row, li_col, fg_col, mhg.reshape(1, MLSTM_WIDTH),
                                      L=min(T, 256))

    h_p = _out_proj(att.reshape(B * T, ATT_WIDTH), hn.reshape(B * T, MLSTM_WIDTH), wo_a, wo_m, xp2,
                    tm=tm, tn=1024)
    u_p = _up_proj(h_p, g2, wu, tm=tm, tn=512)
    y_p = _down_proj(u_p, wd, h_p, gf, tm=tm, tk=512)

    k_prompt = z3[:, T - keep:, ATT_WIDTH:2 * ATT_WIDTH].reshape(1, B, keep, ATT_HEADS, HEAD_DIM)
    v_prompt = z3[:, T - keep:, 2 * ATT_WIDTH:3 * ATT_WIDTH].reshape(1, B, keep, ATT_HEADS, HEAD_DIM)

    xs2 = x_sample.reshape(Bd, D)
    zs, gates_s = _in_proj(xs2, g1, w_main, w_gate, bias, cos_s, sin_s, tm=Bd, tn=512, pos_tiles=1)
    zs3 = zs.reshape(Bd, 1, N_MAIN)
    att_s = _attn_sample(zs3, cache_k[0].reshape(Bd, wbuf, ATT_WIDTH), cache_v[0].reshape(Bd, wbuf, ATT_WIDTH))

    o = 3 * ATT_WIDTH
    qs = zs[:, o:o + H * MLSTM_DQK].reshape(Bd, H, MLSTM_DQK)
    o += H * MLSTM_DQK
    ks = zs[:, o:o + H * MLSTM_DQK].reshape(Bd, H, MLSTM_DQK)
    o += H * MLSTM_DQK
    vs = zs[:, o:o + MLSTM_WIDTH].reshape(Bd, H, MLSTM_DV)
    o += MLSTM_WIDTH
    oms = zs[:, o:o + MLSTM_WIDTH].reshape(Bd, H, MLSTM_DV)
    hn_s, c_s, n_s, m_s = _mlstm_sample(qs, ks, vs, oms, gates_s.reshape(Bd, 1, LANES),
                                        state_C[0], state_n[0], state_m[0].reshape(Bd, 1, H),
                                        mhg.reshape(H, MLSTM_DV))

    h_s = _out_proj(att_s.reshape(Bd, ATT_WIDTH), hn_s.reshape(Bd, MLSTM_WIDTH), wo_a, wo_m, xs2,
                    tm=Bd, tn=1024)
    u_s = _up_proj(h_s, g2, wu, tm=Bd, tn=512)
    y_s = _down_proj(u_s, wd, h_s, gf, tm=Bd, tk=512)

    k_sample = zs[:, ATT_WIDTH:2 * ATT_WIDTH].reshape(1, Bd, 1, ATT_HEADS, HEAD_DIM)
    v_sample = zs[:, 2 * ATT_WIDTH:3 * ATT_WIDTH].reshape(1, Bd, 1, ATT_HEADS, HEAD_DIM)

    return (y_p.reshape(B, T, D), y_s.reshape(Bd, 1, D),
            k_prompt, v_prompt,
            c_p[None], n_p.reshape(1, B, H, MLSTM_DQK), m_p.reshape(1, B, H),
            k_sample, v_sample,
            c_s[None], n_s[None], m_s.reshape(1, Bd, H))
```

```python
import functools

import jax
import jax.numpy as jnp
from jax import lax
from jax.experimental import pallas as pl
from jax.experimental.pallas import tpu as pltpu

F32 = jnp.float32
BF16 = jnp.bfloat16

D_MODEL = 4096
HEAD_DIM = 128
ATT_HEADS = 8
ATT_WIDTH = ATT_HEADS * HEAD_DIM
MLSTM_HEADS = 6
MLSTM_DQK = 256
MLSTM_DV = 512
MLSTM_WIDTH = MLSTM_HEADS * MLSTM_DV
N_MAIN = 3 * ATT_WIDTH + 2 * MLSTM_HEADS * MLSTM_DQK + 2 * MLSTM_WIDTH
GATE_CAP = 15.0
DILATED_CONFIGS = ((128, 1), (512, 4), (2048, 16))
WIN_MAX = 2048
ROPE_THETA = 10000.0
EPS = 1e-6
PAST_LEN = 8192
NEG_INF = float("-inf")

LANES = 128
V7X_VMEM_BYTES = 64 * 1024 * 1024
Q_TILE = 128
ACC_COLS = 512


def _vmem_limit(block_bytes, scratch_bytes, temp_bytes):
    need = 2 * sum(block_bytes) + scratch_bytes + temp_bytes
    return int(min(max(need, 16 * 1024 * 1024), V7X_VMEM_BYTES - 4 * 1024 * 1024))


def _nbytes(shape, dtype):
    n = 1
    for s in shape:
        n *= s
    return n * jnp.dtype(dtype).itemsize


def _log_sigmoid(x):
    return jnp.minimum(x, 0.0) - jnp.log1p(jnp.exp(-jnp.abs(x)))


def _sigmoid(x):
    return 1.0 / (1.0 + jnp.exp(-x))


def _rms_scale(x):
    return lax.rsqrt(jnp.mean(x * x, axis=-1, keepdims=True) + EPS)


def _cast_kernel(w_ref, o_ref):
    o_ref[...] = w_ref[...].astype(o_ref.dtype)


def _cast_columns_bf16(w, n_cols, *, tk=1024, tn=1024):
    _, K, _ = w.shape
    assert K % tk == 0 and n_cols % tn == 0
    return pl.pallas_call(
        _cast_kernel,
        grid=(K // tk, n_cols // tn),
        in_specs=[pl.BlockSpec((None, tk, tn), lambda i, j: (0, i, j))],
        out_specs=pl.BlockSpec((tk, tn), lambda i, j: (i, j)),
        out_shape=jax.ShapeDtypeStruct((K, n_cols), BF16),
        compiler_params=pltpu.CompilerParams(
            dimension_semantics=("arbitrary", "arbitrary"),
            vmem_limit_bytes=_vmem_limit([_nbytes((tk, tn), F32), _nbytes((tk, tn), BF16)], 0, 0)),
        name="cast_bf16",
    )(w)


def _inproj_kernel(x_ref, g_ref, w_ref, wg_ref, bias_ref, cos_ref, sin_ref,
                   z_ref, gate_ref, xn_ref, *, n_rope_tiles, tn):
    j = pl.program_id(1)

    @pl.when(j == 0)
    def _():
        x = x_ref[...]
        xn = (x * _rms_scale(x) * g_ref[...]).astype(BF16)
        xn_ref[...] = xn
        pre = jnp.dot(xn, wg_ref[...], preferred_element_type=F32) + bias_ref[...]
        gate_ref[...] = GATE_CAP * jnp.tanh(pre / GATE_CAP)

    acc = jnp.dot(xn_ref[...], w_ref[...], preferred_element_type=F32)

    @pl.when(j < n_rope_tiles)
    def _():
        c = cos_ref[...]
        s = sin_ref[...]
        for t in range(tn // HEAD_DIM):
            a = acc[:, t * HEAD_DIM:(t + 1) * HEAD_DIM]
            z_ref[:, t * HEAD_DIM:(t + 1) * HEAD_DIM] = a * c + pltpu.roll(a, HEAD_DIM // 2, 1) * s

    @pl.when(j >= n_rope_tiles)
    def _():
        z_ref[...] = acc


def _in_proj(x2d, gain, w_main, w_gate, bias, cosf, sinf, *, tm, tn, pos_tiles):
    M, D = x2d.shape
    N = w_main.shape[1]
    blocks = [_nbytes((tm, D), F32), _nbytes((D, tn), BF16), _nbytes((D, LANES), BF16),
              2 * _nbytes((tm, LANES), F32), _nbytes((tm, tn), F32), _nbytes((tm, LANES), F32)]
    return pl.pallas_call(
        functools.partial(_inproj_kernel, n_rope_tiles=2 * ATT_WIDTH // tn, tn=tn),
        grid=(M // tm, N // tn),
        in_specs=[
            pl.BlockSpec((tm, D), lambda i, j: (i, 0)),
            pl.BlockSpec((1, D), lambda i, j: (0, 0)),
            pl.BlockSpec((D, tn), lambda i, j: (0, j)),
            pl.BlockSpec((D, LANES), lambda i, j: (0, 0)),
            pl.BlockSpec((1, LANES), lambda i, j: (0, 0)),
            pl.BlockSpec((tm, LANES), lambda i, j: (i % pos_tiles, 0)),
            pl.BlockSpec((tm, LANES), lambda i, j: (i % pos_tiles, 0)),
        ],
        out_specs=[pl.BlockSpec((tm, tn), lambda i, j: (i, j)),
                   pl.BlockSpec((tm, LANES), lambda i, j: (i, 0))],
        out_shape=[jax.ShapeDtypeStruct((M, N), F32), jax.ShapeDtypeStruct((M, LANES), F32)],
        scratch_shapes=[pltpu.VMEM((tm, D), BF16)],
        compiler_params=pltpu.CompilerParams(
            dimension_semantics=("arbitrary", "arbitrary"),
            vmem_limit_bytes=_vmem_limit(blocks, _nbytes((tm, D), BF16),
                                         _nbytes((tm, D), F32) + 2 * _nbytes((tm, tn), F32))),
        name="in_proj",
    )(x2d, gain, w_main, w_gate, bias, cosf, sinf)


def _attn_prompt_kernel(q_ref, k_ref, v_ref, o_ref, m_s, l_s, acc_s, *, T):
    scale = HEAD_DIM ** -0.5
    row = lax.broadcasted_iota(jnp.int32, (Q_TILE, Q_TILE), 0)
    col = lax.broadcasted_iota(jnp.int32, (Q_TILE, Q_TILE), 1)
    diag_mask = col <= row
    prev_mask = col >= row
    nt = (((1,), (1,)), ((), ()))

    for ci, (_, r) in enumerate(DILATED_CONFIGS):
        tiles_per_class = T // (r * Q_TILE)
        shift = tiles_per_class.bit_length() - 1

        def rows(ref, start, r=r):
            if r == 1:
                return ref[pl.ds(start, Q_TILE), :]
            return ref[pl.ds(start, Q_TILE, stride=r), :]

        def put(ref, start, val, r=r):
            if r == 1:
                ref[pl.ds(start, Q_TILE), :] = val
            else:
                ref[pl.ds(start, Q_TILE, stride=r), :] = val

        def body(t, carry, r=r, ci=ci, tpc=tiles_per_class, shift=shift, rows=rows, put=put):
            cls = lax.shift_right_logical(t, shift)
            u0 = (t & (tpc - 1)) * Q_TILE
            start = cls + r * u0
            pstart = cls + r * jnp.maximum(u0 - Q_TILE, 0)
            has_prev = u0 > 0
            q = (rows(q_ref, start) * scale).astype(BF16)
            kd = rows(k_ref, start).astype(BF16)
            kp = rows(k_ref, pstart).astype(BF16)
            vd = rows(v_ref, start).astype(BF16)
            vp = rows(v_ref, pstart).astype(BF16)
            sd = lax.dot_general(q, kd, nt, preferred_element_type=F32)
            sp = lax.dot_general(q, kp, nt, preferred_element_type=F32)
            sd = jnp.where(diag_mask, sd, NEG_INF)
            sp = jnp.where(jnp.logical_and(prev_mask, has_prev), sp, NEG_INF)
            mt = jnp.maximum(jnp.max(sd, axis=1, keepdims=True), jnp.max(sp, axis=1, keepdims=True))
            if ci == 0:
                m_new = jnp.broadcast_to(mt, (Q_TILE, LANES))
            else:
                m_old = rows(m_s, start)
                m_new = jnp.maximum(m_old, mt)
                alpha = jnp.exp(m_old - m_new)
            pd = jnp.exp(sd - m_new)
            pp = jnp.exp(sp - m_new)
            lt = jnp.sum(pd, axis=1, keepdims=True) + jnp.sum(pp, axis=1, keepdims=True)
            ot = (jnp.dot(pd.astype(BF16), vd, preferred_element_type=F32)
                  + jnp.dot(pp.astype(BF16), vp, preferred_element_type=F32))
            if ci == 0:
                l_new = jnp.broadcast_to(lt, (Q_TILE, LANES))
            else:
                l_new = alpha * rows(l_s, start) + lt
                ot = alpha * rows(acc_s, start) + ot
            put(m_s, start, m_new)
            put(l_s, start, l_new)
            put(acc_s, start, ot)
            return carry

        lax.fori_loop(0, T // Q_TILE, body, 0, unroll=2)

    o_ref[...] = (acc_s[...] / l_s[...]).astype(o_ref.dtype)


def _attn_prompt(z3):
    B, T, _ = z3.shape
    assert T % (DILATED_CONFIGS[-1][1] * Q_TILE) == 0
    blk = (None, T, HEAD_DIM)
    blocks = [3 * _nbytes((T, HEAD_DIM), F32), _nbytes((T, HEAD_DIM), BF16)]
    return pl.pallas_call(
        functools.partial(_attn_prompt_kernel, T=T),
        grid=(B, ATT_HEADS),
        in_specs=[pl.BlockSpec(blk, lambda b, h: (b, 0, h)),
                  pl.BlockSpec(blk, lambda b, h: (b, 0, ATT_HEADS + h)),
                  pl.BlockSpec(blk, lambda b, h: (b, 0, 2 * ATT_HEADS + h))],
        out_specs=pl.BlockSpec(blk, lambda b, h: (b, 0, h)),
        out_shape=jax.ShapeDtypeStruct((B, T, ATT_WIDTH), BF16),
        scratch_shapes=[pltpu.VMEM((T, LANES), F32)] * 3,
        compiler_params=pltpu.CompilerParams(
            dimension_semantics=("arbitrary", "arbitrary"),
            vmem_limit_bytes=_vmem_limit(blocks, 3 * _nbytes((T, LANES), F32),
                                         2 * _nbytes((T, LANES), F32))),
        name="attn_prompt",
    )(z3, z3, z3)


def _mlstm_prompt_kernel(q_ref, k_ref, v_ref, om_ref, lir_ref, fgr_ref, lic_ref, fgc_ref, mhg_ref,
                         hn_ref, c_out, n_out, m_out, c_s, n_s, m_s, *, L):
    c = pl.program_id(2)

    @pl.when(c == 0)
    def _():
        c_s[...] = jnp.zeros_like(c_s)
        n_s[...] = jnp.zeros_like(n_s)
        m_s[...] = jnp.zeros_like(m_s)

    li_row = lir_ref[...]
    li_col = lic_ref[...]
    lf_row = _log_sigmoid(fgr_ref[...])
    lf_col = _log_sigmoid(fgc_ref[...])
    r_idx = lax.broadcasted_iota(jnp.int32, (L, L), 0)
    c_idx = lax.broadcasted_iota(jnp.int32, (L, L), 1)
    causal = c_idx <= r_idx
    b_col = jnp.sum(jnp.where(causal, lf_row, 0.0), axis=1, keepdims=True)
    b_row = jnp.sum(jnp.where(r_idx <= c_idx, lf_col, 0.0), axis=0, keepdims=True)

    m_prev = m_s[...]
    dlog = jnp.where(causal, b_col - b_row + li_row, NEG_INF)
    inter = b_col + m_prev
    m_t = jnp.maximum(inter, jnp.max(dlog, axis=1, keepdims=True))
    dmat = jnp.exp(dlog - m_t)
    a = jnp.exp(inter - m_t)

    qf = q_ref[...] * (MLSTM_DQK ** -0.5)
    qb = qf.astype(BF16)
    kf = k_ref[...]
    kb = kf.astype(BF16)
    vb = v_ref[...].astype(BF16)
    s = lax.dot_general(qb, kb, (((1,), (1,)), ((), ())), preferred_element_type=F32)
    sw = s * dmat
    cb = c_s[...].astype(BF16)
    num = (jnp.dot(sw.astype(BF16), vb, preferred_element_type=F32)
           + a * jnp.dot(qb, cb, preferred_element_type=F32))
    den = (jnp.sum(sw, axis=1, keepdims=True)
           + a * jnp.sum(qf * n_s[...], axis=1, keepdims=True))
    h = num / jnp.maximum(jnp.abs(den), jnp.exp(-m_t))

    m_new = m_t[L - 1:L, :]
    b_last = b_col[L - 1:L, :]
    ws = jnp.exp(b_last - b_col + li_col - m_new)
    decay = jnp.exp(b_last + m_prev - m_new)
    kw = kf * ws
    kwt = kw.T.astype(BF16)
    c_s[...] = decay * c_s[...] + jnp.dot(kwt, vb, preferred_element_type=F32)
    n_s[...] = decay * n_s[...] + jnp.sum(kw, axis=0, keepdims=True)
    m_s[...] = m_new

    hn = h * _rms_scale(h) * mhg_ref[...] * _sigmoid(om_ref[...])
    hn_ref[...] = hn.astype(hn_ref.dtype)

    @pl.when(c == pl.num_programs(2) - 1)
    def _():
        c_out[...] = c_s[...]
        n_out[...] = n_s[...]
        m_out[...] = m_s[...]


def _mlstm_prompt(z3, li_row, fg_row, li_col, fg_col, mhg, *, L):
    B, T, _ = z3.shape
    H = MLSTM_HEADS
    q_off = 3 * ATT_WIDTH // MLSTM_DQK
    k_off = q_off + H
    v_off = (3 * ATT_WIDTH + 2 * H * MLSTM_DQK) // MLSTM_DV
    o_off = v_off + H
    blocks = [2 * _nbytes((L, MLSTM_DQK), F32), 2 * _nbytes((L, MLSTM_DV), F32),
              4 * _nbytes((L, LANES), F32), _nbytes((L, MLSTM_DV), BF16),
              _nbytes((MLSTM_DQK, MLSTM_DV), F32)]
    return pl.pallas_call(
        functools.partial(_mlstm_prompt_kernel, L=L),
        grid=(B, H, T // L),
        in_specs=[
            pl.BlockSpec((None, L, MLSTM_DQK), lambda b, h, c: (b, c, q_off + h)),
            pl.BlockSpec((None, L, MLSTM_DQK), lambda b, h, c: (b, c, k_off + h)),
            pl.BlockSpec((None, L, MLSTM_DV), lambda b, h, c: (b, c, v_off + h)),
            pl.BlockSpec((None, L, MLSTM_DV), lambda b, h, c: (b, c, o_off + h)),
            pl.BlockSpec((None, None, 1, L), lambda b, h, c: (b, h, 0, c)),
            pl.BlockSpec((None, None, 1, L), lambda b, h, c: (b, h, 0, c)),
            pl.BlockSpec((None, None, L, 1), lambda b, h, c: (b, h, c, 0)),
            pl.BlockSpec((None, None, L, 1), lambda b, h, c: (b, h, c, 0)),
            pl.BlockSpec((1, MLSTM_DV), lambda b, h, c: (0, h)),
        ],
        out_specs=[
            pl.BlockSpec((None, L, MLSTM_DV), lambda b, h, c: (b, c, h)),
            pl.BlockSpec((None, None, MLSTM_DQK, MLSTM_DV), lambda b, h, c: (b, h, 0, 0)),
            pl.BlockSpec((None, None, 1, MLSTM_DQK), lambda b, h, c: (b, h, 0, 0)),
            pl.BlockSpec((None, None, 1, 1), lambda b, h, c: (b, h, 0, 0)),
        ],
        out_shape=[
            jax.ShapeDtypeStruct((B, T, MLSTM_WIDTH), BF16),
            jax.ShapeDtypeStruct((B, H, MLSTM_DQK, MLSTM_DV), F32),
            jax.ShapeDtypeStruct((B, H, 1, MLSTM_DQK), F32),
            jax.ShapeDtypeStruct((B, H, 1, 1), F32),
        ],
        scratch_shapes=[pltpu.VMEM((MLSTM_DQK, MLSTM_DV), F32),
                        pltpu.VMEM((1, MLSTM_DQK), F32),
                        pltpu.VMEM((1, 1), F32)],
        compiler_params=pltpu.CompilerParams(
            dimension_semantics=("arbitrary", "arbitrary", "arbitrary"),
            vmem_limit_bytes=_vmem_limit(blocks, _nbytes((MLSTM_DQK, MLSTM_DV), F32),
                                         8 * _nbytes((L, MLSTM_DV), F32) + 8 * _nbytes((L, L), F32))),
        name="mlstm_prompt",
    )(z3, z3, z3, z3, li_row, fg_row, li_col, fg_col, mhg)


def _outproj_kernel(att_ref, hn_ref, wa_ref, wm_ref, x_ref, h_ref):
    h_ref[...] = (x_ref[...]
                  + jnp.dot(att_ref[...], wa_ref[...], preferred_element_type=F32)
                  + jnp.dot(hn_ref[...], wm_ref[...], preferred_element_type=F32))


def _out_proj(att2d, hn2d, wa, wm, x2d, *, tm, tn):
    M, D = x2d.shape
    blocks = [_nbytes((tm, ATT_WIDTH), BF16), _nbytes((tm, MLSTM_WIDTH), BF16),
              _nbytes((ATT_WIDTH, tn), BF16), _nbytes((MLSTM_WIDTH, tn), BF16),
              2 * _nbytes((tm, tn), F32)]
    return pl.pallas_call(
        _outproj_kernel,
        grid=(M // tm, D // tn),
        in_specs=[
            pl.BlockSpec((tm, ATT_WIDTH), lambda i, j: (i, 0)),
            pl.BlockSpec((tm, MLSTM_WIDTH), lambda i, j: (i, 0)),
            pl.BlockSpec((ATT_WIDTH, tn), lambda i, j: (0, j)),
            pl.BlockSpec((MLSTM_WIDTH, tn), lambda i, j: (0, j)),
            pl.BlockSpec((tm, tn), lambda i, j: (i, j)),
        ],
        out_specs=pl.BlockSpec((tm, tn), lambda i, j: (i, j)),
        out_shape=jax.ShapeDtypeStruct((M, D), F32),
        compiler_params=pltpu.CompilerParams(
            dimension_semantics=("arbitrary", "arbitrary"),
            vmem_limit_bytes=_vmem_limit(blocks, 0, 2 * _nbytes((tm, tn), F32))),
        name="out_proj",
    )(att2d, hn2d, wa, wm, x2d)


def _upproj_kernel(x_ref, g_ref, w_ref, u_ref, xn_ref):
    @pl.when(pl.program_id(1) == 0)
    def _():
        x = x_ref[...]
        xn_ref[...] = (x * _rms_scale(x) * g_ref[...]).astype(BF16)

    acc = jnp.dot(xn_ref[...], w_ref[...], preferred_element_type=F32)
    u_ref[...] = jnp.square(jnp.maximum(acc, 0.0)).astype(u_ref.dtype)


def _up_proj(h2d, gain, w_up, *, tm, tn):
    M, D = h2d.shape
    N = w_up.shape[1]
    blocks = [_nbytes((tm, D), F32), _nbytes((D, tn), BF16), _nbytes((tm, tn), BF16)]
    return pl.pallas_call(
        _upproj_kernel,
        grid=(M // tm, N // tn),
        in_specs=[pl.BlockSpec((tm, D), lambda i, j: (i, 0)),
                  pl.BlockSpec((1, D), lambda i, j: (0, 0)),
                  pl.BlockSpec((D, tn), lambda i, j: (0, j))],
        out_specs=pl.BlockSpec((tm, tn), lambda i, j: (i, j)),
        out_shape=jax.ShapeDtypeStruct((M, N), BF16),
        scratch_shapes=[pltpu.VMEM((tm, D), BF16)],
        compiler_params=pltpu.CompilerParams(
            dimension_semantics=("arbitrary", "arbitrary"),
            vmem_limit_bytes=_vmem_limit(blocks, _nbytes((tm, D), BF16),
                                         _nbytes((tm, D), F32) + 2 * _nbytes((tm, tn), F32))),
        name="up_proj",
    )(h2d, gain, w_up)


def _downproj_kernel(u_ref, w_ref, h_ref, g_ref, y_ref):
    k = pl.program_id(1)

    @pl.when(k == 0)
    def _():
        y_ref[...] = h_ref[...]

    u = u_ref[...]
    for n in range(0, y_ref.shape[1], ACC_COLS):
        y_ref[:, n:n + ACC_COLS] += jnp.dot(u, w_ref[:, n:n + ACC_COLS], preferred_element_type=F32)

    @pl.when(k == pl.num_programs(1) - 1)
    def _():
        scale = _rms_scale(y_ref[...])
        y_ref[...] = y_ref[...] * scale * g_ref[...]


def _down_proj(u2d, w_down, h2d, gain, *, tm, tk):
    M, D = h2d.shape
    K = u2d.shape[1]
    blocks = [_nbytes((tm, tk), BF16), _nbytes((tk, D), BF16), 2 * _nbytes((tm, D), F32)]
    return pl.pallas_call(
        _downproj_kernel,
        grid=(M // tm, K // tk),
        in_specs=[pl.BlockSpec((tm, tk), lambda i, k: (i, k)),
                  pl.BlockSpec((tk, D), lambda i, k: (k, 0)),
                  pl.BlockSpec((tm, D), lambda i, k: (i, 0)),
                  pl.BlockSpec((1, D), lambda i, k: (0, 0))],
        out_specs=pl.BlockSpec((tm, D), lambda i, k: (i, 0)),
        out_shape=jax.ShapeDtypeStruct((M, D), F32),
        compiler_params=pltpu.CompilerParams(
            dimension_semantics=("arbitrary", "arbitrary"),
            vmem_limit_bytes=_vmem_limit(blocks, 0, _nbytes((tm, D), F32) + _nbytes((tm, ACC_COLS), F32))),
        name="down_proj",
    )(u2d, w_down, h2d, gain)


def _attn_sample_kernel(q_ref, kn_ref, vn_ref, k1_ref, k4_ref, k16_ref, v1_ref, v4_ref, v16_ref, o_ref):
    q = q_ref[...] * (HEAD_DIM ** -0.5)
    kn = kn_ref[...]
    vn = vn_ref[...]
    s0 = jnp.sum(q * kn, axis=-1, keepdims=True)

    ms, dens, nums = [], [], []
    for kt_ref, vt_ref in ((k1_ref, v1_ref), (k4_ref, v4_ref), (k16_ref, v16_ref)):
        kt = kt_ref[...]
        vt = vt_ref[...]
        s = jnp.sum(q[None] * kt, axis=-1, keepdims=True)
        m = jnp.maximum(jnp.max(s, axis=0), s0)
        p = jnp.exp(s - m[None])
        p0 = jnp.exp(s0 - m)
        dens.append(jnp.sum(p, axis=0) + p0)
        nums.append(jnp.sum(p * vt, axis=0) + p0 * vn)
        ms.append(m)
    m_all = jnp.maximum(jnp.maximum(ms[0], ms[1]), ms[2])
    num = jnp.zeros((ATT_HEADS, HEAD_DIM), F32)
    den = jnp.zeros((ATT_HEADS, 1), F32)
    for m, d, n in zip(ms, dens, nums):
        wgt = jnp.exp(m - m_all)
        num = num + wgt * n
        den = den + wgt * d
    o_ref[...] = (num / den).astype(o_ref.dtype)


def _attn_sample(qs, kns, vns, cache_k, cache_v):
    Bd, W, H, hd = cache_k.shape
    new_spec = pl.BlockSpec((None, H, hd), lambda b: (b, 0, 0))
    in_specs = [new_spec, new_spec, new_spec]
    views_k, views_v = [], []
    for (w, r) in DILATED_CONFIGS:
        assert w // r == Q_TILE and W % w == 0 and PAST_LEN >= W >= w
        views_k.append(cache_k.reshape(Bd, W // r, r, H, hd))
        views_v.append(cache_v.reshape(Bd, W // r, r, H, hd))
    for _ in range(2):
        for (w, r) in DILATED_CONFIGS:
            last = W // w - 1
            in_specs.append(pl.BlockSpec((None, Q_TILE, None, H, hd),
                                         lambda b, last=last: (b, last, 0, 0, 0)))
    blocks = [3 * _nbytes((H, hd), F32), 6 * _nbytes((Q_TILE, H, hd), F32)]
    return pl.pallas_call(
        _attn_sample_kernel,
        grid=(Bd,),
        in_specs=in_specs,
        out_specs=pl.BlockSpec((None, H, hd), lambda b: (b, 0, 0)),
        out_shape=jax.ShapeDtypeStruct((Bd, H, hd), BF16),
        compiler_params=pltpu.CompilerParams(
            dimension_semantics=("arbitrary",),
            vmem_limit_bytes=_vmem_limit(blocks, 0, 8 * _nbytes((Q_TILE, H, hd), F32))),
        name="attn_sample",
    )(qs, kns, vns, *views_k, *views_v)


def _mlstm_sample_kernel(q_ref, k_ref, v_ref, om_ref, g_ref, c0_ref, n0_ref, m0_ref, mhg_ref,
                         hn_ref, c_out, n_out, m_out):
    H = MLSTM_HEADS
    for h in range(H):
        qf = q_ref[h:h + 1, :] * (MLSTM_DQK ** -0.5)
        kf = k_ref[h:h + 1, :]
        vf = v_ref[h:h + 1, :]
        li = g_ref[:, h:h + 1]
        lf = _log_sigmoid(g_ref[:, H + h:H + h + 1])
        m0 = m0_ref[:, h:h + 1]
        c0 = c0_ref[h]
        n0 = n0_ref[h:h + 1, :]

        inter = lf + m0
        m_t = jnp.maximum(inter, li)
        qb = qf.astype(BF16)
        qk = jnp.sum(qb.astype(F32) * kf.astype(BF16).astype(F32), axis=1, keepdims=True)
        sw = qk * jnp.exp(li - m_t)
        a = jnp.exp(inter - m_t)
        qc = jnp.dot(jnp.broadcast_to(qb, (8, MLSTM_DQK)), c0.astype(BF16),
                     preferred_element_type=F32)[0:1, :]
        num = sw * vf + a * qc
        den = sw + a * jnp.sum(qf * n0, axis=1, keepdims=True)
        hh = num / jnp.maximum(jnp.abs(den), jnp.exp(-m_t))

        ws = jnp.exp(li - m_t)
        decay = jnp.exp(inter - m_t)
        kw = kf * ws
        kcol = jnp.broadcast_to(kw, (LANES, MLSTM_DQK)).T[:, 0:1]
        c_out[h] = decay * c0 + kcol * vf
        n_out[h:h + 1, :] = decay * n0 + kw
        m_out[:, h:h + 1] = m_t

        hn = hh * _rms_scale(hh) * mhg_ref[h:h + 1, :] * _sigmoid(om_ref[h:h + 1, :])
        hn_ref[h:h + 1, :] = hn.astype(hn_ref.dtype)


def _mlstm_sample(qs, ks, vs, oms, gs, c0, n0, m0, mhg):
    Bd = qs.shape[0]
    H = MLSTM_HEADS
    blocks = [2 * _nbytes((8, MLSTM_DQK), F32), 2 * _nbytes((8, MLSTM_DV), F32),
              2 * _nbytes((H, MLSTM_DQK, MLSTM_DV), F32), 4 * _nbytes((8, MLSTM_DV), F32)]
    return pl.pallas_call(
        _mlstm_sample_kernel,
        grid=(Bd,),
        in_specs=[
            pl.BlockSpec((None, H, MLSTM_DQK), lambda b: (b, 0, 0)),
            pl.BlockSpec((None, H, MLSTM_DQK), lambda b: (b, 0, 0)),
            pl.BlockSpec((None, H, MLSTM_DV), lambda b: (b, 0, 0)),
            pl.BlockSpec((None, H, MLSTM_DV), lambda b: (b, 0, 0)),
            pl.BlockSpec((None, 1, LANES), lambda b: (b, 0, 0)),
            pl.BlockSpec((None, H, MLSTM_DQK, MLSTM_DV), lambda b: (b, 0, 0, 0)),
            pl.BlockSpec((None, H, MLSTM_DQK), lambda b: (b, 0, 0)),
            pl.BlockSpec((None, 1, H), lambda b: (b, 0, 0)),
            pl.BlockSpec((H, MLSTM_DV), lambda b: (0, 0)),
        ],
        out_specs=[
            pl.BlockSpec((None, H, MLSTM_DV), lambda b: (b, 0, 0)),
            pl.BlockSpec((None, H, MLSTM_DQK, MLSTM_DV), lambda b: (b, 0, 0, 0)),
            pl.BlockSpec((None, H, MLSTM_DQK), lambda b: (b, 0, 0)),
            pl.BlockSpec((None, 1, H), lambda b: (b, 0, 0)),
        ],
        out_shape=[
            jax.ShapeDtypeStruct((Bd, H, MLSTM_DV), BF16),
            jax.ShapeDtypeStruct((Bd, H, MLSTM_DQK, MLSTM_DV), F32),
            jax.ShapeDtypeStruct((Bd, H, MLSTM_DQK), F32),
            jax.ShapeDtypeStruct((Bd, 1, H), F32),
        ],
        compiler_params=pltpu.CompilerParams(
            dimension_semantics=("arbitrary",),
            vmem_limit_bytes=_vmem_limit(blocks, 0, 4 * _nbytes((MLSTM_DQK, MLSTM_DV), F32))),
        name="mlstm_sample",
    )(qs, ks, vs, oms, gs, c0, n0, m0, mhg)


def _rope_tables(pos):
    half = HEAD_DIM // 2
    inv = ROPE_THETA ** (-jnp.arange(half, dtype=F32) / half)
    ang = pos.astype(F32)[:, None] * inv[None, :]
    cos = jnp.cos(ang)
    sin = jnp.sin(ang)
    return jnp.concatenate([cos, cos], axis=-1), jnp.concatenate([-sin, sin], axis=-1)


def _row_tile(M, cap):
    tm = min(M, cap)
    assert M % tm == 0
    return tm


def kernel(x_prompt, x_sample, cache_k, cache_v, state_C, state_n, state_m,
           norm1_g, w_in, b_if, mh_norm_g, w_out, norm2_g, w_up, w_down, final_g):
    B, T, D = x_prompt.shape
    Bd, Td, _ = x_sample.shape
    depth = w_in.shape[0]
    assert depth == 1 and Td == 1 and D == D_MODEL
    keep = min(WIN_MAX, T)
    H = MLSTM_HEADS

    w_main = _cast_columns_bf16(w_in, N_MAIN)
    w_gate = jnp.pad(w_in[0, :, N_MAIN:], ((0, 0), (0, LANES - 2 * H))).astype(BF16)
    bias = jnp.pad(b_if[0], (0, LANES - 2 * H)).reshape(1, LANES)
    wo_a = w_out[0, :ATT_WIDTH].astype(BF16)
    wo_m = w_out[0, ATT_WIDTH:].astype(BF16)
    wu = w_up[0].astype(BF16)
    wd = w_down[0].astype(BF16)
    g1 = norm1_g[0].reshape(1, D)
    g2 = norm2_g[0].reshape(1, D)
    gf = final_g.reshape(1, D)
    mhg = mh_norm_g[0]

    cos_p, sin_p = _rope_tables(jnp.arange(T, dtype=jnp.int32))
    cos_s, sin_s = _rope_tables(jnp.full((Bd,), PAST_LEN, dtype=jnp.int32))

    xp2 = x_prompt.reshape(B * T, D)
    tm = _row_tile(T, 512)
    z, gates = _in_proj(xp2, g1, w_main, w_gate, bias, cos_p, sin_p, tm=tm, tn=1024, pos_tiles=T // tm)
    z3 = z.reshape(B, T, N_MAIN)
    att = _attn_prompt(z3)

    g3 = gates.reshape(B, T, LANES)
    li_row = jnp.swapaxes(g3[:, :, :H], 1, 2).reshape(B, H, 1, T)
    fg_row = jnp.swapaxes(g3[:, :, H:2 * H], 1, 2).reshape(B, H, 1, T)
    li_col = li_row.reshape(B, H, T, 1)
    fg_col = fg_row.reshape(B, H, T, 1)
    hn, c_p, n_p, m_p = _mlstm_prompt(z3, li_row, fg_row, li_col, fg_col, mhg.reshape(1, MLSTM_WIDTH),
                                      L=min(T, 256))

    h_p = _out_proj(att.reshape(B * T, ATT_WIDTH), hn.reshape(B * T, MLSTM_WIDTH), wo_a, wo_m, xp2,
                    tm=tm, tn=1024)
    u_p = _up_proj(h_p, g2, wu, tm=tm, tn=1024)
    y_p = _down_proj(u_p, wd, h_p, gf, tm=tm, tk=1024)

    k_prompt = z3[:, T - keep:, ATT_WIDTH:2 * ATT_WIDTH].reshape(1, B, keep, ATT_HEADS, HEAD_DIM)
    v_prompt = z3[:, T - keep:, 2 * ATT_WIDTH:3 * ATT_WIDTH].reshape(1, B, keep, ATT_HEADS, HEAD_DIM)

    xs2 = x_sample.reshape(Bd, D)
    zs, gates_s = _in_proj(xs2, g1, w_main, w_gate, bias, cos_s, sin_s, tm=Bd, tn=1024, pos_tiles=1)
    qa_s = zs[:, :ATT_WIDTH].reshape(Bd, ATT_HEADS, HEAD_DIM)
    ka_s = zs[:, ATT_WIDTH:2 * ATT_WIDTH].reshape(Bd, ATT_HEADS, HEAD_DIM)
    va_s = zs[:, 2 * ATT_WIDTH:3 * ATT_WIDTH].reshape(Bd, ATT_HEADS, HEAD_DIM)
    att_s = _attn_sample(qa_s, ka_s, va_s, cache_k[0], cache_v[0])

    o = 3 * ATT_WIDTH
    qs = zs[:, o:o + H * MLSTM_DQK].reshape(Bd, H, MLSTM_DQK)
    o += H * MLSTM_DQK
    ks = zs[:, o:o + H * MLSTM_DQK].reshape(Bd, H, MLSTM_DQK)
    o += H * MLSTM_DQK
    vs = zs[:, o:o + MLSTM_WIDTH].reshape(Bd, H, MLSTM_DV)
    o += MLSTM_WIDTH
    oms = zs[:, o:o + MLSTM_WIDTH].reshape(Bd, H, MLSTM_DV)
    hn_s, c_s, n_s, m_s = _mlstm_sample(qs, ks, vs, oms, gates_s.reshape(Bd, 1, LANES),
                                        state_C[0], state_n[0], state_m[0].reshape(Bd, 1, H),
                                        mhg.reshape(H, MLSTM_DV))

    h_s = _out_proj(att_s.reshape(Bd, ATT_WIDTH), hn_s.reshape(Bd, MLSTM_WIDTH), wo_a, wo_m, xs2,
                    tm=Bd, tn=1024)
    u_s = _up_proj(h_s, g2, wu, tm=Bd, tn=1024)
    y_s = _down_proj(u_s, wd, h_s, gf, tm=Bd, tk=1024)

    k_sample = zs[:, ATT_WIDTH:2 * ATT_WIDTH].reshape(1, Bd, 1, ATT_HEADS, HEAD_DIM)
    v_sample = zs[:, 2 * ATT_WIDTH:3 * ATT_WIDTH].reshape(1, Bd, 1, ATT_HEADS, HEAD_DIM)

    return (y_p.reshape(B, T, D), y_s.reshape(Bd, 1, D),
            k_prompt, v_prompt,
            c_p[None], n_p.reshape(1, B, H, MLSTM_DQK), m_p.reshape(1, B, H),
            k_sample, v_sample,
            c_s[None], n_s[None], m_s.reshape(1, Bd, H))
```

```python
import functools

import jax
import jax.numpy as jnp
from jax import lax
from jax.experimental import pallas as pl
from jax.experimental.pallas import tpu as pltpu

F32 = jnp.float32
BF16 = jnp.bfloat16

D_MODEL = 4096
HEAD_DIM = 128
ATT_HEADS = 8
ATT_WIDTH = ATT_HEADS * HEAD_DIM
MLSTM_HEADS = 6
MLSTM_DQK = 256
MLSTM_DV = 512
MLSTM_WIDTH = MLSTM_HEADS * MLSTM_DV
N_MAIN = 3 * ATT_WIDTH + 2 * MLSTM_HEADS * MLSTM_DQK + 2 * MLSTM_WIDTH
GATE_CAP = 15.0
DILATED_CONFIGS = ((128, 1), (512, 4), (2048, 16))
WIN_MAX = 2048
ROPE_THETA = 10000.0
EPS = 1e-6
PAST_LEN = 8192
NEG_INF = float("-inf")

LANES = 128
V7X_VMEM_BYTES = 64 * 1024 * 1024
Q_TILE = 128
ACC_COLS = 512
ATT_GROUP = 4
NT_DIMS = (((1,), (1,)), ((), ()))


def _vmem_limit(block_bytes, scratch_bytes, temp_bytes):
    need = 2 * sum(block_bytes) + scratch_bytes + temp_bytes
    return int(min(max(need, 16 * 1024 * 1024), V7X_VMEM_BYTES - 4 * 1024 * 1024))


def _nbytes(shape, dtype):
    n = 1
    for s in shape:
        n *= s
    return n * jnp.dtype(dtype).itemsize


def _log_sigmoid(x):
    return jnp.minimum(x, 0.0) - jnp.log1p(jnp.exp(-jnp.abs(x)))


def _sigmoid(x):
    return 1.0 / (1.0 + jnp.exp(-x))


def _rms_scale(x):
    return lax.rsqrt(jnp.mean(x * x, axis=-1, keepdims=True) + EPS)


def _cast_kernel(w_ref, o_ref):
    o_ref[...] = w_ref[...].astype(o_ref.dtype)


def _cast_rows_bf16(w, n_rows, *, tr=1024, tc=1024):
    _, C = w.shape
    assert n_rows % tr == 0 and C % tc == 0
    return pl.pallas_call(
        _cast_kernel,
        grid=(n_rows // tr, C // tc),
        in_specs=[pl.BlockSpec((tr, tc), lambda i, j: (i, j))],
        out_specs=pl.BlockSpec((tr, tc), lambda i, j: (i, j)),
        out_shape=jax.ShapeDtypeStruct((n_rows, C), BF16),
        compiler_params=pltpu.CompilerParams(
            dimension_semantics=("arbitrary", "arbitrary"),
            vmem_limit_bytes=_vmem_limit([_nbytes((tr, tc), F32), _nbytes((tr, tc), BF16)], 0, 0)),
        name="cast_bf16",
    )(w)


def _inproj_kernel(x_ref, g_ref, w_ref, wg_ref, bias_ref, cos_ref, sin_ref,
                   z_ref, gate_ref, xn_ref, *, n_rope_tiles, tn):
    j = pl.program_id(1)

    @pl.when(j == 0)
    def _():
        x = x_ref[...]
        xn = (x * _rms_scale(x) * g_ref[...]).astype(BF16)
        xn_ref[...] = xn
        pre = lax.dot_general(xn, wg_ref[...], NT_DIMS, preferred_element_type=F32) + bias_ref[...]
        gate_ref[...] = GATE_CAP * jnp.tanh(pre / GATE_CAP)

    acc = lax.dot_general(xn_ref[...], w_ref[...], NT_DIMS, preferred_element_type=F32)

    @pl.when(j < n_rope_tiles)
    def _():
        c = cos_ref[...]
        s = sin_ref[...]
        for t in range(tn // HEAD_DIM):
            a = acc[:, t * HEAD_DIM:(t + 1) * HEAD_DIM]
            z_ref[:, t * HEAD_DIM:(t + 1) * HEAD_DIM] = a * c + pltpu.roll(a, HEAD_DIM // 2, 1) * s

    @pl.when(j >= n_rope_tiles)
    def _():
        z_ref[...] = acc


def _in_proj(x2d, gain, w_main, w_gate, bias, cosf, sinf, *, tm, tn, pos_tiles):
    M, D = x2d.shape
    N = w_main.shape[0]
    blocks = [_nbytes((tm, D), F32), _nbytes((tn, D), BF16), _nbytes((LANES, D), BF16),
              2 * _nbytes((tm, LANES), F32), _nbytes((tm, tn), F32), _nbytes((tm, LANES), F32)]
    return pl.pallas_call(
        functools.partial(_inproj_kernel, n_rope_tiles=2 * ATT_WIDTH // tn, tn=tn),
        grid=(M // tm, N // tn),
        in_specs=[
            pl.BlockSpec((tm, D), lambda i, j: (i, 0)),
            pl.BlockSpec((1, D), lambda i, j: (0, 0)),
            pl.BlockSpec((tn, D), lambda i, j: (j, 0)),
            pl.BlockSpec((LANES, D), lambda i, j: (0, 0)),
            pl.BlockSpec((1, LANES), lambda i, j: (0, 0)),
            pl.BlockSpec((tm, LANES), lambda i, j: (i % pos_tiles, 0)),
            pl.BlockSpec((tm, LANES), lambda i, j: (i % pos_tiles, 0)),
        ],
        out_specs=[pl.BlockSpec((tm, tn), lambda i, j: (i, j)),
                   pl.BlockSpec((tm, LANES), lambda i, j: (i, 0))],
        out_shape=[jax.ShapeDtypeStruct((M, N), F32), jax.ShapeDtypeStruct((M, LANES), F32)],
        scratch_shapes=[pltpu.VMEM((tm, D), BF16)],
        compiler_params=pltpu.CompilerParams(
            dimension_semantics=("arbitrary", "arbitrary"),
            vmem_limit_bytes=_vmem_limit(blocks, _nbytes((tm, D), BF16),
                                         _nbytes((tm, D), F32) + 2 * _nbytes((tm, tn), F32))),
        name="in_proj",
    )(x2d, gain, w_main, w_gate, bias, cosf, sinf)


def _attn_prompt_kernel(q_ref, k_ref, v_ref, o_ref, m_s, l_s, acc_s, *, T):
    scale = HEAD_DIM ** -0.5
    G = ATT_GROUP
    row = lax.broadcasted_iota(jnp.int32, (G * Q_TILE, 2 * Q_TILE), 0) & (Q_TILE - 1)
    col = lax.broadcasted_iota(jnp.int32, (G * Q_TILE, 2 * Q_TILE), 1)
    band = jnp.logical_and(col >= row, col - Q_TILE <= row)
    prev_cols = lax.broadcasted_iota(jnp.int32, (Q_TILE, 2 * Q_TILE), 1) < Q_TILE
    nt = (((1,), (1,)), ((), ()))

    for ci, (_, r) in enumerate(DILATED_CONFIGS):
        tiles_per_class = T // (r * Q_TILE)
        run = min(G, tiles_per_class)
        runs_per_group = G // run
        runs_per_class = tiles_per_class // run
        shift = runs_per_class.bit_length() - 1
        n_groups = T // (Q_TILE * G)

        def rows(ref, start, n, r=r):
            if r == 1:
                return ref[pl.ds(start, n), :]
            return ref[pl.ds(start, n, stride=r), :]

        def put(ref, start, n, val, r=r):
            if r == 1:
                ref[pl.ds(start, n), :] = val
            else:
                ref[pl.ds(start, n, stride=r), :] = val

        def body(gi, carry, r=r, ci=ci, run=run, rpg=runs_per_group, rpc=runs_per_class,
                 shift=shift, rows=rows, put=put):
            n = run * Q_TILE
            starts, vxs, scores = [], [], []
            for j in range(rpg):
                ridx = gi * rpg + j
                cls = lax.shift_right_logical(ridx, shift)
                u0 = (ridx & (rpc - 1)) * n
                start = cls + r * u0
                pstart = cls + r * jnp.maximum(u0 - Q_TILE, 0)
                no_prev = u0 == 0
                q = (rows(q_ref, start, n) * scale).astype(BF16)
                kx = jnp.concatenate([rows(k_ref, pstart, Q_TILE), rows(k_ref, start, n)], axis=0).astype(BF16)
                vx = jnp.concatenate([rows(v_ref, pstart, Q_TILE), rows(v_ref, start, n)], axis=0).astype(BF16)
                for g in range(run):
                    s = lax.dot_general(q[g * Q_TILE:(g + 1) * Q_TILE], kx[g * Q_TILE:(g + 2) * Q_TILE], nt,
                                        preferred_element_type=F32)
                    if g == 0:
                        s = jnp.where(jnp.logical_and(prev_cols, no_prev), NEG_INF, s)
                    scores.append(s)
                starts.append(start)
                vxs.append(vx)
            s_all = jnp.where(band, jnp.concatenate(scores, axis=0), NEG_INF)
            mt = jnp.max(s_all, axis=1, keepdims=True)
            if ci == 0:
                m_new = jnp.broadcast_to(mt, (G * Q_TILE, LANES))
            else:
                m_old = jnp.concatenate([rows(m_s, st, n) for st in starts], axis=0)
                m_new = jnp.maximum(m_old, mt)
                alpha = jnp.exp(m_old - m_new)
            p = jnp.exp(s_all - jnp.concatenate([m_new, m_new], axis=1))
            lt = jnp.sum(p, axis=1, keepdims=True)
            pb = p.astype(BF16)
            outs = []
            for j in range(rpg):
                for g in range(run):
                    t = j * run + g
                    outs.append(jnp.dot(pb[t * Q_TILE:(t + 1) * Q_TILE], vxs[j][g * Q_TILE:(g + 2) * Q_TILE],
                                        preferred_element_type=F32))
            ot = jnp.concatenate(outs, axis=0)
            if ci == 0:
                l_new = jnp.broadcast_to(lt, (G * Q_TILE, LANES))
            else:
                l_new = alpha * jnp.concatenate([rows(l_s, st, n) for st in starts], axis=0) + lt
                ot = alpha * jnp.concatenate([rows(acc_s, st, n) for st in starts], axis=0) + ot
            for j, st in enumerate(starts):
                put(m_s, st, n, m_new[j * n:(j + 1) * n])
                put(l_s, st, n, l_new[j * n:(j + 1) * n])
                put(acc_s, st, n, ot[j * n:(j + 1) * n])
            return carry

        lax.fori_loop(0, n_groups, body, 0)

    o_ref[...] = (acc_s[...] / l_s[...]).astype(o_ref.dtype)


def _attn_prompt(z3):
    B, T, _ = z3.shape
    assert T % (DILATED_CONFIGS[-1][1] * Q_TILE) == 0
    blk = (None, T, HEAD_DIM)
    blocks = [3 * _nbytes((T, HEAD_DIM), F32), _nbytes((T, HEAD_DIM), BF16)]
    return pl.pallas_call(
        functools.partial(_attn_prompt_kernel, T=T),
        grid=(B, ATT_HEADS),
        in_specs=[pl.BlockSpec(blk, lambda b, h: (b, 0, h)),
                  pl.BlockSpec(blk, lambda b, h: (b, 0, ATT_HEADS + h)),
                  pl.BlockSpec(blk, lambda b, h: (b, 0, 2 * ATT_HEADS + h))],
        out_specs=pl.BlockSpec(blk, lambda b, h: (b, 0, h)),
        out_shape=jax.ShapeDtypeStruct((B, T, ATT_WIDTH), BF16),
        scratch_shapes=[pltpu.VMEM((T, LANES), F32)] * 3,
        compiler_params=pltpu.CompilerParams(
            dimension_semantics=("arbitrary", "arbitrary"),
            vmem_limit_bytes=_vmem_limit(blocks, 3 * _nbytes((T, LANES), F32),
                                         2 * _nbytes((T, LANES), F32))),
        name="attn_prompt",
    )(z3, z3, z3)


def _mlstm_prompt_kernel(q_ref, k_ref, v_ref, om_ref, lir_ref, fgr_ref, lic_ref, fgc_ref, mhg_ref,
                         hn_ref, c_out, n_out, m_out, c_s, n_s, m_s, *, L):
    c = pl.program_id(2)

    @pl.when(c == 0)
    def _():
        c_s[...] = jnp.zeros_like(c_s)
        n_s[...] = jnp.zeros_like(n_s)
        m_s[...] = jnp.zeros_like(m_s)

    li_row = lir_ref[...]
    li_col = lic_ref[...]
    lf_row = _log_sigmoid(fgr_ref[...])
    lf_col = _log_sigmoid(fgc_ref[...])
    r_idx = lax.broadcasted_iota(jnp.int32, (L, L), 0)
    c_idx = lax.broadcasted_iota(jnp.int32, (L, L), 1)
    causal = c_idx <= r_idx
    b_col = jnp.sum(jnp.where(causal, lf_row, 0.0), axis=1, keepdims=True)
    b_row = jnp.sum(jnp.where(r_idx <= c_idx, lf_col, 0.0), axis=0, keepdims=True)

    m_prev = m_s[...]
    dlog = jnp.where(causal, b_col - b_row + li_row, NEG_INF)
    inter = b_col + m_prev
    m_t = jnp.maximum(inter, jnp.max(dlog, axis=1, keepdims=True))
    dmat = jnp.exp(dlog - m_t)
    a = jnp.exp(inter - m_t)

    qf = q_ref[...] * (MLSTM_DQK ** -0.5)
    qb = qf.astype(BF16)
    kf = k_ref[...]
    kb = kf.astype(BF16)
    vb = v_ref[...].astype(BF16)
    s = lax.dot_general(qb, kb, (((1,), (1,)), ((), ())), preferred_element_type=F32)
    sw = s * dmat
    cb = c_s[...].astype(BF16)
    num = (jnp.dot(sw.astype(BF16), vb, preferred_element_type=F32)
           + a * jnp.dot(qb, cb, preferred_element_type=F32))
    den = (jnp.sum(sw, axis=1, keepdims=True)
           + a * jnp.sum(qf * n_s[...], axis=1, keepdims=True))
    h = num / jnp.maximum(jnp.abs(den), jnp.exp(-m_t))

    m_new = m_t[L - 1:L, :]
    b_last = b_col[L - 1:L, :]
    ws = jnp.exp(b_last - b_col + li_col - m_new)
    decay = jnp.exp(b_last + m_prev - m_new)
    kw = kf * ws
    kwt = kw.T.astype(BF16)
    c_s[...] = decay * c_s[...] + jnp.dot(kwt, vb, preferred_element_type=F32)
    n_s[...] = decay * n_s[...] + jnp.sum(kw, axis=0, keepdims=True)
    m_s[...] = m_new

    hn = h * _rms_scale(h) * mhg_ref[...] * _sigmoid(om_ref[...])
    hn_ref[...] = hn.astype(hn_ref.dtype)

    @pl.when(c == pl.num_programs(2) - 1)
    def _():
        c_out[...] = c_s[...]
        n_out[...] = n_s[...]
        m_out[...] = m_s[...]


def _mlstm_prompt(z3, li_row, fg_row, li_col, fg_col, mhg, *, L):
    B, T, _ = z3.shape
    H = MLSTM_HEADS
    q_off = 3 * ATT_WIDTH // MLSTM_DQK
    k_off = q_off + H
    v_off = (3 * ATT_WIDTH + 2 * H * MLSTM_DQK) // MLSTM_DV
    o_off = v_off + H
    blocks = [2 * _nbytes((L, MLSTM_DQK), F32), 2 * _nbytes((L, MLSTM_DV), F32),
              4 * _nbytes((L, LANES), F32), _nbytes((L, MLSTM_DV), BF16),
              _nbytes((MLSTM_DQK, MLSTM_DV), F32)]
    return pl.pallas_call(
        functools.partial(_mlstm_prompt_kernel, L=L),
        grid=(B, H, T // L),
        in_specs=[
            pl.BlockSpec((None, L, MLSTM_DQK), lambda b, h, c: (b, c, q_off + h)),
            pl.BlockSpec((None, L, MLSTM_DQK), lambda b, h, c: (b, c, k_off + h)),
            pl.BlockSpec((None, L, MLSTM_DV), lambda b, h, c: (b, c, v_off + h)),
            pl.BlockSpec((None, L, MLSTM_DV), lambda b, h, c: (b, c, o_off + h)),
            pl.BlockSpec((None, None, 1, L), lambda b, h, c: (b, h, 0, c)),
            pl.BlockSpec((None, None, 1, L), lambda b, h, c: (b, h, 0, c)),
            pl.BlockSpec((None, None, L, 1), lambda b, h, c: (b, h, c, 0)),
            pl.BlockSpec((None, None, L, 1), lambda b, h, c: (b, h, c, 0)),
            pl.BlockSpec((1, MLSTM_DV), lambda b, h, c: (0, h)),
        ],
        out_specs=[
            pl.BlockSpec((None, L, MLSTM_DV), lambda b, h, c: (b, c, h)),
            pl.BlockSpec((None, None, MLSTM_DQK, MLSTM_DV), lambda b, h, c: (b, h, 0, 0)),
            pl.BlockSpec((None, None, 1, MLSTM_DQK), lambda b, h, c: (b, h, 0, 0)),
            pl.BlockSpec((None, None, 1, 1), lambda b, h, c: (b, h, 0, 0)),
        ],
        out_shape=[
            jax.ShapeDtypeStruct((B, T, MLSTM_WIDTH), BF16),
            jax.ShapeDtypeStruct((B, H, MLSTM_DQK, MLSTM_DV), F32),
            jax.ShapeDtypeStruct((B, H, 1, MLSTM_DQK), F32),
            jax.ShapeDtypeStruct((B, H, 1, 1), F32),
        ],
        scratch_shapes=[pltpu.VMEM((MLSTM_DQK, MLSTM_DV), F32),
                        pltpu.VMEM((1, MLSTM_DQK), F32),
                        pltpu.VMEM((1, 1), F32)],
        compiler_params=pltpu.CompilerParams(
            dimension_semantics=("arbitrary", "arbitrary", "arbitrary"),
            vmem_limit_bytes=_vmem_limit(blocks, _nbytes((MLSTM_DQK, MLSTM_DV), F32),
                                         8 * _nbytes((L, MLSTM_DV), F32) + 8 * _nbytes((L, L), F32))),
        name="mlstm_prompt",
    )(z3, z3, z3, z3, li_row, fg_row, li_col, fg_col, mhg)


def _outproj_kernel(att_ref, hn_ref, wa_ref, wm_ref, x_ref, h_ref):
    h_ref[...] = (x_ref[...]
                  + jnp.dot(att_ref[...], wa_ref[...], preferred_element_type=F32)
                  + jnp.dot(hn_ref[...], wm_ref[...], preferred_element_type=F32))


def _out_proj(att2d, hn2d, wa, wm, x2d, *, tm, tn):
    M, D = x2d.shape
    blocks = [_nbytes((tm, ATT_WIDTH), BF16), _nbytes((tm, MLSTM_WIDTH), BF16),
              _nbytes((ATT_WIDTH, tn), BF16), _nbytes((MLSTM_WIDTH, tn), BF16),
              2 * _nbytes((tm, tn), F32)]
    return pl.pallas_call(
        _outproj_kernel,
        grid=(M // tm, D // tn),
        in_specs=[
            pl.BlockSpec((tm, ATT_WIDTH), lambda i, j: (i, 0)),
            pl.BlockSpec((tm, MLSTM_WIDTH), lambda i, j: (i, 0)),
            pl.BlockSpec((ATT_WIDTH, tn), lambda i, j: (0, j)),
            pl.BlockSpec((MLSTM_WIDTH, tn), lambda i, j: (0, j)),
            pl.BlockSpec((tm, tn), lambda i, j: (i, j)),
        ],
        out_specs=pl.BlockSpec((tm, tn), lambda i, j: (i, j)),
        out_shape=jax.ShapeDtypeStruct((M, D), F32),
        compiler_params=pltpu.CompilerParams(
            dimension_semantics=("arbitrary", "arbitrary"),
            vmem_limit_bytes=_vmem_limit(blocks, 0, 2 * _nbytes((tm, tn), F32))),
        name="out_proj",
    )(att2d, hn2d, wa, wm, x2d)


def _upproj_kernel(x_ref, g_ref, w_ref, u_ref, xn_ref):
    @pl.when(pl.program_id(1) == 0)
    def _():
        x = x_ref[...]
        xn_ref[...] = (x * _rms_scale(x) * g_ref[...]).astype(BF16)

    acc = jnp.dot(xn_ref[...], w_ref[...], preferred_element_type=F32)
    u_ref[...] = jnp.square(jnp.maximum(acc, 0.0)).astype(u_ref.dtype)


def _up_proj(h2d, gain, w_up, *, tm, tn):
    M, D = h2d.shape
    N = w_up.shape[1]
    blocks = [_nbytes((tm, D), F32), _nbytes((D, tn), BF16), _nbytes((tm, tn), BF16)]
    return pl.pallas_call(
        _upproj_kernel,
        grid=(M // tm, N // tn),
        in_specs=[pl.BlockSpec((tm, D), lambda i, j: (i, 0)),
                  pl.BlockSpec((1, D), lambda i, j: (0, 0)),
                  pl.BlockSpec((D, tn), lambda i, j: (0, j))],
        out_specs=pl.BlockSpec((tm, tn), lambda i, j: (i, j)),
        out_shape=jax.ShapeDtypeStruct((M, N), BF16),
        scratch_shapes=[pltpu.VMEM((tm, D), BF16)],
        compiler_params=pltpu.CompilerParams(
            dimension_semantics=("arbitrary", "arbitrary"),
            vmem_limit_bytes=_vmem_limit(blocks, _nbytes((tm, D), BF16),
                                         _nbytes((tm, D), F32) + 2 * _nbytes((tm, tn), F32))),
        name="up_proj",
    )(h2d, gain, w_up)


def _downproj_kernel(u_ref, w_ref, h_ref, g_ref, y_ref):
    k = pl.program_id(1)

    @pl.when(k == 0)
    def _():
        y_ref[...] = h_ref[...]

    u = u_ref[...]
    for n in range(0, y_ref.shape[1], ACC_COLS):
        y_ref[:, n:n + ACC_COLS] += jnp.dot(u, w_ref[:, n:n + ACC_COLS], preferred_element_type=F32)

    @pl.when(k == pl.num_programs(1) - 1)
    def _():
        scale = _rms_scale(y_ref[...])
        y_ref[...] = y_ref[...] * scale * g_ref[...]


def _down_proj(u2d, w_down, h2d, gain, *, tm, tk):
    M, D = h2d.shape
    K = u2d.shape[1]
    blocks = [_nbytes((tm, tk), BF16), _nbytes((tk, D), BF16), 2 * _nbytes((tm, D), F32)]
    return pl.pallas_call(
        _downproj_kernel,
        grid=(M // tm, K // tk),
        in_specs=[pl.BlockSpec((tm, tk), lambda i, k: (i, k)),
                  pl.BlockSpec((tk, D), lambda i, k: (k, 0)),
                  pl.BlockSpec((tm, D), lambda i, k: (i, 0)),
                  pl.BlockSpec((1, D), lambda i, k: (0, 0))],
        out_specs=pl.BlockSpec((tm, D), lambda i, k: (i, 0)),
        out_shape=jax.ShapeDtypeStruct((M, D), F32),
        compiler_params=pltpu.CompilerParams(
            dimension_semantics=("arbitrary", "arbitrary"),
            vmem_limit_bytes=_vmem_limit(blocks, 0, _nbytes((tm, D), F32) + _nbytes((tm, ACC_COLS), F32))),
        name="down_proj",
    )(u2d, w_down, h2d, gain)


def _attn_sample_kernel(q_ref, kn_ref, vn_ref, k1_ref, k4_ref, k16_ref, v1_ref, v4_ref, v16_ref, o_ref):
    q = q_ref[...] * (HEAD_DIM ** -0.5)
    kn = kn_ref[...]
    vn = vn_ref[...]
    s0 = jnp.sum(q * kn, axis=-1, keepdims=True)

    ms, dens, nums = [], [], []
    for kt_ref, vt_ref in ((k1_ref, v1_ref), (k4_ref, v4_ref), (k16_ref, v16_ref)):
        kt = kt_ref[...]
        vt = vt_ref[...]
        s = jnp.sum(q[None] * kt, axis=-1, keepdims=True)
        m = jnp.maximum(jnp.max(s, axis=0), s0)
        p = jnp.exp(s - m[None])
        p0 = jnp.exp(s0 - m)
        dens.append(jnp.sum(p, axis=0) + p0)
        nums.append(jnp.sum(p * vt, axis=0) + p0 * vn)
        ms.append(m)
    m_all = jnp.maximum(jnp.maximum(ms[0], ms[1]), ms[2])
    num = jnp.zeros((ATT_HEADS, HEAD_DIM), F32)
    den = jnp.zeros((ATT_HEADS, 1), F32)
    for m, d, n in zip(ms, dens, nums):
        wgt = jnp.exp(m - m_all)
        num = num + wgt * n
        den = den + wgt * d
    o_ref[...] = (num / den).astype(o_ref.dtype)


def _attn_sample(qs, kns, vns, cache_k, cache_v):
    Bd, W, H, hd = cache_k.shape
    new_spec = pl.BlockSpec((None, H, hd), lambda b: (b, 0, 0))
    in_specs = [new_spec, new_spec, new_spec]
    views_k, views_v = [], []
    for (w, r) in DILATED_CONFIGS:
        assert w // r == Q_TILE and W % w == 0 and PAST_LEN >= W >= w
        views_k.append(cache_k.reshape(Bd, W // r, r, H, hd))
        views_v.append(cache_v.reshape(Bd, W // r, r, H, hd))
    for _ in range(2):
        for (w, r) in DILATED_CONFIGS:
            last = W // w - 1
            in_specs.append(pl.BlockSpec((None, Q_TILE, None, H, hd),
                                         lambda b, last=last: (b, last, 0, 0, 0)))
    blocks = [3 * _nbytes((H, hd), F32), 6 * _nbytes((Q_TILE, H, hd), F32)]
    return pl.pallas_call(
        _attn_sample_kernel,
        grid=(Bd,),
        in_specs=in_specs,
        out_specs=pl.BlockSpec((None, H, hd), lambda b: (b, 0, 0)),
        out_shape=jax.ShapeDtypeStruct((Bd, H, hd), BF16),
        compiler_params=pltpu.CompilerParams(
            dimension_semantics=("arbitrary",),
            vmem_limit_bytes=_vmem_limit(blocks, 0, 8 * _nbytes((Q_TILE, H, hd), F32))),
        name="attn_sample",
    )(qs, kns, vns, *views_k, *views_v)


def _qk_sample_kernel(x_ref, g_ref, w_ref, o_ref, xn_ref):
    @pl.when(pl.program_id(0) == 0)
    def _():
        x = x_ref[...]
        xn_ref[...] = x * _rms_scale(x) * g_ref[...]

    o_ref[...] = lax.dot_general(xn_ref[...], w_ref[...], NT_DIMS, precision=lax.Precision.HIGHEST,
                                 preferred_element_type=F32)


def _qk_sample(x2d, gain, w_t, *, tn=512):
    Bd, D = x2d.shape
    n_out = 2 * MLSTM_HEADS * MLSTM_DQK
    first = 3 * ATT_WIDTH // tn
    blocks = [_nbytes((Bd, D), F32), _nbytes((tn, D), F32), _nbytes((Bd, tn), F32)]
    return pl.pallas_call(
        _qk_sample_kernel,
        grid=(n_out // tn,),
        in_specs=[pl.BlockSpec((Bd, D), lambda j: (0, 0)),
                  pl.BlockSpec((1, D), lambda j: (0, 0)),
                  pl.BlockSpec((tn, D), lambda j: (first + j, 0))],
        out_specs=pl.BlockSpec((Bd, tn), lambda j: (0, j)),
        out_shape=jax.ShapeDtypeStruct((Bd, n_out), F32),
        scratch_shapes=[pltpu.VMEM((Bd, D), F32)],
        compiler_params=pltpu.CompilerParams(
            dimension_semantics=("arbitrary",),
            vmem_limit_bytes=_vmem_limit(blocks, _nbytes((Bd, D), F32), 4 * _nbytes((tn, D), BF16))),
        name="qk_sample",
    )(x2d, gain, w_t)


def _mlstm_sample_kernel(q_ref, k_ref, v_ref, om_ref, g_ref, c0_ref, n0_ref, m0_ref, mhg_ref,
                         hn_ref, c_out, n_out, m_out):
    H = MLSTM_HEADS
    for h in range(H):
        qf = q_ref[h:h + 1, :] * (MLSTM_DQK ** -0.5)
        kf = k_ref[h:h + 1, :]
        vf = v_ref[h:h + 1, :]
        li = g_ref[:, h:h + 1]
        lf = _log_sigmoid(g_ref[:, H + h:H + h + 1])
        m0 = m0_ref[:, h:h + 1]
        c0 = c0_ref[h]
        n0 = n0_ref[h:h + 1, :]

        inter = lf + m0
        m_t = jnp.maximum(inter, li)
        qk = jnp.sum(qf * kf, axis=1, keepdims=True)
        sw = qk * jnp.exp(li - m_t)
        a = jnp.exp(inter - m_t)
        qc = jnp.dot(jnp.broadcast_to(qf, (8, MLSTM_DQK)), c0, precision=lax.Precision.HIGHEST,
                     preferred_element_type=F32)[0:1, :]
        num = sw * vf + a * qc
        den = sw + a * jnp.sum(qf * n0, axis=1, keepdims=True)
        hh = num / jnp.maximum(jnp.abs(den), jnp.exp(-m_t))

        ws = jnp.exp(li - m_t)
        decay = jnp.exp(inter - m_t)
        kw = kf * ws
        kcol = jnp.broadcast_to(kw, (LANES, MLSTM_DQK)).T[:, 0:1]
        c_out[h] = decay * c0 + kcol * vf
        n_out[h:h + 1, :] = decay * n0 + kw
        m_out[:, h:h + 1] = m_t

        hn = hh * _rms_scale(hh) * mhg_ref[h:h + 1, :] * _sigmoid(om_ref[h:h + 1, :])
        hn_ref[h:h + 1, :] = hn.astype(hn_ref.dtype)


def _mlstm_sample(qs, ks, vs, oms, gs, c0, n0, m0, mhg):
    Bd = qs.shape[0]
    H = MLSTM_HEADS
    blocks = [2 * _nbytes((8, MLSTM_DQK), F32), 2 * _nbytes((8, MLSTM_DV), F32),
              2 * _nbytes((H, MLSTM_DQK, MLSTM_DV), F32), 4 * _nbytes((8, MLSTM_DV), F32)]
    return pl.pallas_call(
        _mlstm_sample_kernel,
        grid=(Bd,),
        in_specs=[
            pl.BlockSpec((None, H, MLSTM_DQK), lambda b: (b, 0, 0)),
            pl.BlockSpec((None, H, MLSTM_DQK), lambda b: (b, 0, 0)),
            pl.BlockSpec((None, H, MLSTM_DV), lambda b: (b, 0, 0)),
            pl.BlockSpec((None, H, MLSTM_DV), lambda b: (b, 0, 0)),
            pl.BlockSpec((None, 1, LANES), lambda b: (b, 0, 0)),
            pl.BlockSpec((None, H, MLSTM_DQK, MLSTM_DV), lambda b: (b, 0, 0, 0)),
            pl.BlockSpec((None, H, MLSTM_DQK), lambda b: (b, 0, 0)),
            pl.BlockSpec((None, 1, H), lambda b: (b, 0, 0)),
            pl.BlockSpec((H, MLSTM_DV), lambda b: (0, 0)),
        ],
        out_specs=[
            pl.BlockSpec((None, H, MLSTM_DV), lambda b: (b, 0, 0)),
            pl.BlockSpec((None, H, MLSTM_DQK, MLSTM_DV), lambda b: (b, 0, 0, 0)),
            pl.BlockSpec((None, H, MLSTM_DQK), lambda b: (b, 0, 0)),
            pl.BlockSpec((None, 1, H), lambda b: (b, 0, 0)),
        ],
        out_shape=[
            jax.ShapeDtypeStruct((Bd, H, MLSTM_DV), BF16),
            jax.ShapeDtypeStruct((Bd, H, MLSTM_DQK, MLSTM_DV), F32),
            jax.ShapeDtypeStruct((Bd, H, MLSTM_DQK), F32),
            jax.ShapeDtypeStruct((Bd, 1, H), F32),
        ],
        compiler_params=pltpu.CompilerParams(
            dimension_semantics=("arbitrary",),
            vmem_limit_bytes=_vmem_limit(blocks, 0, 4 * _nbytes((MLSTM_DQK, MLSTM_DV), F32))),
        name="mlstm_sample",
    )(qs, ks, vs, oms, gs, c0, n0, m0, mhg)


def _rope_tables(pos):
    half = HEAD_DIM // 2
    inv = ROPE_THETA ** (-jnp.arange(half, dtype=F32) / half)
    ang = pos.astype(F32)[:, None] * inv[None, :]
    cos = jnp.cos(ang)
    sin = jnp.sin(ang)
    return jnp.concatenate([cos, cos], axis=-1), jnp.concatenate([-sin, sin], axis=-1)


def _row_tile(M, cap):
    tm = min(M, cap)
    assert M % tm == 0
    return tm


def kernel(x_prompt, x_sample, cache_k, cache_v, state_C, state_n, state_m,
           norm1_g, w_in, b_if, mh_norm_g, w_out, norm2_g, w_up, w_down, final_g):
    B, T, D = x_prompt.shape
    Bd, Td, _ = x_sample.shape
    depth = w_in.shape[0]
    assert depth == 1 and Td == 1 and D == D_MODEL
    keep = min(WIN_MAX, T)
    H = MLSTM_HEADS

    w_in_t = jnp.transpose(w_in[0])
    w_main = _cast_rows_bf16(w_in_t, N_MAIN)
    w_gate = jnp.pad(w_in_t[N_MAIN:], ((0, LANES - 2 * H), (0, 0))).astype(BF16)
    bias = jnp.pad(b_if[0], (0, LANES - 2 * H)).reshape(1, LANES)
    wo_a = w_out[0, :ATT_WIDTH].astype(BF16)
    wo_m = w_out[0, ATT_WIDTH:].astype(BF16)
    wu = w_up[0].astype(BF16)
    wd = w_down[0].astype(BF16)
    g1 = norm1_g[0].reshape(1, D)
    g2 = norm2_g[0].reshape(1, D)
    gf = final_g.reshape(1, D)
    mhg = mh_norm_g[0]

    cos_p, sin_p = _rope_tables(jnp.arange(T, dtype=jnp.int32))
    cos_s, sin_s = _rope_tables(jnp.full((Bd,), PAST_LEN, dtype=jnp.int32))

    xp2 = x_prompt.reshape(B * T, D)
    tm = _row_tile(T, 512)
    z, gates = _in_proj(xp2, g1, w_main, w_gate, bias, cos_p, sin_p, tm=tm, tn=1024, pos_tiles=T // tm)
    z3 = z.reshape(B, T, N_MAIN)
    att = _attn_prompt(z3)

    g3 = gates.reshape(B, T, LANES)
    li_row = jnp.swapaxes(g3[:, :, :H], 1, 2).reshape(B, H, 1, T)
    fg_row = jnp.swapaxes(g3[:, :, H:2 * H], 1, 2).reshape(B, H, 1, T)
    li_col = li_row.reshape(B, H, T, 1)
    fg_col = fg_row.reshape(B, H, T, 1)
    hn, c_p, n_p, m_p = _mlstm_prompt(z3, li_row, fg_row, li_col, fg_col, mhg.reshape(1, MLSTM_WIDTH),
                                      L=min(T, 256))

    h_p = _out_proj(att.reshape(B * T, ATT_WIDTH), hn.reshape(B * T, MLSTM_WIDTH), wo_a, wo_m, xp2,
                    tm=tm, tn=1024)
    u_p = _up_proj(h_p, g2, wu, tm=tm, tn=1024)
    y_p = _down_proj(u_p, wd, h_p, gf, tm=tm, tk=1024)

    k_prompt = z3[:, T - keep:, ATT_WIDTH:2 * ATT_WIDTH].reshape(1, B, keep, ATT_HEADS, HEAD_DIM)
    v_prompt = z3[:, T - keep:, 2 * ATT_WIDTH:3 * ATT_WIDTH].reshape(1, B, keep, ATT_HEADS, HEAD_DIM)

    xs2 = x_sample.reshape(Bd, D)
    zs, gates_s = _in_proj(xs2, g1, w_main, w_gate, bias, cos_s, sin_s, tm=Bd, tn=1024, pos_tiles=1)
    qa_s = zs[:, :ATT_WIDTH].reshape(Bd, ATT_HEADS, HEAD_DIM)
    ka_s = zs[:, ATT_WIDTH:2 * ATT_WIDTH].reshape(Bd, ATT_HEADS, HEAD_DIM)
    va_s = zs[:, 2 * ATT_WIDTH:3 * ATT_WIDTH].reshape(Bd, ATT_HEADS, HEAD_DIM)
    att_s = _attn_sample(qa_s, ka_s, va_s, cache_k[0], cache_v[0])

    qk_s = _qk_sample(xs2, g1, w_in_t)
    qs = qk_s[:, :H * MLSTM_DQK].reshape(Bd, H, MLSTM_DQK)
    ks = qk_s[:, H * MLSTM_DQK:].reshape(Bd, H, MLSTM_DQK)
    o = 3 * ATT_WIDTH + 2 * H * MLSTM_DQK
    vs = zs[:, o:o + MLSTM_WIDTH].reshape(Bd, H, MLSTM_DV)
    o += MLSTM_WIDTH
    oms = zs[:, o:o + MLSTM_WIDTH].reshape(Bd, H, MLSTM_DV)
    hn_s, c_s, n_s, m_s = _mlstm_sample(qs, ks, vs, oms, gates_s.reshape(Bd, 1, LANES),
                                        state_C[0], state_n[0], state_m[0].reshape(Bd, 1, H),
                                        mhg.reshape(H, MLSTM_DV))

    h_s = _out_proj(att_s.reshape(Bd, ATT_WIDTH), hn_s.reshape(Bd, MLSTM_WIDTH), wo_a, wo_m, xs2,
                    tm=Bd, tn=1024)
    u_s = _up_proj(h_s, g2, wu, tm=Bd, tn=1024)
    y_s = _down_proj(u_s, wd, h_s, gf, tm=Bd, tk=1024)

    k_sample = zs[:, ATT_WIDTH:2 * ATT_WIDTH].reshape(1, Bd, 1, ATT_HEADS, HEAD_DIM)
    v_sample = zs[:, 2 * ATT_WIDTH:3 * ATT_WIDTH].reshape(1, Bd, 1, ATT_HEADS, HEAD_DIM)

    return (y_p.reshape(B, T, D), y_s.reshape(Bd, 1, D),
            k_prompt, v_prompt,
            c_p[None], n_p.reshape(1, B, H, MLSTM_DQK), m_p.reshape(1, B, H),
            k_sample, v_sample,
            c_s[None], n_s[None], m_s.reshape(1, Bd, H))
```

```python
import functools

import jax
import jax.numpy as jnp
from jax import lax
from jax.experimental import pallas as pl
from jax.experimental.pallas import tpu as pltpu

F32 = jnp.float32
BF16 = jnp.bfloat16

D_MODEL = 4096
HEAD_DIM = 128
ATT_HEADS = 8
ATT_WIDTH = ATT_HEADS * HEAD_DIM
MLSTM_HEADS = 6
MLSTM_DQK = 256
MLSTM_DV = 512
MLSTM_WIDTH = MLSTM_HEADS * MLSTM_DV
N_MAIN = 3 * ATT_WIDTH + 2 * MLSTM_HEADS * MLSTM_DQK + 2 * MLSTM_WIDTH
GATE_CAP = 15.0
DILATED_CONFIGS = ((128, 1), (512, 4), (2048, 16))
WIN_MAX = 2048
ROPE_THETA = 10000.0
EPS = 1e-6
PAST_LEN = 8192
NEG_INF = float("-inf")

LANES = 128
V7X_VMEM_BYTES = 64 * 1024 * 1024
Q_TILE = 128
ACC_COLS = 512
ATT_GROUP = 4
NT_DIMS = (((1,), (1,)), ((), ()))


def _vmem_limit(block_bytes, scratch_bytes, temp_bytes):
    need = 2 * sum(block_bytes) + scratch_bytes + temp_bytes
    return int(min(max(need, 16 * 1024 * 1024), V7X_VMEM_BYTES - 4 * 1024 * 1024))


def _nbytes(shape, dtype):
    n = 1
    for s in shape:
        n *= s
    return n * jnp.dtype(dtype).itemsize


def _log_sigmoid(x):
    return jnp.minimum(x, 0.0) - jnp.log1p(jnp.exp(-jnp.abs(x)))


def _sigmoid(x):
    return 1.0 / (1.0 + jnp.exp(-x))


def _rms_scale(x):
    return lax.rsqrt(jnp.mean(x * x, axis=-1, keepdims=True) + EPS)


def _norm_gate_kernel(x_ref, g_ref, wg_ref, bias_ref, xn_ref, gate_ref):
    xn = (x_ref[...] * _rms_scale(x_ref[...]) * g_ref[...]).astype(BF16)
    xn_ref[...] = xn
    pre = lax.dot_general(xn, wg_ref[...], NT_DIMS, preferred_element_type=F32) + bias_ref[...]
    gate_ref[...] = GATE_CAP * jnp.tanh(pre / GATE_CAP)


def _norm_kernel(x_ref, g_ref, xn_ref):
    xn_ref[...] = (x_ref[...] * _rms_scale(x_ref[...]) * g_ref[...]).astype(BF16)


def _norm_rows(x2d, gain, w_gate=None, bias=None, *, tm):
    M, D = x2d.shape
    with_gates = w_gate is not None
    in_specs = [pl.BlockSpec((tm, D), lambda i: (i, 0)), pl.BlockSpec((1, D), lambda i: (0, 0))]
    out_specs = [pl.BlockSpec((tm, D), lambda i: (i, 0))]
    out_shape = [jax.ShapeDtypeStruct((M, D), BF16)]
    args = [x2d, gain]
    if with_gates:
        in_specs += [pl.BlockSpec((LANES, D), lambda i: (0, 0)), pl.BlockSpec((1, LANES), lambda i: (0, 0))]
        out_specs.append(pl.BlockSpec((tm, LANES), lambda i: (i, 0)))
        out_shape.append(jax.ShapeDtypeStruct((M, LANES), F32))
        args += [w_gate, bias]
    blocks = [_nbytes((tm, D), F32), _nbytes((tm, D), BF16), _nbytes((LANES, D), BF16), _nbytes((tm, LANES), F32)]
    out = pl.pallas_call(
        _norm_gate_kernel if with_gates else _norm_kernel,
        grid=(M // tm,),
        in_specs=in_specs,
        out_specs=out_specs,
        out_shape=out_shape,
        compiler_params=pltpu.CompilerParams(
            dimension_semantics=("arbitrary",),
            vmem_limit_bytes=_vmem_limit(blocks, 0, _nbytes((tm, D), F32))),
        name="norm_gate" if with_gates else "norm",
    )(*args)
    return out if with_gates else out[0]


def _store_with_rope(acc, z_ref, cos_ref, sin_ref, j, n_rope_tiles, tn):
    @pl.when(j < n_rope_tiles)
    def _():
        c = cos_ref[...]
        s = sin_ref[...]
        for t in range(tn // HEAD_DIM):
            a = acc[:, t * HEAD_DIM:(t + 1) * HEAD_DIM]
            z_ref[:, t * HEAD_DIM:(t + 1) * HEAD_DIM] = a * c + pltpu.roll(a, HEAD_DIM // 2, 1) * s

    @pl.when(j >= n_rope_tiles)
    def _():
        z_ref[...] = acc


def _inproj_prompt_kernel(xn_ref, w_ref, cos_ref, sin_ref, z_ref, wb_ref, *, n_rope_tiles, tn):
    @pl.when(pl.program_id(1) == 0)
    def _():
        wb_ref[...] = w_ref[...].astype(BF16)

    acc = lax.dot_general(xn_ref[...], wb_ref[...], NT_DIMS, preferred_element_type=F32)
    _store_with_rope(acc, z_ref, cos_ref, sin_ref, pl.program_id(0), n_rope_tiles, tn)


def _in_proj_prompt(xn, w_t, cosf, sinf, *, tm, tn, pos_tiles):
    M, D = xn.shape
    blocks = [_nbytes((tm, D), BF16), _nbytes((tn, D), F32), 2 * _nbytes((tm, LANES), F32),
              _nbytes((tm, tn), F32), _nbytes((tn, D), BF16)]
    return pl.pallas_call(
        functools.partial(_inproj_prompt_kernel, n_rope_tiles=2 * ATT_WIDTH // tn, tn=tn),
        grid=(N_MAIN // tn, M // tm),
        in_specs=[
            pl.BlockSpec((tm, D), lambda j, i: (i, 0)),
            pl.BlockSpec((tn, D), lambda j, i: (j, 0)),
            pl.BlockSpec((tm, LANES), lambda j, i: (i % pos_tiles, 0)),
            pl.BlockSpec((tm, LANES), lambda j, i: (i % pos_tiles, 0)),
        ],
        out_specs=[pl.BlockSpec((tm, tn), lambda j, i: (i, j)),
                   pl.BlockSpec((tn, D), lambda j, i: (j, 0))],
        out_shape=[jax.ShapeDtypeStruct((M, N_MAIN), F32), jax.ShapeDtypeStruct((N_MAIN, D), BF16)],
        compiler_params=pltpu.CompilerParams(
            dimension_semantics=("arbitrary", "arbitrary"),
            vmem_limit_bytes=_vmem_limit(blocks, 0, 2 * _nbytes((tm, tn), F32))),
        name="in_proj_prompt",
    )(xn, w_t, cosf, sinf)


def _upproj_prompt_kernel(xn_ref, w_ref, u_ref, wb_ref):
    @pl.when(pl.program_id(1) == 0)
    def _():
        wb_ref[...] = w_ref[...].astype(BF16)

    acc = jnp.dot(xn_ref[...], wb_ref[...], preferred_element_type=F32)
    u_ref[...] = jnp.square(jnp.maximum(acc, 0.0)).astype(u_ref.dtype)


def _up_proj_prompt(xn, w_up, *, tm, tn):
    M, D = xn.shape
    N = w_up.shape[1]
    blocks = [_nbytes((tm, D), BF16), _nbytes((D, tn), F32), _nbytes((tm, tn), BF16), _nbytes((D, tn), BF16)]
    return pl.pallas_call(
        _upproj_prompt_kernel,
        grid=(N // tn, M // tm),
        in_specs=[pl.BlockSpec((tm, D), lambda j, i: (i, 0)),
                  pl.BlockSpec((D, tn), lambda j, i: (0, j))],
        out_specs=[pl.BlockSpec((tm, tn), lambda j, i: (i, j)),
                   pl.BlockSpec((D, tn), lambda j, i: (0, j))],
        out_shape=[jax.ShapeDtypeStruct((M, N), BF16), jax.ShapeDtypeStruct((D, N), BF16)],
        compiler_params=pltpu.CompilerParams(
            dimension_semantics=("arbitrary", "arbitrary"),
            vmem_limit_bytes=_vmem_limit(blocks, 0, 2 * _nbytes((tm, tn), F32))),
        name="up_proj_prompt",
    )(xn, w_up)


def _outproj_prompt_kernel(att_ref, hn_ref, w_ref, x_ref, h_ref, wb_ref):
    @pl.when(pl.program_id(1) == 0)
    def _():
        wb_ref[...] = w_ref[...].astype(BF16)

    h_ref[...] = (x_ref[...]
                  + jnp.dot(att_ref[...], wb_ref[:ATT_WIDTH, :], preferred_element_type=F32)
                  + jnp.dot(hn_ref[...], wb_ref[ATT_WIDTH:, :], preferred_element_type=F32))


def _out_proj_prompt(att2d, hn2d, w_out, x2d, *, tm, tn):
    M, D = x2d.shape
    K = w_out.shape[0]
    blocks = [_nbytes((tm, K), BF16), _nbytes((K, tn), F32), 2 * _nbytes((tm, tn), F32), _nbytes((K, tn), BF16)]
    return pl.pallas_call(
        _outproj_prompt_kernel,
        grid=(D // tn, M // tm),
        in_specs=[pl.BlockSpec((tm, ATT_WIDTH), lambda j, i: (i, 0)),
                  pl.BlockSpec((tm, MLSTM_WIDTH), lambda j, i: (i, 0)),
                  pl.BlockSpec((K, tn), lambda j, i: (0, j)),
                  pl.BlockSpec((tm, tn), lambda j, i: (i, j))],
        out_specs=[pl.BlockSpec((tm, tn), lambda j, i: (i, j)),
                   pl.BlockSpec((K, tn), lambda j, i: (0, j))],
        out_shape=[jax.ShapeDtypeStruct((M, D), F32), jax.ShapeDtypeStruct((K, D), BF16)],
        compiler_params=pltpu.CompilerParams(
            dimension_semantics=("arbitrary", "arbitrary"),
            vmem_limit_bytes=_vmem_limit(blocks, 0, 2 * _nbytes((tm, tn), F32))),
        name="out_proj_prompt",
    )(att2d, hn2d, w_out, x2d)


def _inproj_kernel(x_ref, g_ref, w_ref, wg_ref, bias_ref, cos_ref, sin_ref,
                   z_ref, gate_ref, xn_ref, *, n_rope_tiles, tn):
    j = pl.program_id(1)

    @pl.when(j == 0)
    def _():
        x = x_ref[...]
        xn = (x * _rms_scale(x) * g_ref[...]).astype(BF16)
        xn_ref[...] = xn
        pre = lax.dot_general(xn, wg_ref[...], NT_DIMS, preferred_element_type=F32) + bias_ref[...]
        gate_ref[...] = GATE_CAP * jnp.tanh(pre / GATE_CAP)

    acc = lax.dot_general(xn_ref[...], w_ref[...], NT_DIMS, preferred_element_type=F32)
    _store_with_rope(acc, z_ref, cos_ref, sin_ref, j, n_rope_tiles, tn)


def _in_proj(x2d, gain, w_main, w_gate, bias, cosf, sinf, *, tm, tn, pos_tiles):
    M, D = x2d.shape
    N = w_main.shape[0]
    blocks = [_nbytes((tm, D), F32), _nbytes((tn, D), BF16), _nbytes((LANES, D), BF16),
              2 * _nbytes((tm, LANES), F32), _nbytes((tm, tn), F32), _nbytes((tm, LANES), F32)]
    return pl.pallas_call(
        functools.partial(_inproj_kernel, n_rope_tiles=2 * ATT_WIDTH // tn, tn=tn),
        grid=(M // tm, N // tn),
        in_specs=[
            pl.BlockSpec((tm, D), lambda i, j: (i, 0)),
            pl.BlockSpec((1, D), lambda i, j: (0, 0)),
            pl.BlockSpec((tn, D), lambda i, j: (j, 0)),
            pl.BlockSpec((LANES, D), lambda i, j: (0, 0)),
            pl.BlockSpec((1, LANES), lambda i, j: (0, 0)),
            pl.BlockSpec((tm, LANES), lambda i, j: (i % pos_tiles, 0)),
            pl.BlockSpec((tm, LANES), lambda i, j: (i % pos_tiles, 0)),
        ],
        out_specs=[pl.BlockSpec((tm, tn), lambda i, j: (i, j)),
                   pl.BlockSpec((tm, LANES), lambda i, j: (i, 0))],
        out_shape=[jax.ShapeDtypeStruct((M, N), F32), jax.ShapeDtypeStruct((M, LANES), F32)],
        scratch_shapes=[pltpu.VMEM((tm, D), BF16)],
        compiler_params=pltpu.CompilerParams(
            dimension_semantics=("arbitrary", "arbitrary"),
            vmem_limit_bytes=_vmem_limit(blocks, _nbytes((tm, D), BF16),
                                         _nbytes((tm, D), F32) + 2 * _nbytes((tm, tn), F32))),
        name="in_proj",
    )(x2d, gain, w_main, w_gate, bias, cosf, sinf)


def _attn_prompt_kernel(q_ref, k_ref, v_ref, o_ref, m_s, l_s, acc_s, *, T):
    scale = HEAD_DIM ** -0.5
    G = ATT_GROUP
    row = lax.broadcasted_iota(jnp.int32, (G * Q_TILE, 2 * Q_TILE), 0) & (Q_TILE - 1)
    col = lax.broadcasted_iota(jnp.int32, (G * Q_TILE, 2 * Q_TILE), 1)
    band = jnp.logical_and(col >= row, col - Q_TILE <= row)
    prev_cols = lax.broadcasted_iota(jnp.int32, (Q_TILE, 2 * Q_TILE), 1) < Q_TILE
    nt = (((1,), (1,)), ((), ()))

    for ci, (_, r) in enumerate(DILATED_CONFIGS):
        tiles_per_class = T // (r * Q_TILE)
        run = min(G, tiles_per_class)
        runs_per_group = G // run
        runs_per_class = tiles_per_class // run
        shift = runs_per_class.bit_length() - 1
        n_groups = T // (Q_TILE * G)

        def rows(ref, start, n, r=r):
            if r == 1:
                return ref[pl.ds(start, n), :]
            return ref[pl.ds(start, n, stride=r), :]

        def put(ref, start, n, val, r=r):
            if r == 1:
                ref[pl.ds(start, n), :] = val
            else:
                ref[pl.ds(start, n, stride=r), :] = val

        def body(gi, carry, r=r, ci=ci, run=run, rpg=runs_per_group, rpc=runs_per_class,
                 shift=shift, rows=rows, put=put):
            n = run * Q_TILE
            starts, vxs, scores = [], [], []
            for j in range(rpg):
                ridx = gi * rpg + j
                cls = lax.shift_right_logical(ridx, shift)
                u0 = (ridx & (rpc - 1)) * n
                start = cls + r * u0
                pstart = cls + r * jnp.maximum(u0 - Q_TILE, 0)
                no_prev = u0 == 0
                q = (rows(q_ref, start, n) * scale).astype(BF16)
                kx = jnp.concatenate([rows(k_ref, pstart, Q_TILE), rows(k_ref, start, n)], axis=0).astype(BF16)
                vx = jnp.concatenate([rows(v_ref, pstart, Q_TILE), rows(v_ref, start, n)], axis=0).astype(BF16)
                for g in range(run):
                    s = lax.dot_general(q[g * Q_TILE:(g + 1) * Q_TILE], kx[g * Q_TILE:(g + 2) * Q_TILE], nt,
                                        preferred_element_type=F32)
                    if g == 0:
                        s = jnp.where(jnp.logical_and(prev_cols, no_prev), NEG_INF, s)
                    scores.append(s)
                starts.append(start)
                vxs.append(vx)
            s_all = jnp.where(band, jnp.concatenate(scores, axis=0), NEG_INF)
            mt = jnp.max(s_all, axis=1, keepdims=True)
            if ci == 0:
                m_new = jnp.broadcast_to(mt, (G * Q_TILE, LANES))
            else:
                m_old = jnp.concatenate([rows(m_s, st, n) for st in starts], axis=0)
                m_new = jnp.maximum(m_old, mt)
                alpha = jnp.exp(m_old - m_new)
            p = jnp.exp(s_all - jnp.concatenate([m_new, m_new], axis=1))
            lt = jnp.sum(p, axis=1, keepdims=True)
            pb = p.astype(BF16)
            outs = []
            for j in range(rpg):
                for g in range(run):
                    t = j * run + g
                    outs.append(jnp.dot(pb[t * Q_TILE:(t + 1) * Q_TILE], vxs[j][g * Q_TILE:(g + 2) * Q_TILE],
                                        preferred_element_type=F32))
            ot = jnp.concatenate(outs, axis=0)
            if ci == 0:
                l_new = jnp.broadcast_to(lt, (G * Q_TILE, LANES))
            else:
                l_new = alpha * jnp.concatenate([rows(l_s, st, n) for st in starts], axis=0) + lt
                ot = alpha * jnp.concatenate([rows(acc_s, st, n) for st in starts], axis=0) + ot
            for j, st in enumerate(starts):
                put(m_s, st, n, m_new[j * n:(j + 1) * n])
                put(l_s, st, n, l_new[j * n:(j + 1) * n])
                put(acc_s, st, n, ot[j * n:(j + 1) * n])
            return carry

        lax.fori_loop(0, n_groups, body, 0)

    o_ref[...] = (acc_s[...] / l_s[...]).astype(o_ref.dtype)


def _attn_prompt(z3):
    B, T, _ = z3.shape
    assert T % (DILATED_CONFIGS[-1][1] * Q_TILE) == 0
    blk = (None, T, HEAD_DIM)
    blocks = [3 * _nbytes((T, HEAD_DIM), F32), _nbytes((T, HEAD_DIM), BF16)]
    return pl.pallas_call(
        functools.partial(_attn_prompt_kernel, T=T),
        grid=(B, ATT_HEADS),
        in_specs=[pl.BlockSpec(blk, lambda b, h: (b, 0, h)),
                  pl.BlockSpec(blk, lambda b, h: (b, 0, ATT_HEADS + h)),
                  pl.BlockSpec(blk, lambda b, h: (b, 0, 2 * ATT_HEADS + h))],
        out_specs=pl.BlockSpec(blk, lambda b, h: (b, 0, h)),
        out_shape=jax.ShapeDtypeStruct((B, T, ATT_WIDTH), BF16),
        scratch_shapes=[pltpu.VMEM((T, LANES), F32)] * 3,
        compiler_params=pltpu.CompilerParams(
            dimension_semantics=("arbitrary", "arbitrary"),
            vmem_limit_bytes=_vmem_limit(blocks, 3 * _nbytes((T, LANES), F32),
                                         2 * _nbytes((T, LANES), F32))),
        name="attn_prompt",
    )(z3, z3, z3)


def _mlstm_prompt_kernel(q_ref, k_ref, v_ref, om_ref, lir_ref, fgr_ref, lic_ref, fgc_ref, mhg_ref,
                         hn_ref, c_out, n_out, m_out, c_s, n_s, m_s, *, L):
    c = pl.program_id(2)

    @pl.when(c == 0)
    def _():
        c_s[...] = jnp.zeros_like(c_s)
        n_s[...] = jnp.zeros_like(n_s)
        m_s[...] = jnp.zeros_like(m_s)

    li_row = lir_ref[...]
    li_col = lic_ref[...]
    lf_row = _log_sigmoid(fgr_ref[...])
    lf_col = _log_sigmoid(fgc_ref[...])
    r_idx = lax.broadcasted_iota(jnp.int32, (L, L), 0)
    c_idx = lax.broadcasted_iota(jnp.int32, (L, L), 1)
    causal = c_idx <= r_idx
    b_col = jnp.sum(jnp.where(causal, lf_row, 0.0), axis=1, keepdims=True)
    b_row = jnp.sum(jnp.where(r_idx <= c_idx, lf_col, 0.0), axis=0, keepdims=True)

    m_prev = m_s[...]
    dlog = jnp.where(causal, b_col - b_row + li_row, NEG_INF)
    inter = b_col + m_prev
    m_t = jnp.maximum(inter, jnp.max(dlog, axis=1, keepdims=True))
    dmat = jnp.exp(dlog - m_t)
    a = jnp.exp(inter - m_t)

    qf = q_ref[...] * (MLSTM_DQK ** -0.5)
    qb = qf.astype(BF16)
    kf = k_ref[...]
    kb = kf.astype(BF16)
    vb = v_ref[...].astype(BF16)
    s = lax.dot_general(qb, kb, (((1,), (1,)), ((), ())), preferred_element_type=F32)
    sw = s * dmat
    cb = c_s[...].astype(BF16)
    num = (jnp.dot(sw.astype(BF16), vb, preferred_element_type=F32)
           + a * jnp.dot(qb, cb, preferred_element_type=F32))
    den = (jnp.sum(sw, axis=1, keepdims=True)
           + a * jnp.sum(qf * n_s[...], axis=1, keepdims=True))
    h = num / jnp.maximum(jnp.abs(den), jnp.exp(-m_t))

    m_new = m_t[L - 1:L, :]
    b_last = b_col[L - 1:L, :]
    ws = jnp.exp(b_last - b_col + li_col - m_new)
    decay = jnp.exp(b_last + m_prev - m_new)
    kw = kf * ws
    kwt = kw.T.astype(BF16)
    c_s[...] = decay * c_s[...] + jnp.dot(kwt, vb, preferred_element_type=F32)
    n_s[...] = decay * n_s[...] + jnp.sum(kw, axis=0, keepdims=True)
    m_s[...] = m_new

    hn = h * _rms_scale(h) * mhg_ref[...] * _sigmoid(om_ref[...])
    hn_ref[...] = hn.astype(hn_ref.dtype)

    @pl.when(c == pl.num_programs(2) - 1)
    def _():
        c_out[...] = c_s[...]
        n_out[...] = n_s[...]
        m_out[...] = m_s[...]


def _mlstm_prompt(z3, li_row, fg_row, li_col, fg_col, mhg, *, L):
    B, T, _ = z3.shape
    H = MLSTM_HEADS
    q_off = 3 * ATT_WIDTH // MLSTM_DQK
    k_off = q_off + H
    v_off = (3 * ATT_WIDTH + 2 * H * MLSTM_DQK) // MLSTM_DV
    o_off = v_off + H
    blocks = [2 * _nbytes((L, MLSTM_DQK), F32), 2 * _nbytes((L, MLSTM_DV), F32),
              4 * _nbytes((L, LANES), F32), _nbytes((L, MLSTM_DV), BF16),
              _nbytes((MLSTM_DQK, MLSTM_DV), F32)]
    return pl.pallas_call(
        functools.partial(_mlstm_prompt_kernel, L=L),
        grid=(B, H, T // L),
        in_specs=[
            pl.BlockSpec((None, L, MLSTM_DQK), lambda b, h, c: (b, c, q_off + h)),
            pl.BlockSpec((None, L, MLSTM_DQK), lambda b, h, c: (b, c, k_off + h)),
            pl.BlockSpec((None, L, MLSTM_DV), lambda b, h, c: (b, c, v_off + h)),
            pl.BlockSpec((None, L, MLSTM_DV), lambda b, h, c: (b, c, o_off + h)),
            pl.BlockSpec((None, None, 1, L), lambda b, h, c: (b, h, 0, c)),
            pl.BlockSpec((None, None, 1, L), lambda b, h, c: (b, h, 0, c)),
            pl.BlockSpec((None, None, L, 1), lambda b, h, c: (b, h, c, 0)),
            pl.BlockSpec((None, None, L, 1), lambda b, h, c: (b, h, c, 0)),
            pl.BlockSpec((1, MLSTM_DV), lambda b, h, c: (0, h)),
        ],
        out_specs=[
            pl.BlockSpec((None, L, MLSTM_DV), lambda b, h, c: (b, c, h)),
            pl.BlockSpec((None, None, MLSTM_DQK, MLSTM_DV), lambda b, h, c: (b, h, 0, 0)),
            pl.BlockSpec((None, None, 1, MLSTM_DQK), lambda b, h, c: (b, h, 0, 0)),
            pl.BlockSpec((None, None, 1, 1), lambda b, h, c: (b, h, 0, 0)),
        ],
        out_shape=[
            jax.ShapeDtypeStruct((B, T, MLSTM_WIDTH), BF16),
            jax.ShapeDtypeStruct((B, H, MLSTM_DQK, MLSTM_DV), F32),
            jax.ShapeDtypeStruct((B, H, 1, MLSTM_DQK), F32),
            jax.ShapeDtypeStruct((B, H, 1, 1), F32),
        ],
        scratch_shapes=[pltpu.VMEM((MLSTM_DQK, MLSTM_DV), F32),
                        pltpu.VMEM((1, MLSTM_DQK), F32),
                        pltpu.VMEM((1, 1), F32)],
        compiler_params=pltpu.CompilerParams(
            dimension_semantics=("arbitrary", "arbitrary", "arbitrary"),
            vmem_limit_bytes=_vmem_limit(blocks, _nbytes((MLSTM_DQK, MLSTM_DV), F32),
                                         8 * _nbytes((L, MLSTM_DV), F32) + 8 * _nbytes((L, L), F32))),
        name="mlstm_prompt",
    )(z3, z3, z3, z3, li_row, fg_row, li_col, fg_col, mhg)


def _outproj_kernel(mix_ref, w_ref, x_ref, h_ref):
    h_ref[...] = x_ref[...] + jnp.dot(mix_ref[...], w_ref[...], preferred_element_type=F32)


def _out_proj(mix2d, w, x2d, *, tm, tn):
    M, D = x2d.shape
    K = w.shape[0]
    blocks = [_nbytes((tm, K), BF16), _nbytes((K, tn), BF16), 2 * _nbytes((tm, tn), F32)]
    return pl.pallas_call(
        _outproj_kernel,
        grid=(M // tm, D // tn),
        in_specs=[
            pl.BlockSpec((tm, K), lambda i, j: (i, 0)),
            pl.BlockSpec((K, tn), lambda i, j: (0, j)),
            pl.BlockSpec((tm, tn), lambda i, j: (i, j)),
        ],
        out_specs=pl.BlockSpec((tm, tn), lambda i, j: (i, j)),
        out_shape=jax.ShapeDtypeStruct((M, D), F32),
        compiler_params=pltpu.CompilerParams(
            dimension_semantics=("arbitrary", "arbitrary"),
            vmem_limit_bytes=_vmem_limit(blocks, 0, 2 * _nbytes((tm, tn), F32))),
        name="out_proj",
    )(mix2d, w, x2d)


def _upproj_kernel(x_ref, g_ref, w_ref, u_ref, xn_ref):
    @pl.when(pl.program_id(1) == 0)
    def _():
        x = x_ref[...]
        xn_ref[...] = (x * _rms_scale(x) * g_ref[...]).astype(BF16)

    acc = jnp.dot(xn_ref[...], w_ref[...], preferred_element_type=F32)
    u_ref[...] = jnp.square(jnp.maximum(acc, 0.0)).astype(u_ref.dtype)


def _up_proj(h2d, gain, w_up, *, tm, tn):
    M, D = h2d.shape
    N = w_up.shape[1]
    blocks = [_nbytes((tm, D), F32), _nbytes((D, tn), BF16), _nbytes((tm, tn), BF16)]
    return pl.pallas_call(
        _upproj_kernel,
        grid=(M // tm, N // tn),
        in_specs=[pl.BlockSpec((tm, D), lambda i, j: (i, 0)),
                  pl.BlockSpec((1, D), lambda i, j: (0, 0)),
                  pl.BlockSpec((D, tn), lambda i, j: (0, j))],
        out_specs=pl.BlockSpec((tm, tn), lambda i, j: (i, j)),
        out_shape=jax.ShapeDtypeStruct((M, N), BF16),
        scratch_shapes=[pltpu.VMEM((tm, D), BF16)],
        compiler_params=pltpu.CompilerParams(
            dimension_semantics=("arbitrary", "arbitrary"),
            vmem_limit_bytes=_vmem_limit(blocks, _nbytes((tm, D), BF16),
                                         _nbytes((tm, D), F32) + 2 * _nbytes((tm, tn), F32))),
        name="up_proj",
    )(h2d, gain, w_up)


def _downproj_kernel(u_ref, w_ref, h_ref, g_ref, y_ref):
    k = pl.program_id(1)

    @pl.when(k == 0)
    def _():
        y_ref[...] = h_ref[...]

    u = u_ref[...]
    for n in range(0, y_ref.shape[1], ACC_COLS):
        y_ref[:, n:n + ACC_COLS] += jnp.dot(u, w_ref[:, n:n + ACC_COLS], preferred_element_type=F32)

    @pl.when(k == pl.num_programs(1) - 1)
    def _():
        scale = _rms_scale(y_ref[...])
        y_ref[...] = y_ref[...] * scale * g_ref[...]


def _down_proj(u2d, w_down, h2d, gain, *, tm, tk):
    M, D = h2d.shape
    K = u2d.shape[1]
    blocks = [_nbytes((tm, tk), BF16), _nbytes((tk, D), BF16), 2 * _nbytes((tm, D), F32)]
    return pl.pallas_call(
        _downproj_kernel,
        grid=(M // tm, K // tk),
        in_specs=[pl.BlockSpec((tm, tk), lambda i, k: (i, k)),
                  pl.BlockSpec((tk, D), lambda i, k: (k, 0)),
                  pl.BlockSpec((tm, D), lambda i, k: (i, 0)),
                  pl.BlockSpec((1, D), lambda i, k: (0, 0))],
        out_specs=pl.BlockSpec((tm, D), lambda i, k: (i, 0)),
        out_shape=jax.ShapeDtypeStruct((M, D), F32),
        compiler_params=pltpu.CompilerParams(
            dimension_semantics=("arbitrary", "arbitrary"),
            vmem_limit_bytes=_vmem_limit(blocks, 0, _nbytes((tm, D), F32) + _nbytes((tm, ACC_COLS), F32))),
        name="down_proj",
    )(u2d, w_down, h2d, gain)


def _attn_sample_kernel(q_ref, kn_ref, vn_ref, k1_ref, k4_ref, k16_ref, v1_ref, v4_ref, v16_ref, o_ref):
    q = q_ref[...] * (HEAD_DIM ** -0.5)
    kn = kn_ref[...]
    vn = vn_ref[...]
    s0 = jnp.sum(q * kn, axis=-1, keepdims=True)

    ms, dens, nums = [], [], []
    for kt_ref, vt_ref in ((k1_ref, v1_ref), (k4_ref, v4_ref), (k16_ref, v16_ref)):
        kt = kt_ref[...]
        vt = vt_ref[...]
        s = jnp.sum(q[None] * kt, axis=-1, keepdims=True)
        m = jnp.maximum(jnp.max(s, axis=0), s0)
        p = jnp.exp(s - m[None])
        p0 = jnp.exp(s0 - m)
        dens.append(jnp.sum(p, axis=0) + p0)
        nums.append(jnp.sum(p * vt, axis=0) + p0 * vn)
        ms.append(m)
    m_all = jnp.maximum(jnp.maximum(ms[0], ms[1]), ms[2])
    num = jnp.zeros((ATT_HEADS, HEAD_DIM), F32)
    den = jnp.zeros((ATT_HEADS, 1), F32)
    for m, d, n in zip(ms, dens, nums):
        wgt = jnp.exp(m - m_all)
        num = num + wgt * n
        den = den + wgt * d
    o_ref[...] = (num / den).astype(o_ref.dtype)


def _attn_sample(qs, kns, vns, cache_k, cache_v):
    Bd, W, H, hd = cache_k.shape
    new_spec = pl.BlockSpec((None, H, hd), lambda b: (b, 0, 0))
    in_specs = [new_spec, new_spec, new_spec]
    views_k, views_v = [], []
    for (w, r) in DILATED_CONFIGS:
        assert w // r == Q_TILE and W % w == 0 and PAST_LEN >= W >= w
        views_k.append(cache_k.reshape(Bd, W // r, r, H, hd))
        views_v.append(cache_v.reshape(Bd, W // r, r, H, hd))
    for _ in range(2):
        for (w, r) in DILATED_CONFIGS:
            last = W // w - 1
            in_specs.append(pl.BlockSpec((None, Q_TILE, None, H, hd),
                                         lambda b, last=last: (b, last, 0, 0, 0)))
    blocks = [3 * _nbytes((H, hd), F32), 6 * _nbytes((Q_TILE, H, hd), F32)]
    return pl.pallas_call(
        _attn_sample_kernel,
        grid=(Bd,),
        in_specs=in_specs,
        out_specs=pl.BlockSpec((None, H, hd), lambda b: (b, 0, 0)),
        out_shape=jax.ShapeDtypeStruct((Bd, H, hd), BF16),
        compiler_params=pltpu.CompilerParams(
            dimension_semantics=("arbitrary",),
            vmem_limit_bytes=_vmem_limit(blocks, 0, 8 * _nbytes((Q_TILE, H, hd), F32))),
        name="attn_sample",
    )(qs, kns, vns, *views_k, *views_v)


def _qk_sample_kernel(x_ref, g_ref, w_ref, o_ref, xn_ref):
    @pl.when(pl.program_id(0) == 0)
    def _():
        x = x_ref[...]
        xn_ref[...] = x * _rms_scale(x) * g_ref[...]

    o_ref[...] = lax.dot_general(xn_ref[...], w_ref[...], NT_DIMS, precision=lax.Precision.HIGHEST,
                                 preferred_element_type=F32)


def _qk_sample(x2d, gain, w_t, *, tn=512):
    Bd, D = x2d.shape
    n_out = 2 * MLSTM_HEADS * MLSTM_DQK
    first = 3 * ATT_WIDTH // tn
    blocks = [_nbytes((Bd, D), F32), _nbytes((tn, D), F32), _nbytes((Bd, tn), F32)]
    return pl.pallas_call(
        _qk_sample_kernel,
        grid=(n_out // tn,),
        in_specs=[pl.BlockSpec((Bd, D), lambda j: (0, 0)),
                  pl.BlockSpec((1, D), lambda j: (0, 0)),
                  pl.BlockSpec((tn, D), lambda j: (first + j, 0))],
        out_specs=pl.BlockSpec((Bd, tn), lambda j: (0, j)),
        out_shape=jax.ShapeDtypeStruct((Bd, n_out), F32),
        scratch_shapes=[pltpu.VMEM((Bd, D), F32)],
        compiler_params=pltpu.CompilerParams(
            dimension_semantics=("arbitrary",),
            vmem_limit_bytes=_vmem_limit(blocks, _nbytes((Bd, D), F32), 4 * _nbytes((tn, D), BF16))),
        name="qk_sample",
    )(x2d, gain, w_t)


def _mlstm_sample_kernel(q_ref, k_ref, v_ref, om_ref, g_ref, c0_ref, n0_ref, m0_ref, mhg_ref,
                         hn_ref, c_out, n_out, m_out):
    H = MLSTM_HEADS
    for h in range(H):
        qf = q_ref[h:h + 1, :] * (MLSTM_DQK ** -0.5)
        kf = k_ref[h:h + 1, :]
        vf = v_ref[h:h + 1, :]
        li = g_ref[:, h:h + 1]
        lf = _log_sigmoid(g_ref[:, H + h:H + h + 1])
        m0 = m0_ref[:, h:h + 1]
        c0 = c0_ref[h]
        n0 = n0_ref[h:h + 1, :]

        inter = lf + m0
        m_t = jnp.maximum(inter, li)
        qk = jnp.sum(qf * kf, axis=1, keepdims=True)
        sw = qk * jnp.exp(li - m_t)
        a = jnp.exp(inter - m_t)
        qc = jnp.dot(jnp.broadcast_to(qf, (8, MLSTM_DQK)), c0, precision=lax.Precision.HIGHEST,
                     preferred_element_type=F32)[0:1, :]
        num = sw * vf + a * qc
        den = sw + a * jnp.sum(qf * n0, axis=1, keepdims=True)
        hh = num / jnp.maximum(jnp.abs(den), jnp.exp(-m_t))

        ws = jnp.exp(li - m_t)
        decay = jnp.exp(inter - m_t)
        kw = kf * ws
        kcol = jnp.broadcast_to(kw, (LANES, MLSTM_DQK)).T[:, 0:1]
        c_out[h] = decay * c0 + kcol * vf
        n_out[h:h + 1, :] = decay * n0 + kw
        m_out[:, h:h + 1] = m_t

        hn = hh * _rms_scale(hh) * mhg_ref[h:h + 1, :] * _sigmoid(om_ref[h:h + 1, :])
        hn_ref[h:h + 1, :] = hn.astype(hn_ref.dtype)


def _mlstm_sample(qs, ks, vs, oms, gs, c0, n0, m0, mhg):
    Bd = qs.shape[0]
    H = MLSTM_HEADS
    blocks = [2 * _nbytes((8, MLSTM_DQK), F32), 2 * _nbytes((8, MLSTM_DV), F32),
              2 * _nbytes((H, MLSTM_DQK, MLSTM_DV), F32), 4 * _nbytes((8, MLSTM_DV), F32)]
    return pl.pallas_call(
        _mlstm_sample_kernel,
        grid=(Bd,),
        in_specs=[
            pl.BlockSpec((None, H, MLSTM_DQK), lambda b: (b, 0, 0)),
            pl.BlockSpec((None, H, MLSTM_DQK), lambda b: (b, 0, 0)),
            pl.BlockSpec((None, H, MLSTM_DV), lambda b: (b, 0, 0)),
            pl.BlockSpec((None, H, MLSTM_DV), lambda b: (b, 0, 0)),
            pl.BlockSpec((None, 1, LANES), lambda b: (b, 0, 0)),
            pl.BlockSpec((None, H, MLSTM_DQK, MLSTM_DV), lambda b: (b, 0, 0, 0)),
            pl.BlockSpec((None, H, MLSTM_DQK), lambda b: (b, 0, 0)),
            pl.BlockSpec((None, 1, H), lambda b: (b, 0, 0)),
            pl.BlockSpec((H, MLSTM_DV), lambda b: (0, 0)),
        ],
        out_specs=[
            pl.BlockSpec((None, H, MLSTM_DV), lambda b: (b, 0, 0)),
            pl.BlockSpec((None, H, MLSTM_DQK, MLSTM_DV), lambda b: (b, 0, 0, 0)),
            pl.BlockSpec((None, H, MLSTM_DQK), lambda b: (b, 0, 0)),
            pl.BlockSpec((None, 1, H), lambda b: (b, 0, 0)),
        ],
        out_shape=[
            jax.ShapeDtypeStruct((Bd, H, MLSTM_DV), BF16),
            jax.ShapeDtypeStruct((Bd, H, MLSTM_DQK, MLSTM_DV), F32),
            jax.ShapeDtypeStruct((Bd, H, MLSTM_DQK), F32),
            jax.ShapeDtypeStruct((Bd, 1, H), F32),
        ],
        compiler_params=pltpu.CompilerParams(
            dimension_semantics=("arbitrary",),
            vmem_limit_bytes=_vmem_limit(blocks, 0, 4 * _nbytes((MLSTM_DQK, MLSTM_DV), F32))),
        name="mlstm_sample",
    )(qs, ks, vs, oms, gs, c0, n0, m0, mhg)


def _rope_tables(pos):
    half = HEAD_DIM // 2
    inv = ROPE_THETA ** (-jnp.arange(half, dtype=F32) / half)
    ang = pos.astype(F32)[:, None] * inv[None, :]
    cos = jnp.cos(ang)
    sin = jnp.sin(ang)
    return jnp.concatenate([cos, cos], axis=-1), jnp.concatenate([-sin, sin], axis=-1)


def _row_tile(M, cap):
    tm = min(M, cap)
    assert M % tm == 0
    return tm


def kernel(x_prompt, x_sample, cache_k, cache_v, state_C, state_n, state_m,
           norm1_g, w_in, b_if, mh_norm_g, w_out, norm2_g, w_up, w_down, final_g):
    B, T, D = x_prompt.shape
    Bd, Td, _ = x_sample.shape
    depth = w_in.shape[0]
    assert depth == 1 and Td == 1 and D == D_MODEL
    keep = min(WIN_MAX, T)
    H = MLSTM_HEADS

    w_in_t = jnp.transpose(w_in[0])
    w_gate = jnp.pad(w_in_t[N_MAIN:], ((0, LANES - 2 * H), (0, 0))).astype(BF16)
    bias = jnp.pad(b_if[0], (0, LANES - 2 * H)).reshape(1, LANES)
    wd = w_down[0].astype(BF16)
    g1 = norm1_g[0].reshape(1, D)
    g2 = norm2_g[0].reshape(1, D)
    gf = final_g.reshape(1, D)
    mhg = mh_norm_g[0]

    cos_p, sin_p = _rope_tables(jnp.arange(T, dtype=jnp.int32))
    cos_s, sin_s = _rope_tables(jnp.full((Bd,), PAST_LEN, dtype=jnp.int32))

    xp2 = x_prompt.reshape(B * T, D)
    tm = _row_tile(T, 512)
    tm_w = _row_tile(T, 1024)
    xn_p, gates = _norm_rows(xp2, g1, w_gate, bias, tm=tm)
    z, w_main = _in_proj_prompt(xn_p, w_in_t, cos_p, sin_p, tm=tm_w, tn=512, pos_tiles=T // tm_w)
    z3 = z.reshape(B, T, N_MAIN)
    att = _attn_prompt(z3)

    g3 = gates.reshape(B, T, LANES)
    li_row = jnp.swapaxes(g3[:, :, :H], 1, 2).reshape(B, H, 1, T)
    fg_row = jnp.swapaxes(g3[:, :, H:2 * H], 1, 2).reshape(B, H, 1, T)
    li_col = li_row.reshape(B, H, T, 1)
    fg_col = fg_row.reshape(B, H, T, 1)
    hn, c_p, n_p, m_p = _mlstm_prompt(z3, li_row, fg_row, li_col, fg_col, mhg.reshape(1, MLSTM_WIDTH),
                                      L=min(T, 256))

    h_p, wo = _out_proj_prompt(att.reshape(B * T, ATT_WIDTH), hn.reshape(B * T, MLSTM_WIDTH), w_out[0], xp2,
                               tm=tm_w, tn=512)
    hn2_p = _norm_rows(h_p, g2, tm=tm)
    u_p, wu = _up_proj_prompt(hn2_p, w_up[0], tm=tm_w, tn=512)
    y_p = _down_proj(u_p, wd, h_p, gf, tm=tm, tk=1024)

    k_prompt = z3[:, T - keep:, ATT_WIDTH:2 * ATT_WIDTH].reshape(1, B, keep, ATT_HEADS, HEAD_DIM)
    v_prompt = z3[:, T - keep:, 2 * ATT_WIDTH:3 * ATT_WIDTH].reshape(1, B, keep, ATT_HEADS, HEAD_DIM)

    xs2 = x_sample.reshape(Bd, D)
    zs, gates_s = _in_proj(xs2, g1, w_main, w_gate, bias, cos_s, sin_s, tm=Bd, tn=1024, pos_tiles=1)
    qa_s = zs[:, :ATT_WIDTH].reshape(Bd, ATT_HEADS, HEAD_DIM)
    ka_s = zs[:, ATT_WIDTH:2 * ATT_WIDTH].reshape(Bd, ATT_HEADS, HEAD_DIM)
    va_s = zs[:, 2 * ATT_WIDTH:3 * ATT_WIDTH].reshape(Bd, ATT_HEADS, HEAD_DIM)
    att_s = _attn_sample(qa_s, ka_s, va_s, cache_k[0], cache_v[0])

    qk_s = _qk_sample(xs2, g1, w_in_t)
    qs = qk_s[:, :H * MLSTM_DQK].reshape(Bd, H, MLSTM_DQK)
    ks = qk_s[:, H * MLSTM_DQK:].reshape(Bd, H, MLSTM_DQK)
    o = 3 * ATT_WIDTH + 2 * H * MLSTM_DQK
    vs = zs[:, o:o + MLSTM_WIDTH].reshape(Bd, H, MLSTM_DV)
    o += MLSTM_WIDTH
    oms = zs[:, o:o + MLSTM_WIDTH].reshape(Bd, H, MLSTM_DV)
    hn_s, c_s, n_s, m_s = _mlstm_sample(qs, ks, vs, oms, gates_s.reshape(Bd, 1, LANES),
                                        state_C[0], state_n[0], state_m[0].reshape(Bd, 1, H),
                                        mhg.reshape(H, MLSTM_DV))

    mix_s = jnp.concatenate([att_s.reshape(Bd, ATT_WIDTH), hn_s.reshape(Bd, MLSTM_WIDTH)], axis=-1)
    h_s = _out_proj(mix_s, wo, xs2, tm=Bd, tn=1024)
    u_s = _up_proj(h_s, g2, wu, tm=Bd, tn=1024)
    y_s = _down_proj(u_s, wd, h_s, gf, tm=Bd, tk=1024)

    k_sample = zs[:, ATT_WIDTH:2 * ATT_WIDTH].reshape(1, Bd, 1, ATT_HEADS, HEAD_DIM)
    v_sample = zs[:, 2 * ATT_WIDTH:3 * ATT_WIDTH].reshape(1, Bd, 1, ATT_HEADS, HEAD_DIM)

    return (y_p.reshape(B, T, D), y_s.reshape(Bd, 1, D),
            k_prompt, v_prompt,
            c_p[None], n_p.reshape(1, B, H, MLSTM_DQK), m_p.reshape(1, B, H),
            k_sample, v_sample,
            c_s[None], n_s[None], m_s.reshape(1, Bd, H))
```

```python
import functools

import jax
import jax.numpy as jnp
from jax import lax
from jax.experimental import pallas as pl
from jax.experimental.pallas import tpu as pltpu

F32 = jnp.float32
BF16 = jnp.bfloat16

D_MODEL = 4096
HEAD_DIM = 128
ATT_HEADS = 8
ATT_WIDTH = ATT_HEADS * HEAD_DIM
MLSTM_HEADS = 6
MLSTM_DQK = 256
MLSTM_DV = 512
MLSTM_WIDTH = MLSTM_HEADS * MLSTM_DV
N_MAIN = 3 * ATT_WIDTH + 2 * MLSTM_HEADS * MLSTM_DQK + 2 * MLSTM_WIDTH
GATE_CAP = 15.0
DILATED_CONFIGS = ((128, 1), (512, 4), (2048, 16))
WIN_MAX = 2048
ROPE_THETA = 10000.0
EPS = 1e-6
PAST_LEN = 8192
NEG_INF = float("-inf")

LANES = 128
V7X_VMEM_BYTES = 64 * 1024 * 1024
Q_TILE = 128
ACC_COLS = 512
ATT_GROUP = 4
NT_DIMS = (((1,), (1,)), ((), ()))
BF16_SUBLANES = 16


def _vmem_limit(block_bytes, scratch_bytes, temp_bytes):
    need = 2 * sum(block_bytes) + scratch_bytes + temp_bytes
    return int(min(max(need, 16 * 1024 * 1024), V7X_VMEM_BYTES - 4 * 1024 * 1024))


def _nbytes(shape, dtype):
    n = 1
    for s in shape:
        n *= s
    return n * jnp.dtype(dtype).itemsize


def _log_sigmoid(x):
    return jnp.minimum(x, 0.0) - jnp.log1p(jnp.exp(-jnp.abs(x)))


def _sigmoid(x):
    return 1.0 / (1.0 + jnp.exp(-x))


def _rms_scale(x):
    return lax.rsqrt(jnp.mean(x * x, axis=-1, keepdims=True) + EPS)


def _norm_gate_kernel(x_ref, g_ref, wg_ref, bias_ref, xn_ref, gate_ref):
    xn = (x_ref[...] * _rms_scale(x_ref[...]) * g_ref[...]).astype(BF16)
    xn_ref[...] = xn
    pre = lax.dot_general(xn, wg_ref[...], NT_DIMS, preferred_element_type=F32) + bias_ref[...]
    gate_ref[...] = GATE_CAP * jnp.tanh(pre / GATE_CAP)


def _norm_kernel(x_ref, g_ref, xn_ref):
    xn_ref[...] = (x_ref[...] * _rms_scale(x_ref[...]) * g_ref[...]).astype(BF16)


def _norm_rows(x2d, gain, w_gate=None, bias=None, *, tm):
    M, D = x2d.shape
    with_gates = w_gate is not None
    in_specs = [pl.BlockSpec((tm, D), lambda i: (i, 0)), pl.BlockSpec((1, D), lambda i: (0, 0))]
    out_specs = [pl.BlockSpec((tm, D), lambda i: (i, 0))]
    out_shape = [jax.ShapeDtypeStruct((M, D), BF16)]
    args = [x2d, gain]
    if with_gates:
        in_specs += [pl.BlockSpec((LANES, D), lambda i: (0, 0)), pl.BlockSpec((1, LANES), lambda i: (0, 0))]
        out_specs.append(pl.BlockSpec((tm, LANES), lambda i: (i, 0)))
        out_shape.append(jax.ShapeDtypeStruct((M, LANES), F32))
        args += [w_gate, bias]
    blocks = [_nbytes((tm, D), F32), _nbytes((tm, D), BF16), _nbytes((LANES, D), BF16), _nbytes((tm, LANES), F32)]
    out = pl.pallas_call(
        _norm_gate_kernel if with_gates else _norm_kernel,
        grid=(M // tm,),
        in_specs=in_specs,
        out_specs=out_specs,
        out_shape=out_shape,
        compiler_params=pltpu.CompilerParams(
            dimension_semantics=("arbitrary",),
            vmem_limit_bytes=_vmem_limit(blocks, 0, _nbytes((tm, D), F32))),
        name="norm_gate" if with_gates else "norm",
    )(*args)
    return out if with_gates else out[0]


def _store_with_rope(acc, z_ref, cos_ref, sin_ref, j, n_rope_tiles, tn):
    @pl.when(j < n_rope_tiles)
    def _():
        c = cos_ref[...]
        s = sin_ref[...]
        for t in range(tn // HEAD_DIM):
            a = acc[:, t * HEAD_DIM:(t + 1) * HEAD_DIM]
            z_ref[:, t * HEAD_DIM:(t + 1) * HEAD_DIM] = a * c + pltpu.roll(a, HEAD_DIM // 2, 1) * s

    @pl.when(j >= n_rope_tiles)
    def _():
        z_ref[...] = acc


class _SideCast:
    def __init__(self, weights, n_steps, n_inner):
        rows = weights[0].shape[0]
        assert all(w.shape[0] == rows for w in weights)
        self.weights = weights
        self.n_inner = n_inner
        self.n_blocks = min(1 << (n_steps.bit_length() - 1), rows // BF16_SUBLANES)
        self.rows = rows // self.n_blocks

    def _index(self, a, b):
        return (jnp.minimum(a * self.n_inner + b, self.n_blocks - 1), 0)

    def specs(self):
        return [pl.BlockSpec((self.rows, w.shape[1]), self._index) for w in self.weights]

    def out_shapes(self):
        return [jax.ShapeDtypeStruct(w.shape, BF16) for w in self.weights]

    def block_bytes(self):
        return [_nbytes((self.rows, w.shape[1]), F32) + _nbytes((self.rows, w.shape[1]), BF16)
                for w in self.weights]

    def run(self, src_refs, dst_refs):
        step = pl.program_id(0) * self.n_inner + pl.program_id(1)

        @pl.when(step < self.n_blocks)
        def _():
            for src, dst in zip(src_refs, dst_refs):
                dst[...] = src[...].astype(dst.dtype)


def _inproj_prompt_kernel(xn_ref, w_ref, cos_ref, sin_ref, *refs, n_rope_tiles, tn, side):
    n_side = len(side.weights)
    side_in, (z_ref, wb_ref), side_out = refs[:n_side], refs[n_side:n_side + 2], refs[n_side + 2:]

    @pl.when(pl.program_id(1) == 0)
    def _():
        wb_ref[...] = w_ref[...].astype(BF16)

    acc = lax.dot_general(xn_ref[...], wb_ref[...], NT_DIMS, preferred_element_type=F32)
    _store_with_rope(acc, z_ref, cos_ref, sin_ref, pl.program_id(0), n_rope_tiles, tn)
    side.run(side_in, side_out)


def _in_proj_prompt(xn, w_t, cosf, sinf, side_weights, *, tm, tn, pos_tiles):
    M, D = xn.shape
    grid = (N_MAIN // tn, M // tm)
    side = _SideCast(side_weights, grid[0] * grid[1], grid[1])
    blocks = [_nbytes((tm, D), BF16), _nbytes((tn, D), F32), 2 * _nbytes((tm, LANES), F32),
              _nbytes((tm, tn), F32), _nbytes((tn, D), BF16)] + side.block_bytes()
    return pl.pallas_call(
        functools.partial(_inproj_prompt_kernel, n_rope_tiles=2 * ATT_WIDTH // tn, tn=tn, side=side),
        grid=grid,
        in_specs=[
            pl.BlockSpec((tm, D), lambda j, i: (i, 0)),
            pl.BlockSpec((tn, D), lambda j, i: (j, 0)),
            pl.BlockSpec((tm, LANES), lambda j, i: (i % pos_tiles, 0)),
            pl.BlockSpec((tm, LANES), lambda j, i: (i % pos_tiles, 0)),
        ] + side.specs(),
        out_specs=[pl.BlockSpec((tm, tn), lambda j, i: (i, j)),
                   pl.BlockSpec((tn, D), lambda j, i: (j, 0))] + side.specs(),
        out_shape=[jax.ShapeDtypeStruct((M, N_MAIN), F32),
                   jax.ShapeDtypeStruct((N_MAIN, D), BF16)] + side.out_shapes(),
        compiler_params=pltpu.CompilerParams(
            dimension_semantics=("arbitrary", "arbitrary"),
            vmem_limit_bytes=_vmem_limit(blocks, 0, 2 * _nbytes((tm, tn), F32))),
        name="in_proj_prompt",
    )(xn, w_t, cosf, sinf, *side_weights)


def _upproj_prompt_kernel(xn_ref, w_ref, *refs, side):
    n_side = len(side.weights)
    side_in, u_ref, side_out = refs[:n_side], refs[n_side], refs[n_side + 1:]
    acc = jnp.dot(xn_ref[...], w_ref[...], preferred_element_type=F32)
    u_ref[...] = jnp.square(jnp.maximum(acc, 0.0)).astype(u_ref.dtype)
    side.run(side_in, side_out)


def _up_proj_prompt(xn, w_up, side_weights, *, tm, tn):
    M, D = xn.shape
    N = w_up.shape[1]
    grid = (M // tm, N // tn)
    side = _SideCast(side_weights, grid[0] * grid[1], grid[1])
    blocks = [_nbytes((tm, D), BF16), _nbytes((D, tn), BF16), _nbytes((tm, tn), BF16)] + side.block_bytes()
    return pl.pallas_call(
        functools.partial(_upproj_prompt_kernel, side=side),
        grid=grid,
        in_specs=[pl.BlockSpec((tm, D), lambda i, j: (i, 0)),
                  pl.BlockSpec((D, tn), lambda i, j: (0, j))] + side.specs(),
        out_specs=[pl.BlockSpec((tm, tn), lambda i, j: (i, j))] + side.specs(),
        out_shape=[jax.ShapeDtypeStruct((M, N), BF16)] + side.out_shapes(),
        compiler_params=pltpu.CompilerParams(
            dimension_semantics=("arbitrary", "arbitrary"),
            vmem_limit_bytes=_vmem_limit(blocks, 0, 2 * _nbytes((tm, tn), F32))),
        name="up_proj_prompt",
    )(xn, w_up, *side_weights)


def _outproj_prompt_kernel(att_ref, hn_ref, w_ref, x_ref, h_ref):
    h_ref[...] = (x_ref[...]
                  + jnp.dot(att_ref[...], w_ref[:ATT_WIDTH, :], preferred_element_type=F32)
                  + jnp.dot(hn_ref[...], w_ref[ATT_WIDTH:, :], preferred_element_type=F32))


def _out_proj_prompt(att2d, hn2d, w, x2d, *, tm, tn):
    M, D = x2d.shape
    K = w.shape[0]
    blocks = [_nbytes((tm, K), BF16), _nbytes((K, tn), BF16), 2 * _nbytes((tm, tn), F32)]
    return pl.pallas_call(
        _outproj_prompt_kernel,
        grid=(M // tm, D // tn),
        in_specs=[pl.BlockSpec((tm, ATT_WIDTH), lambda i, j: (i, 0)),
                  pl.BlockSpec((tm, MLSTM_WIDTH), lambda i, j: (i, 0)),
                  pl.BlockSpec((K, tn), lambda i, j: (0, j)),
                  pl.BlockSpec((tm, tn), lambda i, j: (i, j))],
        out_specs=pl.BlockSpec((tm, tn), lambda i, j: (i, j)),
        out_shape=jax.ShapeDtypeStruct((M, D), F32),
        compiler_params=pltpu.CompilerParams(
            dimension_semantics=("arbitrary", "arbitrary"),
            vmem_limit_bytes=_vmem_limit(blocks, 0, 3 * _nbytes((tm, tn), F32))),
        name="out_proj_prompt",
    )(att2d, hn2d, w, x2d)


def _inproj_kernel(x_ref, g_ref, w_ref, wg_ref, bias_ref, cos_ref, sin_ref,
                   z_ref, gate_ref, xn_ref, *, n_rope_tiles, tn):
    j = pl.program_id(1)

    @pl.when(j == 0)
    def _():
        x = x_ref[...]
        xn = (x * _rms_scale(x) * g_ref[...]).astype(BF16)
        xn_ref[...] = xn
        pre = lax.dot_general(xn, wg_ref[...], NT_DIMS, preferred_element_type=F32) + bias_ref[...]
        gate_ref[...] = GATE_CAP * jnp.tanh(pre / GATE_CAP)

    acc = lax.dot_general(xn_ref[...], w_ref[...], NT_DIMS, preferred_element_type=F32)
    _store_with_rope(acc, z_ref, cos_ref, sin_ref, j, n_rope_tiles, tn)


def _in_proj(x2d, gain, w_main, w_gate, bias, cosf, sinf, *, tm, tn, pos_tiles):
    M, D = x2d.shape
    N = w_main.shape[0]
    blocks = [_nbytes((tm, D), F32), _nbytes((tn, D), BF16), _nbytes((LANES, D), BF16),
              2 * _nbytes((tm, LANES), F32), _nbytes((tm, tn), F32), _nbytes((tm, LANES), F32)]
    return pl.pallas_call(
        functools.partial(_inproj_kernel, n_rope_tiles=2 * ATT_WIDTH // tn, tn=tn),
        grid=(M // tm, N // tn),
        in_specs=[
            pl.BlockSpec((tm, D), lambda i, j: (i, 0)),
            pl.BlockSpec((1, D), lambda i, j: (0, 0)),
            pl.BlockSpec((tn, D), lambda i, j: (j, 0)),
            pl.BlockSpec((LANES, D), lambda i, j: (0, 0)),
            pl.BlockSpec((1, LANES), lambda i, j: (0, 0)),
            pl.BlockSpec((tm, LANES), lambda i, j: (i % pos_tiles, 0)),
            pl.BlockSpec((tm, LANES), lambda i, j: (i % pos_tiles, 0)),
        ],
        out_specs=[pl.BlockSpec((tm, tn), lambda i, j: (i, j)),
                   pl.BlockSpec((tm, LANES), lambda i, j: (i, 0))],
        out_shape=[jax.ShapeDtypeStruct((M, N), F32), jax.ShapeDtypeStruct((M, LANES), F32)],
        scratch_shapes=[pltpu.VMEM((tm, D), BF16)],
        compiler_params=pltpu.CompilerParams(
            dimension_semantics=("arbitrary", "arbitrary"),
            vmem_limit_bytes=_vmem_limit(blocks, _nbytes((tm, D), BF16),
                                         _nbytes((tm, D), F32) + 2 * _nbytes((tm, tn), F32))),
        name="in_proj",
    )(x2d, gain, w_main, w_gate, bias, cosf, sinf)


def _attn_prompt_kernel(q_ref, k_ref, v_ref, o_ref, m_s, l_s, acc_s, *, T):
    scale = HEAD_DIM ** -0.5
    G = ATT_GROUP
    row = lax.broadcasted_iota(jnp.int32, (G * Q_TILE, 2 * Q_TILE), 0) & (Q_TILE - 1)
    col = lax.broadcasted_iota(jnp.int32, (G * Q_TILE, 2 * Q_TILE), 1)
    band = jnp.logical_and(col >= row, col - Q_TILE <= row)
    prev_cols = lax.broadcasted_iota(jnp.int32, (Q_TILE, 2 * Q_TILE), 1) < Q_TILE
    nt = (((1,), (1,)), ((), ()))

    for ci, (_, r) in enumerate(DILATED_CONFIGS):
        tiles_per_class = T // (r * Q_TILE)
        run = min(G, tiles_per_class)
        runs_per_group = G // run
        runs_per_class = tiles_per_class // run
        shift = runs_per_class.bit_length() - 1
        n_groups = T // (Q_TILE * G)

        def rows(ref, start, n, r=r):
            if r == 1:
                return ref[pl.ds(start, n), :]
            return ref[pl.ds(start, n, stride=r), :]

        def put(ref, start, n, val, r=r):
            if r == 1:
                ref[pl.ds(start, n), :] = val
            else:
                ref[pl.ds(start, n, stride=r), :] = val

        def body(gi, carry, r=r, ci=ci, run=run, rpg=runs_per_group, rpc=runs_per_class,
                 shift=shift, rows=rows, put=put):
            n = run * Q_TILE
            starts, vxs, scores = [], [], []
            for j in range(rpg):
                ridx = gi * rpg + j
                cls = lax.shift_right_logical(ridx, shift)
                u0 = (ridx & (rpc - 1)) * n
                start = cls + r * u0
                pstart = cls + r * jnp.maximum(u0 - Q_TILE, 0)
                no_prev = u0 == 0
                q = (rows(q_ref, start, n) * scale).astype(BF16)
                kx = jnp.concatenate([rows(k_ref, pstart, Q_TILE), rows(k_ref, start, n)], axis=0).astype(BF16)
                vx = jnp.concatenate([rows(v_ref, pstart, Q_TILE), rows(v_ref, start, n)], axis=0).astype(BF16)
                for g in range(run):
                    s = lax.dot_general(q[g * Q_TILE:(g + 1) * Q_TILE], kx[g * Q_TILE:(g + 2) * Q_TILE], nt,
                                        preferred_element_type=F32)
                    if g == 0:
                        s = jnp.where(jnp.logical_and(prev_cols, no_prev), NEG_INF, s)
                    scores.append(s)
                starts.append(start)
                vxs.append(vx)
            s_all = jnp.where(band, jnp.concatenate(scores, axis=0), NEG_INF)
            mt = jnp.max(s_all, axis=1, keepdims=True)
            if ci == 0:
                m_new = jnp.broadcast_to(mt, (G * Q_TILE, LANES))
            else:
                m_old = jnp.concatenate([rows(m_s, st, n) for st in starts], axis=0)
                m_new = jnp.maximum(m_old, mt)
                alpha = jnp.exp(m_old - m_new)
            p = jnp.exp(s_all - jnp.concatenate([m_new, m_new], axis=1))
            lt = jnp.sum(p, axis=1, keepdims=True)
            pb = p.astype(BF16)
            outs = []
            for j in range(rpg):
                for g in range(run):
                    t = j * run + g
                    outs.append(jnp.dot(pb[t * Q_TILE:(t + 1) * Q_TILE], vxs[j][g * Q_TILE:(g + 2) * Q_TILE],
                                        preferred_element_type=F32))
            ot = jnp.concatenate(outs, axis=0)
            if ci == 0:
                l_new = jnp.broadcast_to(lt, (G * Q_TILE, LANES))
            else:
                l_new = alpha * jnp.concatenate([rows(l_s, st, n) for st in starts], axis=0) + lt
                ot = alpha * jnp.concatenate([rows(acc_s, st, n) for st in starts], axis=0) + ot
            for j, st in enumerate(starts):
                put(m_s, st, n, m_new[j * n:(j + 1) * n])
                put(l_s, st, n, l_new[j * n:(j + 1) * n])
                put(acc_s, st, n, ot[j * n:(j + 1) * n])
            return carry

        lax.fori_loop(0, n_groups, body, 0)

    o_ref[...] = (acc_s[...] / l_s[...]).astype(o_ref.dtype)


def _attn_prompt(z3):
    B, T, _ = z3.shape
    assert T % (DILATED_CONFIGS[-1][1] * Q_TILE) == 0
    blk = (None, T, HEAD_DIM)
    blocks = [3 * _nbytes((T, HEAD_DIM), F32), _nbytes((T, HEAD_DIM), BF16)]
    return pl.pallas_call(
        functools.partial(_attn_prompt_kernel, T=T),
        grid=(B, ATT_HEADS),
        in_specs=[pl.BlockSpec(blk, lambda b, h: (b, 0, h)),
                  pl.BlockSpec(blk, lambda b, h: (b, 0, ATT_HEADS + h)),
                  pl.BlockSpec(blk, lambda b, h: (b, 0, 2 * ATT_HEADS + h))],
        out_specs=pl.BlockSpec(blk, lambda b, h: (b, 0, h)),
        out_shape=jax.ShapeDtypeStruct((B, T, ATT_WIDTH), BF16),
        scratch_shapes=[pltpu.VMEM((T, LANES), F32)] * 3,
        compiler_params=pltpu.CompilerParams(
            dimension_semantics=("arbitrary", "arbitrary"),
            vmem_limit_bytes=_vmem_limit(blocks, 3 * _nbytes((T, LANES), F32),
                                         2 * _nbytes((T, LANES), F32))),
        name="attn_prompt",
    )(z3, z3, z3)


def _mlstm_prompt_kernel(q_ref, k_ref, v_ref, om_ref, lir_ref, fgr_ref, lic_ref, fgc_ref, mhg_ref,
                         hn_ref, c_out, n_out, m_out, c_s, n_s, m_s, *, L):
    c = pl.program_id(2)

    @pl.when(c == 0)
    def _():
        c_s[...] = jnp.zeros_like(c_s)
        n_s[...] = jnp.zeros_like(n_s)
        m_s[...] = jnp.zeros_like(m_s)

    li_row = lir_ref[...]
    li_col = lic_ref[...]
    lf_row = _log_sigmoid(fgr_ref[...])
    lf_col = _log_sigmoid(fgc_ref[...])
    r_idx = lax.broadcasted_iota(jnp.int32, (L, L), 0)
    c_idx = lax.broadcasted_iota(jnp.int32, (L, L), 1)
    causal = c_idx <= r_idx
    b_col = jnp.sum(jnp.where(causal, lf_row, 0.0), axis=1, keepdims=True)
    b_row = jnp.sum(jnp.where(r_idx <= c_idx, lf_col, 0.0), axis=0, keepdims=True)

    m_prev = m_s[...]
    dlog = jnp.where(causal, b_col - b_row + li_row, NEG_INF)
    inter = b_col + m_prev
    m_t = jnp.maximum(inter, jnp.max(dlog, axis=1, keepdims=True))
    dmat = jnp.exp(dlog - m_t)
    a = jnp.exp(inter - m_t)

    qf = q_ref[...] * (MLSTM_DQK ** -0.5)
    qb = qf.astype(BF16)
    kf = k_ref[...]
    kb = kf.astype(BF16)
    vb = v_ref[...].astype(BF16)
    s = lax.dot_general(qb, kb, (((1,), (1,)), ((), ())), preferred_element_type=F32)
    sw = s * dmat
    cb = c_s[...].astype(BF16)
    num = (jnp.dot(sw.astype(BF16), vb, preferred_element_type=F32)
           + a * jnp.dot(qb, cb, preferred_element_type=F32))
    den = (jnp.sum(sw, axis=1, keepdims=True)
           + a * jnp.sum(qf * n_s[...], axis=1, keepdims=True))
    h = num / jnp.maximum(jnp.abs(den), jnp.exp(-m_t))

    m_new = m_t[L - 1:L, :]
    b_last = b_col[L - 1:L, :]
    ws = jnp.exp(b_last - b_col + li_col - m_new)
    decay = jnp.exp(b_last + m_prev - m_new)
    kw = kf * ws
    kwt = kw.T.astype(BF16)
    c_s[...] = decay * c_s[...] + jnp.dot(kwt, vb, preferred_element_type=F32)
    n_s[...] = decay * n_s[...] + jnp.sum(kw, axis=0, keepdims=True)
    m_s[...] = m_new

    hn = h * _rms_scale(h) * mhg_ref[...] * _sigmoid(om_ref[...])
    hn_ref[...] = hn.astype(hn_ref.dtype)

    @pl.when(c == pl.num_programs(2) - 1)
    def _():
        c_out[...] = c_s[...]
        n_out[...] = n_s[...]
        m_out[...] = m_s[...]


def _mlstm_prompt(z3, li_row, fg_row, li_col, fg_col, mhg, *, L):
    B, T, _ = z3.shape
    H = MLSTM_HEADS
    q_off = 3 * ATT_WIDTH // MLSTM_DQK
    k_off = q_off + H
    v_off = (3 * ATT_WIDTH + 2 * H * MLSTM_DQK) // MLSTM_DV
    o_off = v_off + H
    blocks = [2 * _nbytes((L, MLSTM_DQK), F32), 2 * _nbytes((L, MLSTM_DV), F32),
              4 * _nbytes((L, LANES), F32), _nbytes((L, MLSTM_DV), BF16),
              _nbytes((MLSTM_DQK, MLSTM_DV), F32)]
    return pl.pallas_call(
        functools.partial(_mlstm_prompt_kernel, L=L),
        grid=(B, H, T // L),
        in_specs=[
            pl.BlockSpec((None, L, MLSTM_DQK), lambda b, h, c: (b, c, q_off + h)),
            pl.BlockSpec((None, L, MLSTM_DQK), lambda b, h, c: (b, c, k_off + h)),
            pl.BlockSpec((None, L, MLSTM_DV), lambda b, h, c: (b, c, v_off + h)),
            pl.BlockSpec((None, L, MLSTM_DV), lambda b, h, c: (b, c, o_off + h)),
            pl.BlockSpec((None, None, 1, L), lambda b, h, c: (b, h, 0, c)),
            pl.BlockSpec((None, None, 1, L), lambda b, h, c: (b, h, 0, c)),
            pl.BlockSpec((None, None, L, 1), lambda b, h, c: (b, h, c, 0)),
            pl.BlockSpec((None, None, L, 1), lambda b, h, c: (b, h, c, 0)),
            pl.BlockSpec((1, MLSTM_DV), lambda b, h, c: (0, h)),
        ],
        out_specs=[
            pl.BlockSpec((None, L, MLSTM_DV), lambda b, h, c: (b, c, h)),
            pl.BlockSpec((None, None, MLSTM_DQK, MLSTM_DV), lambda b, h, c: (b, h, 0, 0)),
            pl.BlockSpec((None, None, 1, MLSTM_DQK), lambda b, h, c: (b, h, 0, 0)),
            pl.BlockSpec((None, None, 1, 1), lambda b, h, c: (b, h, 0, 0)),
        ],
        out_shape=[
            jax.ShapeDtypeStruct((B, T, MLSTM_WIDTH), BF16),
            jax.ShapeDtypeStruct((B, H, MLSTM_DQK, MLSTM_DV), F32),
            jax.ShapeDtypeStruct((B, H, 1, MLSTM_DQK), F32),
            jax.ShapeDtypeStruct((B, H, 1, 1), F32),
        ],
        scratch_shapes=[pltpu.VMEM((MLSTM_DQK, MLSTM_DV), F32),
                        pltpu.VMEM((1, MLSTM_DQK), F32),
                        pltpu.VMEM((1, 1), F32)],
        compiler_params=pltpu.CompilerParams(
            dimension_semantics=("arbitrary", "arbitrary", "arbitrary"),
            vmem_limit_bytes=_vmem_limit(blocks, _nbytes((MLSTM_DQK, MLSTM_DV), F32),
                                         8 * _nbytes((L, MLSTM_DV), F32) + 8 * _nbytes((L, L), F32))),
        name="mlstm_prompt",
    )(z3, z3, z3, z3, li_row, fg_row, li_col, fg_col, mhg)


def _outproj_kernel(mix_ref, w_ref, x_ref, h_ref):
    h_ref[...] = x_ref[...] + jnp.dot(mix_ref[...], w_ref[...], preferred_element_type=F32)


def _out_proj(mix2d, w, x2d, *, tm, tn):
    M, D = x2d.shape
    K = w.shape[0]
    blocks = [_nbytes((tm, K), BF16), _nbytes((K, tn), BF16), 2 * _nbytes((tm, tn), F32)]
    return pl.pallas_call(
        _outproj_kernel,
        grid=(M // tm, D // tn),
        in_specs=[
            pl.BlockSpec((tm, K), lambda i, j: (i, 0)),
            pl.BlockSpec((K, tn), lambda i, j: (0, j)),
            pl.BlockSpec((tm, tn), lambda i, j: (i, j)),
        ],
        out_specs=pl.BlockSpec((tm, tn), lambda i, j: (i, j)),
        out_shape=jax.ShapeDtypeStruct((M, D), F32),
        compiler_params=pltpu.CompilerParams(
            dimension_semantics=("arbitrary", "arbitrary"),
            vmem_limit_bytes=_vmem_limit(blocks, 0, 2 * _nbytes((tm, tn), F32))),
        name="out_proj",
    )(mix2d, w, x2d)


def _upproj_kernel(x_ref, g_ref, w_ref, u_ref, xn_ref):
    @pl.when(pl.program_id(1) == 0)
    def _():
        x = x_ref[...]
        xn_ref[...] = (x * _rms_scale(x) * g_ref[...]).astype(BF16)

    acc = jnp.dot(xn_ref[...], w_ref[...], preferred_element_type=F32)
    u_ref[...] = jnp.square(jnp.maximum(acc, 0.0)).astype(u_ref.dtype)


def _up_proj(h2d, gain, w_up, *, tm, tn):
    M, D = h2d.shape
    N = w_up.shape[1]
    blocks = [_nbytes((tm, D), F32), _nbytes((D, tn), BF16), _nbytes((tm, tn), BF16)]
    return pl.pallas_call(
        _upproj_kernel,
        grid=(M // tm, N // tn),
        in_specs=[pl.BlockSpec((tm, D), lambda i, j: (i, 0)),
                  pl.BlockSpec((1, D), lambda i, j: (0, 0)),
                  pl.BlockSpec((D, tn), lambda i, j: (0, j))],
        out_specs=pl.BlockSpec((tm, tn), lambda i, j: (i, j)),
        out_shape=jax.ShapeDtypeStruct((M, N), BF16),
        scratch_shapes=[pltpu.VMEM((tm, D), BF16)],
        compiler_params=pltpu.CompilerParams(
            dimension_semantics=("arbitrary", "arbitrary"),
            vmem_limit_bytes=_vmem_limit(blocks, _nbytes((tm, D), BF16),
                                         _nbytes((tm, D), F32) + 2 * _nbytes((tm, tn), F32))),
        name="up_proj",
    )(h2d, gain, w_up)


def _downproj_kernel(u_ref, w_ref, h_ref, g_ref, y_ref):
    k = pl.program_id(1)

    @pl.when(k == 0)
    def _():
        y_ref[...] = h_ref[...]

    u = u_ref[...]
    for n in range(0, y_ref.shape[1], ACC_COLS):
        y_ref[:, n:n + ACC_COLS] += jnp.dot(u, w_ref[:, n:n + ACC_COLS], preferred_element_type=F32)

    @pl.when(k == pl.num_programs(1) - 1)
    def _():
        scale = _rms_scale(y_ref[...])
        y_ref[...] = y_ref[...] * scale * g_ref[...]


def _down_proj(u2d, w_down, h2d, gain, *, tm, tk):
    M, D = h2d.shape
    K = u2d.shape[1]
    blocks = [_nbytes((tm, tk), BF16), _nbytes((tk, D), BF16), 2 * _nbytes((tm, D), F32)]
    return pl.pallas_call(
        _downproj_kernel,
        grid=(M // tm, K // tk),
        in_specs=[pl.BlockSpec((tm, tk), lambda i, k: (i, k)),
                  pl.BlockSpec((tk, D), lambda i, k: (k, 0)),
                  pl.BlockSpec((tm, D), lambda i, k: (i, 0)),
                  pl.BlockSpec((1, D), lambda i, k: (0, 0))],
        out_specs=pl.BlockSpec((tm, D), lambda i, k: (i, 0)),
        out_shape=jax.ShapeDtypeStruct((M, D), F32),
        compiler_params=pltpu.CompilerParams(
            dimension_semantics=("arbitrary", "arbitrary"),
            vmem_limit_bytes=_vmem_limit(blocks, 0, _nbytes((tm, D), F32) + _nbytes((tm, ACC_COLS), F32))),
        name="down_proj",
    )(u2d, w_down, h2d, gain)


def _attn_sample_kernel(q_ref, kn_ref, vn_ref, k1_ref, k4_ref, k16_ref, v1_ref, v4_ref, v16_ref, o_ref):
    q = q_ref[...] * (HEAD_DIM ** -0.5)
    kn = kn_ref[...]
    vn = vn_ref[...]
    s0 = jnp.sum(q * kn, axis=-1, keepdims=True)

    ms, dens, nums = [], [], []
    for kt_ref, vt_ref in ((k1_ref, v1_ref), (k4_ref, v4_ref), (k16_ref, v16_ref)):
        kt = kt_ref[...]
        vt = vt_ref[...]
        s = jnp.sum(q[None] * kt, axis=-1, keepdims=True)
        m = jnp.maximum(jnp.max(s, axis=0), s0)
        p = jnp.exp(s - m[None])
        p0 = jnp.exp(s0 - m)
        dens.append(jnp.sum(p, axis=0) + p0)
        nums.append(jnp.sum(p * vt, axis=0) + p0 * vn)
        ms.append(m)
    m_all = jnp.maximum(jnp.maximum(ms[0], ms[1]), ms[2])
    num = jnp.zeros((ATT_HEADS, HEAD_DIM), F32)
    den = jnp.zeros((ATT_HEADS, 1), F32)
    for m, d, n in zip(ms, dens, nums):
        wgt = jnp.exp(m - m_all)
        num = num + wgt * n
        den = den + wgt * d
    o_ref[...] = (num / den).astype(o_ref.dtype)


def _attn_sample(qs, kns, vns, cache_k, cache_v):
    Bd, W, H, hd = cache_k.shape
    new_spec = pl.BlockSpec((None, H, hd), lambda b: (b, 0, 0))
    in_specs = [new_spec, new_spec, new_spec]
    views_k, views_v = [], []
    for (w, r) in DILATED_CONFIGS:
        assert w // r == Q_TILE and W % w == 0 and PAST_LEN >= W >= w
        views_k.append(cache_k.reshape(Bd, W // r, r, H, hd))
        views_v.append(cache_v.reshape(Bd, W // r, r, H, hd))
    for _ in range(2):
        for (w, r) in DILATED_CONFIGS:
            last = W // w - 1
            in_specs.append(pl.BlockSpec((None, Q_TILE, None, H, hd),
                                         lambda b, last=last: (b, last, 0, 0, 0)))
    blocks = [3 * _nbytes((H, hd), F32), 6 * _nbytes((Q_TILE, H, hd), F32)]
    return pl.pallas_call(
        _attn_sample_kernel,
        grid=(Bd,),
        in_specs=in_specs,
        out_specs=pl.BlockSpec((None, H, hd), lambda b: (b, 0, 0)),
        out_shape=jax.ShapeDtypeStruct((Bd, H, hd), BF16),
        compiler_params=pltpu.CompilerParams(
            dimension_semantics=("arbitrary",),
            vmem_limit_bytes=_vmem_limit(blocks, 0, 8 * _nbytes((Q_TILE, H, hd), F32))),
        name="attn_sample",
    )(qs, kns, vns, *views_k, *views_v)


def _qk_sample_kernel(x_ref, g_ref, w_ref, o_ref, xn_ref):
    @pl.when(pl.program_id(0) == 0)
    def _():
        x = x_ref[...]
        xn_ref[...] = x * _rms_scale(x) * g_ref[...]

    o_ref[...] = lax.dot_general(xn_ref[...], w_ref[...], NT_DIMS, precision=lax.Precision.HIGHEST,
                                 preferred_element_type=F32)


def _qk_sample(x2d, gain, w_t, *, tn=512):
    Bd, D = x2d.shape
    n_out = 2 * MLSTM_HEADS * MLSTM_DQK
    first = 3 * ATT_WIDTH // tn
    blocks = [_nbytes((Bd, D), F32), _nbytes((tn, D), F32), _nbytes((Bd, tn), F32)]
    return pl.pallas_call(
        _qk_sample_kernel,
        grid=(n_out // tn,),
        in_specs=[pl.BlockSpec((Bd, D), lambda j: (0, 0)),
                  pl.BlockSpec((1, D), lambda j: (0, 0)),
                  pl.BlockSpec((tn, D), lambda j: (first + j, 0))],
        out_specs=pl.BlockSpec((Bd, tn), lambda j: (0, j)),
        out_shape=jax.ShapeDtypeStruct((Bd, n_out), F32),
        scratch_shapes=[pltpu.VMEM((Bd, D), F32)],
        compiler_params=pltpu.CompilerParams(
            dimension_semantics=("arbitrary",),
            vmem_limit_bytes=_vmem_limit(blocks, _nbytes((Bd, D), F32), 4 * _nbytes((tn, D), BF16))),
        name="qk_sample",
    )(x2d, gain, w_t)


def _mlstm_sample_kernel(q_ref, k_ref, v_ref, om_ref, g_ref, c0_ref, n0_ref, m0_ref, mhg_ref,
                         hn_ref, c_out, n_out, m_out):
    H = MLSTM_HEADS
    for h in range(H):
        qf = q_ref[h:h + 1, :] * (MLSTM_DQK ** -0.5)
        kf = k_ref[h:h + 1, :]
        vf = v_ref[h:h + 1, :]
        li = g_ref[:, h:h + 1]
        lf = _log_sigmoid(g_ref[:, H + h:H + h + 1])
        m0 = m0_ref[:, h:h + 1]
        c0 = c0_ref[h]
        n0 = n0_ref[h:h + 1, :]

        inter = lf + m0
        m_t = jnp.maximum(inter, li)
        qk = jnp.sum(qf * kf, axis=1, keepdims=True)
        sw = qk * jnp.exp(li - m_t)
        a = jnp.exp(inter - m_t)
        qc = jnp.dot(jnp.broadcast_to(qf, (8, MLSTM_DQK)), c0, precision=lax.Precision.HIGHEST,
                     preferred_element_type=F32)[0:1, :]
        num = sw * vf + a * qc
        den = sw + a * jnp.sum(qf * n0, axis=1, keepdims=True)
        hh = num / jnp.maximum(jnp.abs(den), jnp.exp(-m_t))

        ws = jnp.exp(li - m_t)
        decay = jnp.exp(inter - m_t)
        kw = kf * ws
        kcol = jnp.broadcast_to(kw, (LANES, MLSTM_DQK)).T[:, 0:1]
        c_out[h] = decay * c0 + kcol * vf
        n_out[h:h + 1, :] = decay * n0 + kw
        m_out[:, h:h + 1] = m_t

        hn = hh * _rms_scale(hh) * mhg_ref[h:h + 1, :] * _sigmoid(om_ref[h:h + 1, :])
        hn_ref[h:h + 1, :] = hn.astype(hn_ref.dtype)


def _mlstm_sample(qs, ks, vs, oms, gs, c0, n0, m0, mhg):
    Bd = qs.shape[0]
    H = MLSTM_HEADS
    blocks = [2 * _nbytes((8, MLSTM_DQK), F32), 2 * _nbytes((8, MLSTM_DV), F32),
              2 * _nbytes((H, MLSTM_DQK, MLSTM_DV), F32), 4 * _nbytes((8, MLSTM_DV), F32)]
    return pl.pallas_call(
        _mlstm_sample_kernel,
        grid=(Bd,),
        in_specs=[
            pl.BlockSpec((None, H, MLSTM_DQK), lambda b: (b, 0, 0)),
            pl.BlockSpec((None, H, MLSTM_DQK), lambda b: (b, 0, 0)),
            pl.BlockSpec((None, H, MLSTM_DV), lambda b: (b, 0, 0)),
            pl.BlockSpec((None, H, MLSTM_DV), lambda b: (b, 0, 0)),
            pl.BlockSpec((None, 1, LANES), lambda b: (b, 0, 0)),
            pl.BlockSpec((None, H, MLSTM_DQK, MLSTM_DV), lambda b: (b, 0, 0, 0)),
            pl.BlockSpec((None, H, MLSTM_DQK), lambda b: (b, 0, 0)),
            pl.BlockSpec((None, 1, H), lambda b: (b, 0, 0)),
            pl.BlockSpec((H, MLSTM_DV), lambda b: (0, 0)),
        ],
        out_specs=[
            pl.BlockSpec((None, H, MLSTM_DV), lambda b: (b, 0, 0)),
            pl.BlockSpec((None, H, MLSTM_DQK, MLSTM_DV), lambda b: (b, 0, 0, 0)),
            pl.BlockSpec((None, H, MLSTM_DQK), lambda b: (b, 0, 0)),
            pl.BlockSpec((None, 1, H), lambda b: (b, 0, 0)),
        ],
        out_shape=[
            jax.ShapeDtypeStruct((Bd, H, MLSTM_DV), BF16),
            jax.ShapeDtypeStruct((Bd, H, MLSTM_DQK, MLSTM_DV), F32),
            jax.ShapeDtypeStruct((Bd, H, MLSTM_DQK), F32),
            jax.ShapeDtypeStruct((Bd, 1, H), F32),
        ],
        compiler_params=pltpu.CompilerParams(
            dimension_semantics=("arbitrary",),
            vmem_limit_bytes=_vmem_limit(blocks, 0, 4 * _nbytes((MLSTM_DQK, MLSTM_DV), F32))),
        name="mlstm_sample",
    )(qs, ks, vs, oms, gs, c0, n0, m0, mhg)


def _rope_tables(pos):
    half = HEAD_DIM // 2
    inv = ROPE_THETA ** (-jnp.arange(half, dtype=F32) / half)
    ang = pos.astype(F32)[:, None] * inv[None, :]
    cos = jnp.cos(ang)
    sin = jnp.sin(ang)
    return jnp.concatenate([cos, cos], axis=-1), jnp.concatenate([-sin, sin], axis=-1)


def _row_tile(M, cap):
    tm = min(M, cap)
    assert M % tm == 0
    return tm


def kernel(x_prompt, x_sample, cache_k, cache_v, state_C, state_n, state_m,
           norm1_g, w_in, b_if, mh_norm_g, w_out, norm2_g, w_up, w_down, final_g):
    B, T, D = x_prompt.shape
    Bd, Td, _ = x_sample.shape
    depth = w_in.shape[0]
    assert depth == 1 and Td == 1 and D == D_MODEL
    keep = min(WIN_MAX, T)
    H = MLSTM_HEADS

    w_in_t = jnp.transpose(w_in[0])
    w_gate = jnp.pad(w_in_t[N_MAIN:], ((0, LANES - 2 * H), (0, 0))).astype(BF16)
    bias = jnp.pad(b_if[0], (0, LANES - 2 * H)).reshape(1, LANES)
    g1 =norm1_g[0].reshape(1, D)
    g2 = norm2_g[0].reshape(1, D)
    gf = final_g.reshape(1, D)
    mhg = mh_norm_g[0]

    cos_p, sin_p = _rope_tables(jnp.arange(T, dtype=jnp.int32))
    cos_s, sin_s = _rope_tables(jnp.full((Bd,), PAST_LEN, dtype=jnp.int32))

    xp2 = x_prompt.reshape(B * T, D)
    tm = _row_tile(T, 512)
    tm_w = _row_tile(T, 1024)
    xn_p, gates = _norm_rows(xp2, g1, w_gate, bias, tm=tm)
    z, w_main, wo, wu = _in_proj_prompt(xn_p, w_in_t, cos_p, sin_p, [w_out[0], w_up[0]],
                                        tm=tm_w, tn=512, pos_tiles=T // tm_w)
    z3 = z.reshape(B, T, N_MAIN)
    att = _attn_prompt(z3)

    g3 = gates.reshape(B, T, LANES)
    li_row = jnp.swapaxes(g3[:, :, :H], 1, 2).reshape(B, H, 1, T)
    fg_row = jnp.swapaxes(g3[:, :, H:2 * H], 1, 2).reshape(B, H, 1, T)
    li_col = li_row.reshape(B, H, T, 1)
    fg_col = fg_row.reshape(B, H, T, 1)
    hn, c_p, n_p, m_p = _mlstm_prompt(z3, li_row, fg_row, li_col, fg_col, mhg.reshape(1, MLSTM_WIDTH),
                                      L=min(T, 256))

    h_p = _out_proj_prompt(att.reshape(B * T, ATT_WIDTH), hn.reshape(B * T, MLSTM_WIDTH), wo, xp2,
                           tm=tm_w, tn=1024)
    hn2_p = _norm_rows(h_p, g2, tm=tm)
    u_p, wd = _up_proj_prompt(hn2_p, wu, [w_down[0]], tm=tm_w, tn=1024)
    y_p = _down_proj(u_p, wd, h_p, gf, tm=tm, tk=1024)

    k_prompt = z3[:, T - keep:, ATT_WIDTH:2 * ATT_WIDTH].reshape(1, B, keep, ATT_HEADS, HEAD_DIM)
    v_prompt = z3[:, T - keep:, 2 * ATT_WIDTH:3 * ATT_WIDTH].reshape(1, B, keep, ATT_HEADS, HEAD_DIM)

    xs2 = x_sample.reshape(Bd, D)
    zs, gates_s = _in_proj(xs2, g1, w_main, w_gate, bias, cos_s, sin_s, tm=Bd, tn=1024, pos_tiles=1)
    qa_s = zs[:, :ATT_WIDTH].reshape(Bd, ATT_HEADS, HEAD_DIM)
    ka_s = zs[:, ATT_WIDTH:2 * ATT_WIDTH].reshape(Bd, ATT_HEADS, HEAD_DIM)
    va_s = zs[:, 2 * ATT_WIDTH:3 * ATT_WIDTH].reshape(Bd, ATT_HEADS, HEAD_DIM)
    att_s = _attn_sample(qa_s, ka_s, va_s, cache_k[0], cache_v[0])

    qk_s = _qk_sample(xs2, g1, w_in_t)
    qs = qk_s[:, :H * MLSTM_DQK].reshape(Bd, H, MLSTM_DQK)
    ks = qk_s[:, H * MLSTM_DQK:].reshape(Bd, H, MLSTM_DQK)
    o = 3 * ATT_WIDTH + 2 * H * MLSTM_DQK
    vs = zs[:, o:o + MLSTM_WIDTH].reshape(Bd, H, MLSTM_DV)
    o += MLSTM_WIDTH
    oms = zs[:, o:o + MLSTM_WIDTH].reshape(Bd, H, MLSTM_DV)
    hn_s, c_s, n_s, m_s = _mlstm_sample(qs, ks, vs, oms, gates_s.reshape(Bd, 1, LANES),
                                        state_C[0], state_n[0], state_m[0].reshape(Bd, 1, H),
                                        mhg.reshape(H, MLSTM_DV))

    mix_s = jnp.concatenate([att_s.reshape(Bd, ATT_WIDTH), hn_s.reshape(Bd, MLSTM_WIDTH)], axis=-1)
    h_s = _out_proj(mix_s, wo, xs2, tm=Bd, tn=1024)
    u_s = _up_proj(h_s, g2, wu, tm=Bd, tn=1024)
    y_s = _down_proj(u_s, wd, h_s, gf, tm=Bd, tk=1024)

    k_sample = zs[:, ATT_WIDTH:2 * ATT_WIDTH].reshape(1, Bd, 1, ATT_HEADS, HEAD_DIM)
    v_sample = zs[:, 2 * ATT_WIDTH:3 * ATT_WIDTH].reshape(1, Bd, 1, ATT_HEADS, HEAD_DIM)

    return (y_p.reshape(B, T, D), y_s.reshape(Bd, 1, D),
            k_prompt, v_prompt,
            c_p[None], n_p.reshape(1, B, H, MLSTM_DQK), m_p.reshape(1, B, H),
            k_sample, v_sample,
            c_s[None], n_s[None], m_s.reshape(1, Bd, H))
```

```python
import functools

import jax
import jax.numpy as jnp
from jax import lax
from jax.experimental import pallas as pl
from jax.experimental.pallas import tpu as pltpu

F32 = jnp.float32
BF16 = jnp.bfloat16

D_MODEL = 4096
HEAD_DIM = 128
ATT_HEADS = 8
ATT_WIDTH = ATT_HEADS * HEAD_DIM
MLSTM_HEADS = 6
MLSTM_DQK = 256
MLSTM_DV = 512
MLSTM_WIDTH = MLSTM_HEADS * MLSTM_DV
N_MAIN = 3 * ATT_WIDTH + 2 * MLSTM_HEADS * MLSTM_DQK + 2 * MLSTM_WIDTH
GATE_CAP = 15.0
DILATED_CONFIGS = ((128, 1), (512, 4), (2048, 16))
WIN_MAX = 2048
ROPE_THETA = 10000.0
EPS = 1e-6
PAST_LEN = 8192
NEG_INF = float("-inf")

LANES = 128
V7X_VMEM_BYTES = 64 * 1024 * 1024
Q_TILE = 128
ACC_COLS = 512
ATT_GROUP = 4
NT_DIMS = (((1,), (1,)), ((), ()))
BF16_SUBLANES = 16
MLSTM_HEADS_PER_STEP = 2


def _vmem_limit(block_bytes, scratch_bytes, temp_bytes):
    need = 2 * sum(block_bytes) + scratch_bytes + temp_bytes
    return int(min(max(need, 16 * 1024 * 1024), V7X_VMEM_BYTES - 4 * 1024 * 1024))


def _nbytes(shape, dtype):
    n = 1
    for s in shape:
        n *= s
    return n * jnp.dtype(dtype).itemsize


def _log_sigmoid(x):
    return jnp.minimum(x, 0.0) - jnp.log1p(jnp.exp(-jnp.abs(x)))


def _sigmoid(x):
    return 1.0 / (1.0 + jnp.exp(-x))


def _rms_scale(x):
    return lax.rsqrt(jnp.mean(x * x, axis=-1, keepdims=True) + EPS)


def _norm_gate_kernel(x_ref, g_ref, wg_ref, bias_ref, xn_ref, gate_ref):
    xn = (x_ref[...] * _rms_scale(x_ref[...]) * g_ref[...]).astype(BF16)
    xn_ref[...] = xn
    pre = lax.dot_general(xn, wg_ref[...], NT_DIMS, preferred_element_type=F32) + bias_ref[...]
    gate_ref[...] = GATE_CAP * jnp.tanh(pre / GATE_CAP)


def _norm_rows(x2d, gain, w_gate, bias, *, tm):
    M, D = x2d.shape
    blocks = [_nbytes((tm, D), F32), _nbytes((tm, D), BF16), _nbytes((LANES, D), BF16), _nbytes((tm, LANES), F32)]
    return pl.pallas_call(
        _norm_gate_kernel,
        grid=(M // tm,),
        in_specs=[pl.BlockSpec((tm, D), lambda i: (i, 0)),
                  pl.BlockSpec((1, D), lambda i: (0, 0)),
                  pl.BlockSpec((LANES, D), lambda i: (0, 0)),
                  pl.BlockSpec((1, LANES), lambda i: (0, 0))],
        out_specs=[pl.BlockSpec((tm, D), lambda i: (i, 0)),
                   pl.BlockSpec((tm, LANES), lambda i: (i, 0))],
        out_shape=[jax.ShapeDtypeStruct((M, D), BF16), jax.ShapeDtypeStruct((M, LANES), F32)],
        compiler_params=pltpu.CompilerParams(
            dimension_semantics=("arbitrary",),
            vmem_limit_bytes=_vmem_limit(blocks, 0, _nbytes((tm, D), F32))),
        name="norm_gate",
    )(x2d, gain, w_gate, bias)


def _store_with_rope(acc, z_ref, cos_ref, sin_ref, j, n_rope_tiles, tn):
    @pl.when(j < n_rope_tiles)
    def _():
        c = cos_ref[...]
        s = sin_ref[...]
        for t in range(tn // HEAD_DIM):
            a = acc[:, t * HEAD_DIM:(t + 1) * HEAD_DIM]
            z_ref[:, t * HEAD_DIM:(t + 1) * HEAD_DIM] = a * c + pltpu.roll(a, HEAD_DIM // 2, 1) * s

    @pl.when(j >= n_rope_tiles)
    def _():
        z_ref[...] = acc


class _SideCast:
    def __init__(self, weights, n_steps, n_inner):
        rows = weights[0].shape[0]
        assert all(w.shape[0] == rows for w in weights)
        self.weights = weights
        self.n_inner = n_inner
        self.n_blocks = min(1 << (n_steps.bit_length() - 1), rows // BF16_SUBLANES)
        self.rows = rows // self.n_blocks

    def _index(self, a, b):
        return (jnp.minimum(a * self.n_inner + b, self.n_blocks - 1), 0)

    def specs(self):
        return [pl.BlockSpec((self.rows, w.shape[1]), self._index) for w in self.weights]

    def out_shapes(self):
        return [jax.ShapeDtypeStruct(w.shape, BF16) for w in self.weights]

    def block_bytes(self):
        return [_nbytes((self.rows, w.shape[1]), F32) + _nbytes((self.rows, w.shape[1]), BF16)
                for w in self.weights]

    def run(self, src_refs, dst_refs):
        step = pl.program_id(0) * self.n_inner + pl.program_id(1)

        @pl.when(step < self.n_blocks)
        def _():
            for src, dst in zip(src_refs, dst_refs):
                dst[...] = src[...].astype(dst.dtype)


def _inproj_prompt_kernel(xn_ref, w_ref, cos_ref, sin_ref, *refs, n_rope_tiles, tn, side):
    n_side = len(side.weights)
    side_in, (z_ref, wb_ref), side_out = refs[:n_side], refs[n_side:n_side + 2], refs[n_side + 2:]

    @pl.when(pl.program_id(1) == 0)
    def _():
        wb_ref[...] = w_ref[...].astype(BF16)

    acc = lax.dot_general(xn_ref[...], wb_ref[...], NT_DIMS, preferred_element_type=F32)
    _store_with_rope(acc, z_ref, cos_ref, sin_ref, pl.program_id(0), n_rope_tiles, tn)
    side.run(side_in, side_out)


def _in_proj_prompt(xn, w_t, cosf, sinf, side_weights, *, tm, tn, pos_tiles):
    M, D = xn.shape
    grid = (N_MAIN // tn, M // tm)
    side = _SideCast(side_weights, grid[0] * grid[1], grid[1])
    blocks = [_nbytes((tm, D), BF16), _nbytes((tn, D), F32), 2 * _nbytes((tm, LANES), F32),
              _nbytes((tm, tn), F32), _nbytes((tn, D), BF16)] + side.block_bytes()
    return pl.pallas_call(
        functools.partial(_inproj_prompt_kernel, n_rope_tiles=2 * ATT_WIDTH // tn, tn=tn, side=side),
        grid=grid,
        in_specs=[
            pl.BlockSpec((tm, D), lambda j, i: (i, 0)),
            pl.BlockSpec((tn, D), lambda j, i: (j, 0)),
            pl.BlockSpec((tm, LANES), lambda j, i: (i % pos_tiles, 0)),
            pl.BlockSpec((tm, LANES), lambda j, i: (i % pos_tiles, 0)),
        ] + side.specs(),
        out_specs=[pl.BlockSpec((tm, tn), lambda j, i: (i, j)),
                   pl.BlockSpec((tn, D), lambda j, i: (j, 0))] + side.specs(),
        out_shape=[jax.ShapeDtypeStruct((M, N_MAIN), F32),
                   jax.ShapeDtypeStruct((N_MAIN, D), BF16)] + side.out_shapes(),
        compiler_params=pltpu.CompilerParams(
            dimension_semantics=("arbitrary", "arbitrary"),
            vmem_limit_bytes=_vmem_limit(blocks, 0, 2 * _nbytes((tm, tn), F32))),
        name="in_proj_prompt",
    )(xn, w_t, cosf, sinf, *side_weights)


def _upproj_prompt_kernel(hg_ref, ssq_ref, w_ref, *refs, side):
    n_side = len(side.weights)
    side_in, u_ref, side_out = refs[:n_side], refs[n_side], refs[n_side + 1:]
    r = lax.rsqrt(ssq_ref[:, 0:1] * (1.0 / hg_ref.shape[1]) + EPS)
    acc = jnp.dot(hg_ref[...], w_ref[...], preferred_element_type=F32) * r
    u_ref[...] = jnp.square(jnp.maximum(acc, 0.0)).astype(u_ref.dtype)
    side.run(side_in, side_out)


def _up_proj_prompt(hg, ssq, w_up, side_weights, *, tm, tn):
    M, D = hg.shape
    N = w_up.shape[1]
    grid = (M // tm, N // tn)
    side = _SideCast(side_weights, grid[0] * grid[1], grid[1])
    blocks = [_nbytes((tm, D), BF16), _nbytes((tm, LANES), F32), _nbytes((D, tn), BF16),
              _nbytes((tm, tn), BF16)] + side.block_bytes()
    return pl.pallas_call(
        functools.partial(_upproj_prompt_kernel, side=side),
        grid=grid,
        in_specs=[pl.BlockSpec((tm, D), lambda i, j: (i, 0)),
                  pl.BlockSpec((tm, LANES), lambda i, j: (i, 0)),
                  pl.BlockSpec((D, tn), lambda i, j: (0, j))] + side.specs(),
        out_specs=[pl.BlockSpec((tm, tn), lambda i, j: (i, j))] + side.specs(),
        out_shape=[jax.ShapeDtypeStruct((M, N), BF16)] + side.out_shapes(),
        compiler_params=pltpu.CompilerParams(
            dimension_semantics=("arbitrary", "arbitrary"),
            vmem_limit_bytes=_vmem_limit(blocks, 0, 3 * _nbytes((tm, tn), F32))),
        name="up_proj_prompt",
    )(hg, ssq, w_up, *side_weights)


def _outproj_prompt_kernel(att_ref, hn_ref, w_ref, x_ref, g_ref, h_ref, hg_ref, ssq_ref):
    @pl.when(pl.program_id(1) == 0)
    def _():
        ssq_ref[...] = jnp.zeros_like(ssq_ref)

    att = att_ref[...]
    hn = hn_ref[...]
    ssq = jnp.zeros((h_ref.shape[0], 1), F32)
    for n in range(0, h_ref.shape[1], ACC_COLS):
        cols = slice(n, n + ACC_COLS)
        h = (x_ref[:, cols]
             + jnp.dot(att, w_ref[:ATT_WIDTH, cols], preferred_element_type=F32)
             + jnp.dot(hn, w_ref[ATT_WIDTH:, cols], preferred_element_type=F32))
        h_ref[:, cols] = h
        hg_ref[:, cols] = (h * g_ref[:, cols]).astype(hg_ref.dtype)
        ssq = ssq + jnp.sum(h * h, axis=1, keepdims=True)
    ssq_ref[...] += ssq


def _out_proj_prompt(att2d, hn2d, w, x2d, gain, *, tm, tn):
    M, D = x2d.shape
    K = w.shape[0]
    blocks = [_nbytes((tm, K), BF16), _nbytes((K, tn), BF16), 2 * _nbytes((tm, tn), F32),
              _nbytes((tm, tn), BF16), _nbytes((tm, LANES), F32)]
    return pl.pallas_call(
        _outproj_prompt_kernel,
        grid=(M // tm, D // tn),
        in_specs=[pl.BlockSpec((tm, ATT_WIDTH), lambda i, j: (i, 0), pipeline_mode=pl.Buffered(1)),
                  pl.BlockSpec((tm, MLSTM_WIDTH), lambda i, j: (i, 0), pipeline_mode=pl.Buffered(1)),
                  pl.BlockSpec((K, tn), lambda i, j: (0, j)),
                  pl.BlockSpec((tm, tn), lambda i, j: (i, j)),
                  pl.BlockSpec((1, tn), lambda i, j: (0, j))],
        out_specs=[pl.BlockSpec((tm, tn), lambda i, j: (i, j)),
                   pl.BlockSpec((tm, tn), lambda i, j: (i, j)),
                   pl.BlockSpec((tm, LANES), lambda i, j: (i, 0))],
        out_shape=[jax.ShapeDtypeStruct((M, D), F32), jax.ShapeDtypeStruct((M, D), BF16),
                   jax.ShapeDtypeStruct((M, LANES), F32)],
        compiler_params=pltpu.CompilerParams(
            dimension_semantics=("arbitrary", "arbitrary"),
            vmem_limit_bytes=_vmem_limit(blocks, 0, 6 * _nbytes((tm, ACC_COLS), F32))),
        name="out_proj_prompt",
    )(att2d, hn2d, w, x2d, gain)


def _inproj_kernel(x_ref, g_ref, w_ref, wg_ref, bias_ref, cos_ref, sin_ref,
                   z_ref, gate_ref, xn_ref, *, n_rope_tiles, tn):
    j = pl.program_id(1)

    @pl.when(j == 0)
    def _():
        x = x_ref[...]
        xn = (x * _rms_scale(x) * g_ref[...]).astype(BF16)
        xn_ref[...] = xn
        pre = lax.dot_general(xn, wg_ref[...], NT_DIMS, preferred_element_type=F32) + bias_ref[...]
        gate_ref[...] = GATE_CAP * jnp.tanh(pre / GATE_CAP)

    acc = lax.dot_general(xn_ref[...], w_ref[...], NT_DIMS, preferred_element_type=F32)
    _store_with_rope(acc, z_ref, cos_ref, sin_ref, j, n_rope_tiles, tn)


def _in_proj(x2d, gain, w_main, w_gate, bias, cosf, sinf, *, tm, tn, pos_tiles):
    M, D = x2d.shape
    N = w_main.shape[0]
    blocks = [_nbytes((tm, D), F32), _nbytes((tn, D), BF16), _nbytes((LANES, D), BF16),
              2 * _nbytes((tm, LANES), F32), _nbytes((tm, tn), F32), _nbytes((tm, LANES), F32)]
    return pl.pallas_call(
        functools.partial(_inproj_kernel, n_rope_tiles=2 * ATT_WIDTH // tn, tn=tn),
        grid=(M // tm, N // tn),
        in_specs=[
            pl.BlockSpec((tm, D), lambda i, j: (i, 0)),
            pl.BlockSpec((1, D), lambda i, j: (0, 0)),
            pl.BlockSpec((tn, D), lambda i, j: (j, 0)),
            pl.BlockSpec((LANES, D), lambda i, j: (0, 0)),
            pl.BlockSpec((1, LANES), lambda i, j: (0, 0)),
            pl.BlockSpec((tm, LANES), lambda i, j: (i % pos_tiles, 0)),
            pl.BlockSpec((tm, LANES), lambda i, j: (i % pos_tiles, 0)),
        ],
        out_specs=[pl.BlockSpec((tm, tn), lambda i, j: (i, j)),
                   pl.BlockSpec((tm, LANES), lambda i, j: (i, 0))],
        out_shape=[jax.ShapeDtypeStruct((M, N), F32), jax.ShapeDtypeStruct((M, LANES), F32)],
        scratch_shapes=[pltpu.VMEM((tm, D), BF16)],
        compiler_params=pltpu.CompilerParams(
            dimension_semantics=("arbitrary", "arbitrary"),
            vmem_limit_bytes=_vmem_limit(blocks, _nbytes((tm, D), BF16),
                                         _nbytes((tm, D), F32) + 2 * _nbytes((tm, tn), F32))),
        name="in_proj",
    )(x2d, gain, w_main, w_gate, bias, cosf, sinf)


def _attn_prompt_kernel(q_ref, k_ref, v_ref, o_ref, m_s, l_s, acc_s, *, T):
    scale = HEAD_DIM ** -0.5
    G = ATT_GROUP
    row = lax.broadcasted_iota(jnp.int32, (G * Q_TILE, 2 * Q_TILE), 0) & (Q_TILE - 1)
    col = lax.broadcasted_iota(jnp.int32, (G * Q_TILE, 2 * Q_TILE), 1)
    band = jnp.logical_and(col >= row, col - Q_TILE <= row)
    prev_cols = lax.broadcasted_iota(jnp.int32, (Q_TILE, 2 * Q_TILE), 1) < Q_TILE
    nt = (((1,), (1,)), ((), ()))

    for ci, (_, r) in enumerate(DILATED_CONFIGS):
        tiles_per_class = T // (r * Q_TILE)
        run = min(G, tiles_per_class)
        runs_per_group = G // run
        runs_per_class = tiles_per_class // run
        shift = runs_per_class.bit_length() - 1
        n_groups = T // (Q_TILE * G)

        def rows(ref, start, n, r=r):
            if r == 1:
                return ref[pl.ds(start, n), :]
            return ref[pl.ds(start, n, stride=r), :]

        def put(ref, start, n, val, r=r):
            if r == 1:
                ref[pl.ds(start, n), :] = val
            else:
                ref[pl.ds(start, n, stride=r), :] = val

        def body(gi, carry, r=r, ci=ci, run=run, rpg=runs_per_group, rpc=runs_per_class,
                 shift=shift, rows=rows, put=put):
            n = run * Q_TILE
            starts, vxs, scores = [], [], []
            for j in range(rpg):
                ridx = gi * rpg + j
                cls = lax.shift_right_logical(ridx, shift)
                u0 = (ridx & (rpc - 1)) * n
                start = cls + r * u0
                pstart = cls + r * jnp.maximum(u0 - Q_TILE, 0)
                no_prev = u0 == 0
                q = (rows(q_ref, start, n) * scale).astype(BF16)
                kx = jnp.concatenate([rows(k_ref, pstart, Q_TILE), rows(k_ref, start, n)], axis=0).astype(BF16)
                vx = jnp.concatenate([rows(v_ref, pstart, Q_TILE), rows(v_ref, start, n)], axis=0).astype(BF16)
                for g in range(run):
                    s = lax.dot_general(q[g * Q_TILE:(g + 1) * Q_TILE], kx[g * Q_TILE:(g + 2) * Q_TILE], nt,
                                        preferred_element_type=F32)
                    if g == 0:
                        s = jnp.where(jnp.logical_and(prev_cols, no_prev), NEG_INF, s)
                    scores.append(s)
                starts.append(start)
                vxs.append(vx)
            s_all = jnp.where(band, jnp.concatenate(scores, axis=0), NEG_INF)
            mt = jnp.max(s_all, axis=1, keepdims=True)
            if ci == 0:
                m_new = jnp.broadcast_to(mt, (G * Q_TILE, LANES))
            else:
                m_old = jnp.concatenate([rows(m_s, st, n) for st in starts], axis=0)
                m_new = jnp.maximum(m_old, mt)
                alpha = jnp.exp(m_old - m_new)
            p = jnp.exp(s_all - jnp.concatenate([m_new, m_new], axis=1))
            lt = jnp.sum(p, axis=1, keepdims=True)
            pb = p.astype(BF16)
            outs = []
            for j in range(rpg):
                for g in range(run):
                    t = j * run + g
                    outs.append(jnp.dot(pb[t * Q_TILE:(t + 1) * Q_TILE], vxs[j][g * Q_TILE:(g + 2) * Q_TILE],
                                        preferred_element_type=F32))
            ot = jnp.concatenate(outs, axis=0)
            if ci == 0:
                l_new = jnp.broadcast_to(lt, (G * Q_TILE, LANES))
            else:
                l_new = alpha * jnp.concatenate([rows(l_s, st, n) for st in starts], axis=0) + lt
                ot = alpha * jnp.concatenate([rows(acc_s, st, n) for st in starts], axis=0) + ot
            for j, st in enumerate(starts):
                put(m_s, st, n, m_new[j * n:(j + 1) * n])
                put(l_s, st, n, l_new[j * n:(j + 1) * n])
                put(acc_s, st, n, ot[j * n:(j + 1) * n])
            return carry

        lax.fori_loop(0, n_groups, body, 0)

    o_ref[...] = (acc_s[...] / l_s[...]).astype(o_ref.dtype)


def _attn_prompt(z3):
    B, T, _ = z3.shape
    assert T % (DILATED_CONFIGS[-1][1] * Q_TILE) == 0
    blk = (None, T, HEAD_DIM)
    blocks = [3 * _nbytes((T, HEAD_DIM), F32), _nbytes((T, HEAD_DIM), BF16)]
    return pl.pallas_call(
        functools.partial(_attn_prompt_kernel, T=T),
        grid=(B, ATT_HEADS),
        in_specs=[pl.BlockSpec(blk, lambda b, h: (b, 0, h)),
                  pl.BlockSpec(blk, lambda b, h: (b, 0, ATT_HEADS + h)),
                  pl.BlockSpec(blk, lambda b, h: (b, 0, 2 * ATT_HEADS + h))],
        out_specs=pl.BlockSpec(blk, lambda b, h: (b, 0, h)),
        out_shape=jax.ShapeDtypeStruct((B, T, ATT_WIDTH), BF16),
        scratch_shapes=[pltpu.VMEM((T, LANES), F32)] * 3,
        compiler_params=pltpu.CompilerParams(
            dimension_semantics=("arbitrary", "arbitrary"),
            vmem_limit_bytes=_vmem_limit(blocks, 3 * _nbytes((T, LANES), F32),
                                         2 * _nbytes((T, LANES), F32))),
        name="attn_prompt",
    )(z3, z3, z3)


def _mlstm_prompt_kernel(q_ref, k_ref, v_ref, om_ref, lir_ref, fgr_ref, lic_ref, fgc_ref, mhg_ref,
                         hn_ref, c_out, n_out, m_out, c_s, n_s, m_s, *, L, heads):
    c = pl.program_id(2)

    @pl.when(c == 0)
    def _():
        c_s[...] = jnp.zeros_like(c_s)
        n_s[...] = jnp.zeros_like(n_s)
        m_s[...] = jnp.zeros_like(m_s)

    r_idx = lax.broadcasted_iota(jnp.int32, (L, L), 0)
    c_idx = lax.broadcasted_iota(jnp.int32, (L, L), 1)
    causal = c_idx <= r_idx

    for hh in range(heads):
        qs = slice(hh * MLSTM_DQK, (hh + 1) * MLSTM_DQK)
        vs = slice(hh * MLSTM_DV, (hh + 1) * MLSTM_DV)
        li_row = lir_ref[hh]
        li_col = lic_ref[hh]
        lf_row = _log_sigmoid(fgr_ref[hh])
        lf_col = _log_sigmoid(fgc_ref[hh])
        b_col = jnp.sum(jnp.where(causal, lf_row, 0.0), axis=1, keepdims=True)
        b_row = jnp.sum(jnp.where(r_idx <= c_idx, lf_col, 0.0), axis=0, keepdims=True)

        m_prev = m_s[hh]
        dlog = jnp.where(causal, b_col - b_row + li_row, NEG_INF)
        inter = b_col + m_prev
        m_t = jnp.maximum(inter, jnp.max(dlog, axis=1, keepdims=True))
        dmat = jnp.exp(dlog - m_t)
        a = jnp.exp(inter - m_t)

        qf = q_ref[:, qs] * (MLSTM_DQK ** -0.5)
        qb = qf.astype(BF16)
        kf = k_ref[:, qs]
        kb = kf.astype(BF16)
        vb = v_ref[:, vs].astype(BF16)
        s = lax.dot_general(qb, kb, NT_DIMS, preferred_element_type=F32)
        sw = s * dmat
        c_prev = c_s[hh]
        num = (jnp.dot(sw.astype(BF16), vb, preferred_element_type=F32)
               + a * jnp.dot(qb, c_prev.astype(BF16), preferred_element_type=F32))
        den = (jnp.sum(sw, axis=1, keepdims=True)
               + a * jnp.sum(qf * n_s[hh], axis=1, keepdims=True))
        h = num / jnp.maximum(jnp.abs(den), jnp.exp(-m_t))

        m_new = m_t[L - 1:L, :]
        b_last = b_col[L - 1:L, :]
        ws = jnp.exp(b_last - b_col + li_col - m_new)
        decay = jnp.exp(b_last + m_prev - m_new)
        kw = kf * ws
        kwt = kw.T.astype(BF16)
        c_s[hh] = decay * c_prev + jnp.dot(kwt, vb, preferred_element_type=F32)
        n_s[hh] = decay * n_s[hh] + jnp.sum(kw, axis=0, keepdims=True)
        m_s[hh] = m_new

        hn = h * _rms_scale(h) * mhg_ref[:, vs] * _sigmoid(om_ref[:, vs])
        hn_ref[:, vs] = hn.astype(hn_ref.dtype)

    @pl.when(c == pl.num_programs(2) - 1)
    def _():
        c_out[...] = c_s[...]
        n_out[...] = n_s[...]
        m_out[...] = m_s[...]


def _mlstm_prompt(z3, li_row, fg_row, li_col, fg_col, mhg, *, L, heads):
    B, T, _ = z3.shape
    H = MLSTM_HEADS
    assert H % heads == 0
    wq, wv = heads * MLSTM_DQK, heads * MLSTM_DV
    q_off = 3 * ATT_WIDTH // wq
    k_off = q_off + H // heads
    v_off = (3 * ATT_WIDTH + 2 * H * MLSTM_DQK) // wv
    o_off = v_off + H // heads
    blocks = [2 * _nbytes((L, wq), F32), 2 * _nbytes((L, wv), F32),
              4 * heads * _nbytes((L, LANES), F32), _nbytes((L, wv), BF16),
              heads * _nbytes((MLSTM_DQK, MLSTM_DV), F32)]
    return pl.pallas_call(
        functools.partial(_mlstm_prompt_kernel, L=L, heads=heads),
        grid=(B, H // heads, T // L),
        in_specs=[
            pl.BlockSpec((None, L, wq), lambda b, h, c: (b, c, q_off + h)),
            pl.BlockSpec((None, L, wq), lambda b, h, c: (b, c, k_off + h)),
            pl.BlockSpec((None, L, wv), lambda b, h, c: (b, c, v_off + h)),
            pl.BlockSpec((None, L, wv), lambda b, h, c: (b, c, o_off + h)),
            pl.BlockSpec((None, heads, 1, L), lambda b, h, c: (b, h, 0, c)),
            pl.BlockSpec((None, heads, 1, L), lambda b, h, c: (b, h, 0, c)),
            pl.BlockSpec((None, heads, L, 1), lambda b, h, c: (b, h, c, 0)),
            pl.BlockSpec((None, heads, L, 1), lambda b, h, c: (b, h, c, 0)),
            pl.BlockSpec((1, wv), lambda b, h, c: (0, h)),
        ],
        out_specs=[
            pl.BlockSpec((None, L, wv), lambda b, h, c: (b, c, h)),
            pl.BlockSpec((None, heads, MLSTM_DQK, MLSTM_DV), lambda b, h, c: (b, h, 0, 0)),
            pl.BlockSpec((None, heads, 1, MLSTM_DQK), lambda b, h, c: (b, h, 0, 0)),
            pl.BlockSpec((None, heads, 1, 1), lambda b, h, c: (b, h, 0, 0)),
        ],
        out_shape=[
            jax.ShapeDtypeStruct((B, T, MLSTM_WIDTH), BF16),
            jax.ShapeDtypeStruct((B, H, MLSTM_DQK, MLSTM_DV), F32),
            jax.ShapeDtypeStruct((B, H, 1, MLSTM_DQK), F32),
            jax.ShapeDtypeStruct((B, H, 1, 1), F32),
        ],
        scratch_shapes=[pltpu.VMEM((heads, MLSTM_DQK, MLSTM_DV), F32),
                        pltpu.VMEM((heads, 1, MLSTM_DQK), F32),
                        pltpu.VMEM((heads, 1, 1), F32)],
        compiler_params=pltpu.CompilerParams(
            dimension_semantics=("arbitrary", "arbitrary", "arbitrary"),
            vmem_limit_bytes=_vmem_limit(blocks, heads * _nbytes((MLSTM_DQK, MLSTM_DV), F32),
                                         heads * (8 * _nbytes((L, MLSTM_DV), F32) + 8 * _nbytes((L, L), F32)))),
        name="mlstm_prompt",
    )(z3, z3, z3, z3, li_row, fg_row, li_col, fg_col, mhg)


def _outproj_kernel(mix_ref, w_ref, x_ref, h_ref):
    h_ref[...] = x_ref[...] + jnp.dot(mix_ref[...], w_ref[...], preferred_element_type=F32)


def _out_proj(mix2d, w, x2d, *, tm, tn):
    M, D = x2d.shape
    K = w.shape[0]
    blocks = [_nbytes((tm, K), BF16), _nbytes((K, tn), BF16), 2 * _nbytes((tm, tn), F32)]
    return pl.pallas_call(
        _outproj_kernel,
        grid=(M // tm, D // tn),
        in_specs=[
            pl.BlockSpec((tm, K), lambda i, j: (i, 0)),
            pl.BlockSpec((K, tn), lambda i, j: (0, j)),
            pl.BlockSpec((tm, tn), lambda i, j: (i, j)),
        ],
        out_specs=pl.BlockSpec((tm, tn), lambda i, j: (i, j)),
        out_shape=jax.ShapeDtypeStruct((M, D), F32),
        compiler_params=pltpu.CompilerParams(
            dimension_semantics=("arbitrary", "arbitrary"),
            vmem_limit_bytes=_vmem_limit(blocks, 0, 2 * _nbytes((tm, tn), F32))),
        name="out_proj",
    )(mix2d, w, x2d)


def _upproj_kernel(x_ref, g_ref, w_ref, u_ref, xn_ref):
    @pl.when(pl.program_id(1) == 0)
    def _():
        x = x_ref[...]
        xn_ref[...] = (x * _rms_scale(x) * g_ref[...]).astype(BF16)

    acc = jnp.dot(xn_ref[...], w_ref[...], preferred_element_type=F32)
    u_ref[...] = jnp.square(jnp.maximum(acc, 0.0)).astype(u_ref.dtype)


def _up_proj(h2d, gain, w_up, *, tm, tn):
    M, D = h2d.shape
    N = w_up.shape[1]
    blocks = [_nbytes((tm, D), F32), _nbytes((D, tn), BF16), _nbytes((tm, tn), BF16)]
    return pl.pallas_call(
        _upproj_kernel,
        grid=(M // tm, N // tn),
        in_specs=[pl.BlockSpec((tm, D), lambda i, j: (i, 0)),
                  pl.BlockSpec((1, D), lambda i, j: (0, 0)),
                  pl.BlockSpec((D, tn), lambda i, j: (0, j))],
        out_specs=pl.BlockSpec((tm, tn), lambda i, j: (i, j)),
        out_shape=jax.ShapeDtypeStruct((M, N), BF16),
        scratch_shapes=[pltpu.VMEM((tm, D), BF16)],
        compiler_params=pltpu.CompilerParams(
            dimension_semantics=("arbitrary", "arbitrary"),
            vmem_limit_bytes=_vmem_limit(blocks, _nbytes((tm, D), BF16),
                                         _nbytes((tm, D), F32) + 2 * _nbytes((tm, tn), F32))),
        name="up_proj",
    )(h2d, gain, w_up)


def _downproj_kernel(u_ref, w_ref, h_ref, g_ref, y_ref):
    k = pl.program_id(1)

    @pl.when(k == 0)
    def _():
        y_ref[...] = h_ref[...]

    u = u_ref[...]
    for n in range(0, y_ref.shape[1], ACC_COLS):
        y_ref[:, n:n + ACC_COLS] += jnp.dot(u, w_ref[:, n:n + ACC_COLS], preferred_element_type=F32)

    @pl.when(k == pl.num_programs(1) - 1)
    def _():
        scale = _rms_scale(y_ref[...])
        y_ref[...] = y_ref[...] * scale * g_ref[...]


def _down_proj(u2d, w_down, h2d, gain, *, tm, tk):
    M, D = h2d.shape
    K = u2d.shape[1]
    blocks = [_nbytes((tm, tk), BF16), _nbytes((tk, D), BF16), 2 * _nbytes((tm, D), F32)]
    return pl.pallas_call(
        _downproj_kernel,
        grid=(M // tm, K // tk),
        in_specs=[pl.BlockSpec((tm, tk), lambda i, k: (i, k)),
                  pl.BlockSpec((tk, D), lambda i, k: (k, 0)),
                  pl.BlockSpec((tm, D), lambda i, k: (i, 0)),
                  pl.BlockSpec((1, D), lambda i, k: (0, 0))],
        out_specs=pl.BlockSpec((tm, D), lambda i, k: (i, 0)),
        out_shape=jax.ShapeDtypeStruct((M, D), F32),
        compiler_params=pltpu.CompilerParams(
            dimension_semantics=("arbitrary", "arbitrary"),
            vmem_limit_bytes=_vmem_limit(blocks, 0, _nbytes((tm, D), F32) + _nbytes((tm, ACC_COLS), F32))),
        name="down_proj",
    )(u2d, w_down, h2d, gain)


def _attn_sample_kernel(q_ref, kn_ref, vn_ref, k1_ref, k4_ref, k16_ref, v1_ref, v4_ref, v16_ref, o_ref):
    q = q_ref[...] * (HEAD_DIM ** -0.5)
    kn = kn_ref[...]
    vn = vn_ref[...]
    s0 = jnp.sum(q * kn, axis=-1, keepdims=True)

    ms, dens, nums = [], [], []
    for kt_ref, vt_ref in ((k1_ref, v1_ref), (k4_ref, v4_ref), (k16_ref, v16_ref)):
        kt = kt_ref[...]
        vt = vt_ref[...]
        s = jnp.sum(q[None] * kt, axis=-1, keepdims=True)
        m = jnp.maximum(jnp.max(s, axis=0), s0)
        p = jnp.exp(s - m[None])
        p0 = jnp.exp(s0 - m)
        dens.append(jnp.sum(p, axis=0) + p0)
        nums.append(jnp.sum(p * vt, axis=0) + p0 * vn)
        ms.append(m)
    m_all = jnp.maximum(jnp.maximum(ms[0], ms[1]), ms[2])
    num = jnp.zeros((ATT_HEADS, HEAD_DIM), F32)
    den = jnp.zeros((ATT_HEADS, 1), F32)
    for m, d, n in zip(ms, dens, nums):
        wgt = jnp.exp(m - m_all)
        num = num + wgt * n
        den = den + wgt * d
    o_ref[...] = (num / den).astype(o_ref.dtype)


def _attn_sample(qs, kns, vns, cache_k, cache_v):
    Bd, W, H, hd = cache_k.shape
    new_spec = pl.BlockSpec((None, H, hd), lambda b: (b, 0, 0))
    in_specs = [new_spec, new_spec, new_spec]
    views_k, views_v = [], []
    for (w, r) in DILATED_CONFIGS:
        assert w // r == Q_TILE and W % w == 0 and PAST_LEN >= W >= w
        views_k.append(cache_k.reshape(Bd, W // r, r, H, hd))
        views_v.append(cache_v.reshape(Bd, W // r, r, H, hd))
    for _ in range(2):
        for (w, r) in DILATED_CONFIGS:
            last = W // w - 1
            in_specs.append(pl.BlockSpec((None, Q_TILE, None, H, hd),
                                         lambda b, last=last: (b, last, 0, 0, 0)))
    blocks = [3 * _nbytes((H, hd), F32), 6 * _nbytes((Q_TILE, H, hd), F32)]
    return pl.pallas_call(
        _attn_sample_kernel,
        grid=(Bd,),
        in_specs=in_specs,
        out_specs=pl.BlockSpec((None, H, hd), lambda b: (b, 0, 0)),
        out_shape=jax.ShapeDtypeStruct((Bd, H, hd), BF16),
        compiler_params=pltpu.CompilerParams(
            dimension_semantics=("arbitrary",),
            vmem_limit_bytes=_vmem_limit(blocks, 0, 8 * _nbytes((Q_TILE, H, hd), F32))),
        name="attn_sample",
    )(qs, kns, vns, *views_k, *views_v)


def _qk_sample_kernel(x_ref, g_ref, w_ref, o_ref, xn_ref):
    @pl.when(pl.program_id(0) == 0)
    def _():
        x = x_ref[...]
        xn_ref[...] = x * _rms_scale(x) * g_ref[...]

    o_ref[...] = lax.dot_general(xn_ref[...], w_ref[...], NT_DIMS, precision=lax.Precision.HIGHEST,
                                 preferred_element_type=F32)


def _qk_sample(x2d, gain, w_t, *, tn=512):
    Bd, D = x2d.shape
    n_out = 2 * MLSTM_HEADS * MLSTM_DQK
    first = 3 * ATT_WIDTH // tn
    blocks = [_nbytes((Bd, D), F32), _nbytes((tn, D), F32), _nbytes((Bd, tn), F32)]
    return pl.pallas_call(
        _qk_sample_kernel,
        grid=(n_out // tn,),
        in_specs=[pl.BlockSpec((Bd, D), lambda j: (0, 0)),
                  pl.BlockSpec((1, D), lambda j: (0, 0)),
                  pl.BlockSpec((tn, D), lambda j: (first + j, 0))],
        out_specs=pl.BlockSpec((Bd, tn), lambda j: (0, j)),
        out_shape=jax.ShapeDtypeStruct((Bd, n_out), F32),
        scratch_shapes=[pltpu.VMEM((Bd, D), F32)],
        compiler_params=pltpu.CompilerParams(
            dimension_semantics=("arbitrary",),
            vmem_limit_bytes=_vmem_limit(blocks, _nbytes((Bd, D), F32), 4 * _nbytes((tn, D), BF16))),
        name="qk_sample",
    )(x2d, gain, w_t)


def _mlstm_sample_kernel(q_ref, k_ref, v_ref, om_ref, g_ref, c0_ref, n0_ref, m0_ref, mhg_ref,
                         hn_ref, c_out, n_out, m_out):
    H = MLSTM_HEADS
    for h in range(H):
        qf = q_ref[h:h + 1, :] * (MLSTM_DQK ** -0.5)
        kf = k_ref[h:h + 1, :]
        vf = v_ref[h:h + 1, :]
        li = g_ref[:, h:h + 1]
        lf = _log_sigmoid(g_ref[:, H + h:H + h + 1])
        m0 = m0_ref[:, h:h + 1]
        c0 = c0_ref[h]
        n0 = n0_ref[h:h + 1, :]

        inter = lf + m0
        m_t = jnp.maximum(inter, li)
        qk = jnp.sum(qf * kf, axis=1, keepdims=True)
        sw = qk * jnp.exp(li - m_t)
        a = jnp.exp(inter - m_t)
        qc = jnp.dot(jnp.broadcast_to(qf, (8, MLSTM_DQK)), c0, precision=lax.Precision.HIGHEST,
                     preferred_element_type=F32)[0:1, :]
        num = sw * vf + a * qc
        den = sw + a * jnp.sum(qf * n0, axis=1, keepdims=True)
        hh = num / jnp.maximum(jnp.abs(den), jnp.exp(-m_t))

        ws = jnp.exp(li - m_t)
        decay = jnp.exp(inter - m_t)
        kw = kf * ws
        kcol = jnp.broadcast_to(kw, (LANES, MLSTM_DQK)).T[:, 0:1]
        c_out[h] = decay * c0 + kcol * vf
        n_out[h:h + 1, :] = decay * n0 + kw
        m_out[:, h:h + 1] = m_t

        hn = hh * _rms_scale(hh) * mhg_ref[h:h + 1, :] * _sigmoid(om_ref[h:h + 1, :])
        hn_ref[h:h + 1, :] = hn.astype(hn_ref.dtype)


def _mlstm_sample(qs, ks, vs, oms, gs, c0, n0, m0, mhg):
    Bd = qs.shape[0]
    H = MLSTM_HEADS
    blocks = [2 * _nbytes((8, MLSTM_DQK), F32), 2 * _nbytes((8, MLSTM_DV), F32),
              2 * _nbytes((H, MLSTM_DQK, MLSTM_DV), F32), 4 * _nbytes((8, MLSTM_DV), F32)]
    return pl.pallas_call(
        _mlstm_sample_kernel,
        grid=(Bd,),
        in_specs=[
            pl.BlockSpec((None, H, MLSTM_DQK), lambda b: (b, 0, 0)),
            pl.BlockSpec((None, H, MLSTM_DQK), lambda b: (b, 0, 0)),
            pl.BlockSpec((None, H, MLSTM_DV), lambda b: (b, 0, 0)),
            pl.BlockSpec((None, H, MLSTM_DV), lambda b: (b, 0, 0)),
            pl.BlockSpec((None, 1, LANES), lambda b: (b, 0, 0)),
            pl.BlockSpec((None, H, MLSTM_DQK, MLSTM_DV), lambda b: (b, 0, 0, 0)),
            pl.BlockSpec((None, H, MLSTM_DQK), lambda b: (b, 0, 0)),
            pl.BlockSpec((None, 1, H), lambda b: (b, 0, 0)),
            pl.BlockSpec((H, MLSTM_DV), lambda b: (0, 0)),
        ],
        out_specs=[
            pl.BlockSpec((None, H, MLSTM_DV), lambda b: (b, 0, 0)),
            pl.BlockSpec((None, H, MLSTM_DQK, MLSTM_DV), lambda b: (b, 0, 0, 0)),
            pl.BlockSpec((None, H, MLSTM_DQK), lambda b: (b, 0, 0)),
            pl.BlockSpec((None, 1, H), lambda b: (b, 0, 0)),
        ],
        out_shape=[
            jax.ShapeDtypeStruct((Bd, H, MLSTM_DV), BF16),
            jax.ShapeDtypeStruct((Bd, H, MLSTM_DQK, MLSTM_DV), F32),
            jax.ShapeDtypeStruct((Bd, H, MLSTM_DQK), F32),
            jax.ShapeDtypeStruct((Bd, 1, H), F32),
        ],
        compiler_params=pltpu.CompilerParams(
            dimension_semantics=("arbitrary",),
            vmem_limit_bytes=_vmem_limit(blocks, 0, 4 * _nbytes((MLSTM_DQK, MLSTM_DV), F32))),
        name="mlstm_sample",
    )(qs, ks, vs, oms, gs, c0, n0, m0, mhg)


def _rope_tables(pos):
    half = HEAD_DIM // 2
    inv = ROPE_THETA ** (-jnp.arange(half, dtype=F32) / half)
    ang = pos.astype(F32)[:, None] * inv[None, :]
    cos = jnp.cos(ang)
    sin = jnp.sin(ang)
    return jnp.concatenate([cos, cos], axis=-1), jnp.concatenate([-sin, sin], axis=-1)


def _row_tile(M, cap):
    tm = min(M, cap)
    assert M % tm == 0
    return tm


def kernel(x_prompt, x_sample, cache_k, cache_v, state_C, state_n, state_m,
           norm1_g, w_in, b_if, mh_norm_g, w_out, norm2_g, w_up, w_down, final_g):
    B, T, D = x_prompt.shape
    Bd, Td, _ = x_sample.shape
    depth = w_in.shape[0]
    assert depth == 1 and Td == 1 and D == D_MODEL
    keep = min(WIN_MAX, T)
    H = MLSTM_HEADS

    w_in_t = jnp.transpose(w_in[0])
    w_gate = jnp.pad(w_in_t[N_MAIN:], ((0, LANES - 2 * H), (0, 0))).astype(BF16)
    bias = jnp.pad(b_if[0], (0, LANES - 2 * H)).reshape(1, LANES)
    g1 =norm1_g[0].reshape(1, D)
    g2 = norm2_g[0].reshape(1, D)
    gf = final_g.reshape(1, D)
    mhg = mh_norm_g[0]

    cos_p, sin_p = _rope_tables(jnp.arange(T, dtype=jnp.int32))
    cos_s, sin_s = _rope_tables(jnp.full((Bd,), PAST_LEN, dtype=jnp.int32))

    xp2 = x_prompt.reshape(B * T, D)
    tm = _row_tile(T, 512)
    tm_w = _row_tile(T, 1024)
    xn_p, gates = _norm_rows(xp2, g1, w_gate, bias, tm=tm)
    z, w_main, wo, wu = _in_proj_prompt(xn_p, w_in_t, cos_p, sin_p, [w_out[0], w_up[0]],
                                        tm=tm_w, tn=512, pos_tiles=T // tm_w)
    z3 = z.reshape(B, T, N_MAIN)
    att = _attn_prompt(z3)

    g3 = gates.reshape(B, T, LANES)
    li_row = jnp.swapaxes(g3[:, :, :H], 1, 2).reshape(B, H, 1, T)
    fg_row = jnp.swapaxes(g3[:, :, H:2 * H], 1, 2).reshape(B, H, 1, T)
    li_col = li_row.reshape(B, H, T, 1)
    fg_col = fg_row.reshape(B, H, T, 1)
    hn, c_p, n_p, m_p = _mlstm_prompt(z3, li_row, fg_row, li_col, fg_col, mhg.reshape(1, MLSTM_WIDTH),
                                      L=min(T, 256), heads=MLSTM_HEADS_PER_STEP)

    h_p, hg_p, ssq_p = _out_proj_prompt(att.reshape(B * T, ATT_WIDTH), hn.reshape(B * T, MLSTM_WIDTH),
                                        wo, xp2, g2, tm=tm_w, tn=1024)
    u_p, wd = _up_proj_prompt(hg_p, ssq_p, wu, [w_down[0]], tm=tm_w, tn=1024)
    y_p = _down_proj(u_p, wd, h_p, gf, tm=tm, tk=1024)

    k_prompt = z3[:, T - keep:, ATT_WIDTH:2 * ATT_WIDTH].reshape(1, B, keep, ATT_HEADS, HEAD_DIM)
    v_prompt = z3[:, T - keep:, 2 * ATT_WIDTH:3 * ATT_WIDTH].reshape(1, B, keep, ATT_HEADS, HEAD_DIM)

    xs2 = x_sample.reshape(Bd, D)
    zs, gates_s = _in_proj(xs2, g1, w_main, w_gate, bias, cos_s, sin_s, tm=Bd, tn=1024, pos_tiles=1)
    qa_s = zs[:, :ATT_WIDTH].reshape(Bd, ATT_HEADS, HEAD_DIM)
    ka_s = zs[:, ATT_WIDTH:2 * ATT_WIDTH].reshape(Bd, ATT_HEADS, HEAD_DIM)
    va_s = zs[:, 2 * ATT_WIDTH:3 * ATT_WIDTH].reshape(Bd, ATT_HEADS, HEAD_DIM)
    att_s = _attn_sample(qa_s, ka_s, va_s, cache_k[0], cache_v[0])

    qk_s = _qk_sample(xs2, g1, w_in_t)
    qs = qk_s[:, :H * MLSTM_DQK].reshape(Bd, H, MLSTM_DQK)
    ks = qk_s[:, H * MLSTM_DQK:].reshape(Bd, H, MLSTM_DQK)
    o = 3 * ATT_WIDTH + 2 * H * MLSTM_DQK
    vs = zs[:, o:o + MLSTM_WIDTH].reshape(Bd, H, MLSTM_DV)
    o += MLSTM_WIDTH
    oms = zs[:, o:o + MLSTM_WIDTH].reshape(Bd, H, MLSTM_DV)
    hn_s, c_s, n_s, m_s = _mlstm_sample(qs, ks, vs, oms, gates_s.reshape(Bd, 1, LANES),
                                        state_C[0], state_n[0], state_m[0].reshape(Bd, 1, H),
                                        mhg.reshape(H, MLSTM_DV))

    mix_s = jnp.concatenate([att_s.reshape(Bd, ATT_WIDTH), hn_s.reshape(Bd, MLSTM_WIDTH)], axis=-1)
    h_s = _out_proj(mix_s, wo, xs2, tm=Bd, tn=1024)
    u_s = _up_proj(h_s, g2, wu, tm=Bd, tn=1024)
    y_s = _down_proj(u_s, wd, h_s, gf, tm=Bd, tk=1024)

    k_sample = zs[:, ATT_WIDTH:2 * ATT_WIDTH].reshape(1, Bd, 1, ATT_HEADS, HEAD_DIM)
    v_sample = zs[:, 2 * ATT_WIDTH:3 * ATT_WIDTH].reshape(1, Bd, 1, ATT_HEADS, HEAD_DIM)

    return (y_p.reshape(B, T, D), y_s.reshape(Bd, 1, D),
            k_prompt, v_prompt,
            c_p[None], n_p.reshape(1, B, H, MLSTM_DQK), m_p.reshape(1, B, H),
            k_sample, v_sample,
            c_s[None], n_s[None], m_s.reshape(1, Bd, H))
```

```python
import functools

import jax
import jax.numpy as jnp
from jax import lax
from jax.experimental import pallas as pl
from jax.experimental.pallas import tpu as pltpu

F32 = jnp.float32
BF16 = jnp.bfloat16

D_MODEL = 4096
HEAD_DIM = 128
ATT_HEADS = 8
ATT_WIDTH = ATT_HEADS * HEAD_DIM
MLSTM_HEADS = 6
MLSTM_DQK = 256
MLSTM_DV = 512
MLSTM_WIDTH = MLSTM_HEADS * MLSTM_DV
N_MAIN = 3 * ATT_WIDTH + 2 * MLSTM_HEADS * MLSTM_DQK + 2 * MLSTM_WIDTH
GATE_CAP = 15.0
DILATED_CONFIGS = ((128, 1), (512, 4), (2048, 16))
WIN_MAX = 2048
ROPE_THETA = 10000.0
EPS = 1e-6
PAST_LEN = 8192
NEG_INF = float("-inf")

LANES = 128
V7X_VMEM_BYTES = 64 * 1024 * 1024
Q_TILE = 128
ACC_COLS = 512
ATT_GROUP = 4
NT_DIMS = (((1,), (1,)), ((), ()))
BF16_SUBLANES = 16
MLSTM_HEADS_PER_STEP = 2


def _vmem_limit(block_bytes, scratch_bytes, temp_bytes):
    need = 2 * sum(block_bytes) + scratch_bytes + temp_bytes
    return int(min(max(need, 16 * 1024 * 1024), V7X_VMEM_BYTES - 4 * 1024 * 1024))


def _nbytes(shape, dtype):
    n = 1
    for s in shape:
        n *= s
    return n * jnp.dtype(dtype).itemsize


def _log_sigmoid(x):
    return jnp.minimum(x, 0.0) - jnp.log1p(jnp.exp(-jnp.abs(x)))


def _sigmoid(x):
    return 1.0 / (1.0 + jnp.exp(-x))


def _rms_scale(x):
    return lax.rsqrt(jnp.mean(x * x, axis=-1, keepdims=True) + EPS)


def _norm_gate_kernel(x_ref, g_ref, wg_ref, bias_ref, xn_ref, gate_ref):
    xn = (x_ref[...] * _rms_scale(x_ref[...]) * g_ref[...]).astype(BF16)
    xn_ref[...] = xn
    pre = lax.dot_general(xn, wg_ref[...], NT_DIMS, preferred_element_type=F32) + bias_ref[...]
    gate_ref[...] = GATE_CAP * jnp.tanh(pre / GATE_CAP)


def _norm_kernel(x_ref, g_ref, xn_ref):
    xn_ref[...] = (x_ref[...] * _rms_scale(x_ref[...]) * g_ref[...]).astype(BF16)


def _norm_rows(x2d, gain, w_gate=None, bias=None, *, tm):
    M, D = x2d.shape
    with_gates = w_gate is not None
    in_specs = [pl.BlockSpec((tm, D), lambda i: (i, 0)), pl.BlockSpec((1, D), lambda i: (0, 0))]
    out_specs = [pl.BlockSpec((tm, D), lambda i: (i, 0))]
    out_shape = [jax.ShapeDtypeStruct((M, D), BF16)]
    args = [x2d, gain]
    if with_gates:
        in_specs += [pl.BlockSpec((LANES, D), lambda i: (0, 0)), pl.BlockSpec((1, LANES), lambda i: (0, 0))]
        out_specs.append(pl.BlockSpec((tm, LANES), lambda i: (i, 0)))
        out_shape.append(jax.ShapeDtypeStruct((M, LANES), F32))
        args += [w_gate, bias]
    blocks = [_nbytes((tm, D), F32), _nbytes((tm, D), BF16), _nbytes((LANES, D), BF16), _nbytes((tm, LANES), F32)]
    out = pl.pallas_call(
        _norm_gate_kernel if with_gates else _norm_kernel,
        grid=(M // tm,),
        in_specs=in_specs,
        out_specs=out_specs,
        out_shape=out_shape,
        compiler_params=pltpu.CompilerParams(
            dimension_semantics=("arbitrary",),
            vmem_limit_bytes=_vmem_limit(blocks, 0, _nbytes((tm, D), F32))),
        name="norm_gate" if with_gates else "norm",
    )(*args)
    return out if with_gates else out[0]


def _cast_kernel(w_ref, o_ref):
    o_ref[...] = w_ref[...].astype(o_ref.dtype)


def _cast_rows_bf16(w, n_rows, *, tr=1024, tc=1024):
    _, C = w.shape
    assert n_rows % tr == 0 and C % tc == 0
    return pl.pallas_call(
        _cast_kernel,
        grid=(n_rows // tr, C // tc),
        in_specs=[pl.BlockSpec((tr, tc), lambda i, j: (i, j))],
        out_specs=pl.BlockSpec((tr, tc), lambda i, j: (i, j)),
        out_shape=jax.ShapeDtypeStruct((n_rows, C), BF16),
        compiler_params=pltpu.CompilerParams(
            dimension_semantics=("arbitrary", "arbitrary"),
            vmem_limit_bytes=_vmem_limit([_nbytes((tr, tc), F32), _nbytes((tr, tc), BF16)], 0, 0)),
        name="cast_bf16",
    )(w)


def _store_with_rope(product, z_ref, cos_ref, sin_ref, j, n_rope_tiles, tn):
    @pl.when(j < n_rope_tiles)
    def _():
        acc = product()
        c = cos_ref[...]
        s = sin_ref[...]
        for t in range(tn // HEAD_DIM):
            a = acc[:, t * HEAD_DIM:(t + 1) * HEAD_DIM]
            z_ref[:, t * HEAD_DIM:(t + 1) * HEAD_DIM] = a * c + pltpu.roll(a, HEAD_DIM // 2, 1) * s

    @pl.when(j >= n_rope_tiles)
    def _():
        z_ref[...] = product()


class _SideCast:
    def __init__(self, weights, n_steps, n_inner):
        rows = weights[0].shape[0]
        assert all(w.shape[0] == rows for w in weights)
        self.weights = weights
        self.n_inner = n_inner
        self.n_blocks = min(1 << (n_steps.bit_length() - 1), rows // BF16_SUBLANES)
        self.rows = rows // self.n_blocks

    def _index(self, a, b):
        return (jnp.minimum(a * self.n_inner + b, self.n_blocks - 1), 0)

    def specs(self):
        return [pl.BlockSpec((self.rows, w.shape[1]), self._index) for w in self.weights]

    def out_shapes(self):
        return [jax.ShapeDtypeStruct(w.shape, BF16) for w in self.weights]

    def block_bytes(self):
        return [_nbytes((self.rows, w.shape[1]), F32) + _nbytes((self.rows, w.shape[1]), BF16)
                for w in self.weights]

    def run(self, src_refs, dst_refs):
        step = pl.program_id(0) * self.n_inner + pl.program_id(1)

        @pl.when(step < self.n_blocks)
        def _():
            for src, dst in zip(src_refs, dst_refs):
                dst[...] = src[...].astype(dst.dtype)


def _inproj_prompt_kernel(xn_ref, w_ref, cos_ref, sin_ref, *refs, n_rope_tiles, tn, side):
    n_side = len(side.weights)
    side_in, z_ref, side_out = refs[:n_side], refs[n_side], refs[n_side + 1:]

    def product():
        return lax.dot_general(xn_ref[...], w_ref[...], NT_DIMS, preferred_element_type=F32)

    _store_with_rope(product, z_ref, cos_ref, sin_ref, pl.program_id(1), n_rope_tiles, tn)
    side.run(side_in, side_out)


def _in_proj_prompt(xn, w_t, cosf, sinf, side_weights, *, tm, tn, pos_tiles):
    M, D = xn.shape
    N = w_t.shape[0]
    grid = (M // tm, N // tn)
    side = _SideCast(side_weights, grid[0] * grid[1], grid[1])
    blocks = [_nbytes((tm, D), BF16) // 2, _nbytes((tn, D), BF16), 2 * _nbytes((tm, LANES), F32),
              _nbytes((tm, tn), F32)] + side.block_bytes()
    return pl.pallas_call(
        functools.partial(_inproj_prompt_kernel, n_rope_tiles=2 * ATT_WIDTH // tn, tn=tn, side=side),
        grid=grid,
        in_specs=[
            pl.BlockSpec((tm, D), lambda i, j: (i, 0), pipeline_mode=pl.Buffered(1)),
            pl.BlockSpec((tn, D), lambda i, j: (j, 0)),
            pl.BlockSpec((tm, LANES), lambda i, j: (i % pos_tiles, 0)),
            pl.BlockSpec((tm, LANES), lambda i, j: (i % pos_tiles, 0)),
        ] + side.specs(),
        out_specs=[pl.BlockSpec((tm, tn), lambda i, j: (i, j))] + side.specs(),
        out_shape=[jax.ShapeDtypeStruct((M, N), F32)] + side.out_shapes(),
        compiler_params=pltpu.CompilerParams(
            dimension_semantics=("arbitrary", "arbitrary"),
            vmem_limit_bytes=_vmem_limit(blocks, 0, 3 * _nbytes((tm, tn), F32))),
        name="in_proj_prompt",
    )(xn, w_t, cosf, sinf, *side_weights)


def _upproj_prompt_kernel(xn_ref, w_ref, *refs, side):
    n_side = len(side.weights)
    side_in, u_ref, side_out = refs[:n_side], refs[n_side], refs[n_side + 1:]
    acc = jnp.dot(xn_ref[...], w_ref[...], preferred_element_type=F32)
    u_ref[...] = jnp.square(jnp.maximum(acc, 0.0)).astype(u_ref.dtype)
    side.run(side_in, side_out)


def _up_proj_prompt(xn, w_up, side_weights, *, tm, tn):
    M, D = xn.shape
    N = w_up.shape[1]
    grid = (M // tm, N // tn)
    side = _SideCast(side_weights, grid[0] * grid[1], grid[1])
    blocks = [_nbytes((tm, D), BF16), _nbytes((D, tn), BF16), _nbytes((tm, tn), BF16)] + side.block_bytes()
    return pl.pallas_call(
        functools.partial(_upproj_prompt_kernel, side=side),
        grid=grid,
        in_specs=[pl.BlockSpec((tm, D), lambda i, j: (i, 0)),
                  pl.BlockSpec((D, tn), lambda i, j: (0, j))] + side.specs(),
        out_specs=[pl.BlockSpec((tm, tn), lambda i, j: (i, j))] + side.specs(),
        out_shape=[jax.ShapeDtypeStruct((M, N), BF16)] + side.out_shapes(),
        compiler_params=pltpu.CompilerParams(
            dimension_semantics=("arbitrary", "arbitrary"),
            vmem_limit_bytes=_vmem_limit(blocks, 0, 3 * _nbytes((tm, tn), F32))),
        name="up_proj_prompt",
    )(xn, w_up, *side_weights)


def _outproj_prompt_kernel(att_ref, hn_ref, w_ref, x_ref, h_ref):
    h_ref[...] = (x_ref[...]
                  + jnp.dot(att_ref[...], w_ref[:ATT_WIDTH, :], preferred_element_type=F32)
                  + jnp.dot(hn_ref[...], w_ref[ATT_WIDTH:, :], preferred_element_type=F32))


def _out_proj_prompt(att2d, hn2d, w, x2d, *, tm, tn):
    M, D = x2d.shape
    K = w.shape[0]
    blocks = [_nbytes((tm, K), BF16), _nbytes((K, tn), BF16), 2 * _nbytes((tm, tn), F32)]
    return pl.pallas_call(
        _outproj_prompt_kernel,
        grid=(M // tm, D // tn),
        in_specs=[pl.BlockSpec((tm, ATT_WIDTH), lambda i, j: (i, 0)),
                  pl.BlockSpec((tm, MLSTM_WIDTH), lambda i, j: (i, 0)),
                  pl.BlockSpec((K, tn), lambda i, j: (0, j)),
                  pl.BlockSpec((tm, tn), lambda i, j: (i, j))],
        out_specs=pl.BlockSpec((tm, tn), lambda i, j: (i, j)),
        out_shape=jax.ShapeDtypeStruct((M, D), F32),
        compiler_params=pltpu.CompilerParams(
            dimension_semantics=("arbitrary", "arbitrary"),
            vmem_limit_bytes=_vmem_limit(blocks, 0, 3 * _nbytes((tm, tn), F32))),
        name="out_proj_prompt",
    )(att2d, hn2d, w, x2d)


def _inproj_kernel(x_ref, g_ref, w_ref, wg_ref, bias_ref, cos_ref, sin_ref,
                   z_ref, gate_ref, xn_ref, *, n_rope_tiles, tn):
    j = pl.program_id(1)

    @pl.when(j == 0)
    def _():
        x = x_ref[...]
        xn = (x * _rms_scale(x) * g_ref[...]).astype(BF16)
        xn_ref[...] = xn
        pre = lax.dot_general(xn, wg_ref[...], NT_DIMS, preferred_element_type=F32) + bias_ref[...]
        gate_ref[...] = GATE_CAP * jnp.tanh(pre / GATE_CAP)

    def product():
        return lax.dot_general(xn_ref[...], w_ref[...], NT_DIMS, preferred_element_type=F32)

    _store_with_rope(product, z_ref, cos_ref, sin_ref, j, n_rope_tiles, tn)


def _in_proj(x2d, gain, w_main, w_gate, bias, cosf, sinf, *, tm, tn, pos_tiles):
    M, D = x2d.shape
    N = w_main.shape[0]
    blocks = [_nbytes((tm, D), F32), _nbytes((tn, D), BF16), _nbytes((LANES, D), BF16),
              2 * _nbytes((tm, LANES), F32), _nbytes((tm, tn), F32), _nbytes((tm, LANES), F32)]
    return pl.pallas_call(
        functools.partial(_inproj_kernel, n_rope_tiles=2 * ATT_WIDTH // tn, tn=tn),
        grid=(M // tm, N // tn),
        in_specs=[
            pl.BlockSpec((tm, D), lambda i, j: (i, 0)),
            pl.BlockSpec((1, D), lambda i, j: (0, 0)),
            pl.BlockSpec((tn, D), lambda i, j: (j, 0)),
            pl.BlockSpec((LANES, D), lambda i, j: (0, 0)),
            pl.BlockSpec((1, LANES), lambda i, j: (0, 0)),
            pl.BlockSpec((tm, LANES), lambda i, j: (i % pos_tiles, 0)),
            pl.BlockSpec((tm, LANES), lambda i, j: (i % pos_tiles, 0)),
        ],
        out_specs=[pl.BlockSpec((tm, tn), lambda i, j: (i, j)),
                   pl.BlockSpec((tm, LANES), lambda i, j: (i, 0))],
        out_shape=[jax.ShapeDtypeStruct((M, N), F32), jax.ShapeDtypeStruct((M, LANES), F32)],
        scratch_shapes=[pltpu.VMEM((tm, D), BF16)],
        compiler_params=pltpu.CompilerParams(
            dimension_semantics=("arbitrary", "arbitrary"),
            vmem_limit_bytes=_vmem_limit(blocks, _nbytes((tm, D), BF16),
                                         _nbytes((tm, D), F32) + 2 * _nbytes((tm, tn), F32))),
        name="in_proj",
    )(x2d, gain, w_main, w_gate, bias, cosf, sinf)


def _attn_prompt_kernel(q_ref, k_ref, v_ref, o_ref, m_s, l_s, acc_s, *, T):
    scale = HEAD_DIM ** -0.5
    G = ATT_GROUP
    row = lax.broadcasted_iota(jnp.int32, (G * Q_TILE, 2 * Q_TILE), 0) & (Q_TILE - 1)
    col = lax.broadcasted_iota(jnp.int32, (G * Q_TILE, 2 * Q_TILE), 1)
    band = jnp.logical_and(col >= row, col - Q_TILE <= row)
    prev_cols = lax.broadcasted_iota(jnp.int32, (Q_TILE, 2 * Q_TILE), 1) < Q_TILE
    nt = (((1,), (1,)), ((), ()))

    for ci, (_, r) in enumerate(DILATED_CONFIGS):
        tiles_per_class = T // (r * Q_TILE)
        run = min(G, tiles_per_class)
        runs_per_group = G // run
        runs_per_class = tiles_per_class // run
        shift = runs_per_class.bit_length() - 1
        n_groups = T // (Q_TILE * G)

        def rows(ref, start, n, r=r):
            if r == 1:
                return ref[pl.ds(start, n), :]
            return ref[pl.ds(start, n, stride=r), :]

        def put(ref, start, n, val, r=r):
            if r == 1:
                ref[pl.ds(start, n), :] = val
            else:
                ref[pl.ds(start, n, stride=r), :] = val

        def body(gi, carry, r=r, ci=ci, run=run, rpg=runs_per_group, rpc=runs_per_class,
                 shift=shift, rows=rows, put=put):
            n = run * Q_TILE
            starts, vxs, scores = [], [], []
            for j in range(rpg):
                ridx = gi * rpg + j
                cls = lax.shift_right_logical(ridx, shift)
                u0 = (ridx & (rpc - 1)) * n
                start = cls + r * u0
                pstart = cls + r * jnp.maximum(u0 - Q_TILE, 0)
                no_prev = u0 == 0
                q = (rows(q_ref, start, n) * scale).astype(BF16)
                kx = jnp.concatenate([rows(k_ref, pstart, Q_TILE), rows(k_ref, start, n)], axis=0).astype(BF16)
                vx = jnp.concatenate([rows(v_ref, pstart, Q_TILE), rows(v_ref, start, n)], axis=0).astype(BF16)
                for g in range(run):
                    s = lax.dot_general(q[g * Q_TILE:(g + 1) * Q_TILE], kx[g * Q_TILE:(g + 2) * Q_TILE], nt,
                                        preferred_element_type=F32)
                    if g == 0:
                        s = jnp.where(jnp.logical_and(prev_cols, no_prev), NEG_INF, s)
                    scores.append(s)
                starts.append(start)
                vxs.append(vx)
            s_all = jnp.where(band, jnp.concatenate(scores, axis=0), NEG_INF)
            mt = jnp.max(s_all, axis=1, keepdims=True)
            if ci == 0:
                m_new = jnp.broadcast_to(mt, (G * Q_TILE, LANES))
            else:
                m_old = jnp.concatenate([rows(m_s, st, n) for st in starts], axis=0)
                m_new = jnp.maximum(m_old, mt)
                alpha = jnp.exp(m_old - m_new)
            p = jnp.exp(s_all - jnp.concatenate([m_new, m_new], axis=1))
            lt = jnp.sum(p, axis=1, keepdims=True)
            pb = p.astype(BF16)
            outs = []
            for j in range(rpg):
                for g in range(run):
                    t = j * run + g
                    outs.append(jnp.dot(pb[t * Q_TILE:(t + 1) * Q_TILE], vxs[j][g * Q_TILE:(g + 2) * Q_TILE],
                                        preferred_element_type=F32))
            ot = jnp.concatenate(outs, axis=0)
            if ci == 0:
                l_new = jnp.broadcast_to(lt, (G * Q_TILE, LANES))
            else:
                l_new = alpha * jnp.concatenate([rows(l_s, st, n) for st in starts], axis=0) + lt
                ot = alpha * jnp.concatenate([rows(acc_s, st, n) for st in starts], axis=0) + ot
            for j, st in enumerate(starts):
                put(m_s, st, n, m_new[j * n:(j + 1) * n])
                put(l_s, st, n, l_new[j * n:(j + 1) * n])
                put(acc_s, st, n, ot[j * n:(j + 1) * n])
            return carry

        lax.fori_loop(0, n_groups, body, 0)

    o_ref[...] = (acc_s[...] / l_s[...]).astype(o_ref.dtype)


def _attn_prompt(z3):
    B, T, _ = z3.shape
    assert T % (DILATED_CONFIGS[-1][1] * Q_TILE) == 0
    blk = (None, T, HEAD_DIM)
    blocks = [3 * _nbytes((T, HEAD_DIM), F32), _nbytes((T, HEAD_DIM), BF16)]
    return pl.pallas_call(
        functools.partial(_attn_prompt_kernel, T=T),
        grid=(B, ATT_HEADS),
        in_specs=[pl.BlockSpec(blk, lambda b, h: (b, 0, h)),
                  pl.BlockSpec(blk, lambda b, h: (b, 0, ATT_HEADS + h)),
                  pl.BlockSpec(blk, lambda b, h: (b, 0, 2 * ATT_HEADS + h))],
        out_specs=pl.BlockSpec(blk, lambda b, h: (b, 0, h)),
        out_shape=jax.ShapeDtypeStruct((B, T, ATT_WIDTH), BF16),
        scratch_shapes=[pltpu.VMEM((T, LANES), F32)] * 3,
        compiler_params=pltpu.CompilerParams(
            dimension_semantics=("arbitrary", "arbitrary"),
            vmem_limit_bytes=_vmem_limit(blocks, 3 * _nbytes((T, LANES), F32),
                                         2 * _nbytes((T, LANES), F32))),
        name="attn_prompt",
    )(z3, z3, z3)


def _mlstm_prompt_kernel(q_ref, k_ref, v_ref, om_ref, lir_ref, fgr_ref, lic_ref, fgc_ref, mhg_ref,
                         hn_ref, c_out, n_out, m_out, c_s, n_s, m_s, *, L, heads):
    c = pl.program_id(2)

    @pl.when(c == 0)
    def _():
        c_s[...] = jnp.zeros_like(c_s)
        n_s[...] = jnp.zeros_like(n_s)
        m_s[...] = jnp.zeros_like(m_s)

    r_idx = lax.broadcasted_iota(jnp.int32, (L, L), 0)
    c_idx = lax.broadcasted_iota(jnp.int32, (L, L), 1)
    causal = c_idx <= r_idx

    for hh in range(heads):
        qs = slice(hh * MLSTM_DQK, (hh + 1) * MLSTM_DQK)
        vs = slice(hh * MLSTM_DV, (hh + 1) * MLSTM_DV)
        li_row = lir_ref[hh]
        li_col = lic_ref[hh]
        lf_row = _log_sigmoid(fgr_ref[hh])
        lf_col = _log_sigmoid(fgc_ref[hh])
        b_col = jnp.sum(jnp.where(causal, lf_row, 0.0), axis=1, keepdims=True)
        b_row = jnp.sum(jnp.where(r_idx <= c_idx, lf_col, 0.0), axis=0, keepdims=True)

        m_prev = m_s[hh]
        dlog = jnp.where(causal, b_col - b_row + li_row, NEG_INF)
        inter = b_col + m_prev
        m_t = jnp.maximum(inter, jnp.max(dlog, axis=1, keepdims=True))
        dmat = jnp.exp(dlog - m_t)
        a = jnp.exp(inter - m_t)

        qf = q_ref[:, qs] * (MLSTM_DQK ** -0.5)
        qb = qf.astype(BF16)
        kf = k_ref[:, qs]
        kb = kf.astype(BF16)
        vb = v_ref[:, vs].astype(BF16)
        s = lax.dot_general(qb, kb, NT_DIMS, preferred_element_type=F32)
        sw = s * dmat
        c_prev = c_s[hh]
        num = (jnp.dot(sw.astype(BF16), vb, preferred_element_type=F32)
               + a * jnp.dot(qb, c_prev.astype(BF16), preferred_element_type=F32))
        den = (jnp.sum(sw, axis=1, keepdims=True)
               + a * jnp.sum(qf * n_s[hh], axis=1, keepdims=True))
        h = num / jnp.maximum(jnp.abs(den), jnp.exp(-m_t))

        m_new = m_t[L - 1:L, :]
        b_last = b_col[L - 1:L, :]
        ws = jnp.exp(b_last - b_col + li_col - m_new)
        decay = jnp.exp(b_last + m_prev - m_new)
        kw = kf * ws
        kwt = kw.T.astype(BF16)
        c_s[hh] = decay * c_prev + jnp.dot(kwt, vb, preferred_element_type=F32)
        n_s[hh] = decay * n_s[hh] + jnp.sum(kw, axis=0, keepdims=True)
        m_s[hh] = m_new

        hn = h * _rms_scale(h) * mhg_ref[:, vs] * _sigmoid(om_ref[:, vs])
        hn_ref[:, vs] = hn.astype(hn_ref.dtype)

    @pl.when(c == pl.num_programs(2) - 1)
    def _():
        c_out[...] = c_s[...]
        n_out[...] = n_s[...]
        m_out[...] = m_s[...]


def _mlstm_prompt(z3, li_row, fg_row, li_col, fg_col, mhg, *, L, heads):
    B, T, _ = z3.shape
    H = MLSTM_HEADS
    assert H % heads == 0
    wq, wv = heads * MLSTM_DQK, heads * MLSTM_DV
    q_off = 3 * ATT_WIDTH // wq
    k_off = q_off + H // heads
    v_off = (3 * ATT_WIDTH + 2 * H * MLSTM_DQK) // wv
    o_off = v_off + H // heads
    blocks = [2 * _nbytes((L, wq), F32), 2 * _nbytes((L, wv), F32),
              4 * heads * _nbytes((L, LANES), F32), _nbytes((L, wv), BF16),
              heads * _nbytes((MLSTM_DQK, MLSTM_DV), F32)]
    return pl.pallas_call(
        functools.partial(_mlstm_prompt_kernel, L=L, heads=heads),
        grid=(B, H // heads, T // L),
        in_specs=[
            pl.BlockSpec((None, L, wq), lambda b, h, c: (b, c, q_off + h)),
            pl.BlockSpec((None, L, wq), lambda b, h, c: (b, c, k_off + h)),
            pl.BlockSpec((None, L, wv), lambda b, h, c: (b, c, v_off + h)),
            pl.BlockSpec((None, L, wv), lambda b, h, c: (b, c, o_off + h)),
            pl.BlockSpec((None, heads, 1, L), lambda b, h, c: (b, h, 0, c)),
            pl.BlockSpec((None, heads, 1, L), lambda b, h, c: (b, h, 0, c)),
            pl.BlockSpec((None, heads, L, 1), lambda b, h, c: (b, h, c, 0)),
            pl.BlockSpec((None, heads, L, 1), lambda b, h, c: (b, h, c, 0)),
            pl.BlockSpec((1, wv), lambda b, h, c: (0, h)),
        ],
        out_specs=[
            pl.BlockSpec((None, L, wv), lambda b, h, c: (b, c, h)),
            pl.BlockSpec((None, heads, MLSTM_DQK, MLSTM_DV), lambda b, h, c: (b, h, 0, 0)),
            pl.BlockSpec((None, heads, 1, MLSTM_DQK), lambda b, h, c: (b, h, 0, 0)),
            pl.BlockSpec((None, heads, 1, 1), lambda b, h, c: (b, h, 0, 0)),
        ],
        out_shape=[
            jax.ShapeDtypeStruct((B, T, MLSTM_WIDTH), BF16),
            jax.ShapeDtypeStruct((B, H, MLSTM_DQK, MLSTM_DV), F32),
            jax.ShapeDtypeStruct((B, H, 1, MLSTM_DQK), F32),
            jax.ShapeDtypeStruct((B, H, 1, 1), F32),
        ],
        scratch_shapes=[pltpu.VMEM((heads, MLSTM_DQK, MLSTM_DV), F32),
                        pltpu.VMEM((heads, 1, MLSTM_DQK), F32),
                        pltpu.VMEM((heads, 1, 1), F32)],
        compiler_params=pltpu.CompilerParams(
            dimension_semantics=("arbitrary", "arbitrary", "arbitrary"),
            vmem_limit_bytes=_vmem_limit(blocks, heads * _nbytes((MLSTM_DQK, MLSTM_DV), F32),
                                         heads * (8 * _nbytes((L, MLSTM_DV), F32) + 8 * _nbytes((L, L), F32)))),
        name="mlstm_prompt",
    )(z3, z3, z3, z3, li_row, fg_row, li_col, fg_col, mhg)


def _outproj_kernel(mix_ref, w_ref, x_ref, h_ref):
    h_ref[...] = x_ref[...] + jnp.dot(mix_ref[...], w_ref[...], preferred_element_type=F32)


def _out_proj(mix2d, w, x2d, *, tm, tn):
    M, D = x2d.shape
    K = w.shape[0]
    blocks = [_nbytes((tm, K), BF16), _nbytes((K, tn), BF16), 2 * _nbytes((tm, tn), F32)]
    return pl.pallas_call(
        _outproj_kernel,
        grid=(M // tm, D // tn),
        in_specs=[
            pl.BlockSpec((tm, K), lambda i, j: (i, 0)),
            pl.BlockSpec((K, tn), lambda i, j: (0, j)),
            pl.BlockSpec((tm, tn), lambda i, j: (i, j)),
        ],
        out_specs=pl.BlockSpec((tm, tn), lambda i, j: (i, j)),
        out_shape=jax.ShapeDtypeStruct((M, D), F32),
        compiler_params=pltpu.CompilerParams(
            dimension_semantics=("arbitrary", "arbitrary"),
            vmem_limit_bytes=_vmem_limit(blocks, 0, 2 * _nbytes((tm, tn), F32))),
        name="out_proj",
    )(mix2d, w, x2d)


def _upproj_kernel(x_ref, g_ref, w_ref, u_ref, xn_ref):
    @pl.when(pl.program_id(1) == 0)
    def _():
        x = x_ref[...]
        xn_ref[...] = (x * _rms_scale(x) * g_ref[...]).astype(BF16)

    acc = jnp.dot(xn_ref[...], w_ref[...], preferred_element_type=F32)
    u_ref[...] = jnp.square(jnp.maximum(acc, 0.0)).astype(u_ref.dtype)


def _up_proj(h2d, gain, w_up, *, tm, tn):
    M, D = h2d.shape
    N = w_up.shape[1]
    blocks = [_nbytes((tm, D), F32), _nbytes((D, tn), BF16), _nbytes((tm, tn), BF16)]
    return pl.pallas_call(
        _upproj_kernel,
        grid=(M // tm, N // tn),
        in_specs=[pl.BlockSpec((tm, D), lambda i, j: (i, 0)),
                  pl.BlockSpec((1, D), lambda i, j: (0, 0)),
                  pl.BlockSpec((D, tn), lambda i, j: (0, j))],
        out_specs=pl.BlockSpec((tm, tn), lambda i, j: (i, j)),
        out_shape=jax.ShapeDtypeStruct((M, N), BF16),
        scratch_shapes=[pltpu.VMEM((tm, D), BF16)],
        compiler_params=pltpu.CompilerParams(
            dimension_semantics=("arbitrary", "arbitrary"),
            vmem_limit_bytes=_vmem_limit(blocks, _nbytes((tm, D), BF16),
                                         _nbytes((tm, D), F32) + 2 * _nbytes((tm, tn), F32))),
        name="up_proj",
    )(h2d, gain, w_up)


def _downproj_kernel(u_ref, w_ref, h_ref, g_ref, y_ref):
    k = pl.program_id(1)

    @pl.when(k == 0)
    def _():
        y_ref[...] = h_ref[...]

    u = u_ref[...]
    for n in range(0, y_ref.shape[1], ACC_COLS):
        y_ref[:, n:n + ACC_COLS] += jnp.dot(u, w_ref[:, n:n + ACC_COLS], preferred_element_type=F32)

    @pl.when(k == pl.num_programs(1) - 1)
    def _():
        scale = _rms_scale(y_ref[...])
        y_ref[...] = y_ref[...] * scale * g_ref[...]


def _down_proj(u2d, w_down, h2d, gain, *, tm, tk):
    M, D = h2d.shape
    K = u2d.shape[1]
    blocks = [_nbytes((tm, tk), BF16), _nbytes((tk, D), BF16), 2 * _nbytes((tm, D), F32)]
    return pl.pallas_call(
        _downproj_kernel,
        grid=(M // tm, K // tk),
        in_specs=[pl.BlockSpec((tm, tk), lambda i, k: (i, k)),
                  pl.BlockSpec((tk, D), lambda i, k: (k, 0)),
                  pl.BlockSpec((tm, D), lambda i, k: (i, 0)),
                  pl.BlockSpec((1, D), lambda i, k: (0, 0))],
        out_specs=pl.BlockSpec((tm, D), lambda i, k: (i, 0)),
        out_shape=jax.ShapeDtypeStruct((M, D), F32),
        compiler_params=pltpu.CompilerParams(
            dimension_semantics=("arbitrary", "arbitrary"),
            vmem_limit_bytes=_vmem_limit(blocks, 0, _nbytes((tm, D), F32) + _nbytes((tm, ACC_COLS), F32))),
        name="down_proj",
    )(u2d, w_down, h2d, gain)


def _attn_sample_kernel(q_ref, kn_ref, vn_ref, k1_ref, k4_ref, k16_ref, v1_ref, v4_ref, v16_ref, o_ref):
    q = q_ref[...] * (HEAD_DIM ** -0.5)
    kn = kn_ref[...]
    vn = vn_ref[...]
    s0 = jnp.sum(q * kn, axis=-1, keepdims=True)

    ms, dens, nums = [], [], []
    for kt_ref, vt_ref in ((k1_ref, v1_ref), (k4_ref, v4_ref), (k16_ref, v16_ref)):
        kt = kt_ref[...]
        vt = vt_ref[...]
        s = jnp.sum(q[None] * kt, axis=-1, keepdims=True)
        m = jnp.maximum(jnp.max(s, axis=0), s0)
        p = jnp.exp(s - m[None])
        p0 = jnp.exp(s0 - m)
        dens.append(jnp.sum(p, axis=0) + p0)
        nums.append(jnp.sum(p * vt, axis=0) + p0 * vn)
        ms.append(m)
    m_all = jnp.maximum(jnp.maximum(ms[0], ms[1]), ms[2])
    num = jnp.zeros((ATT_HEADS, HEAD_DIM), F32)
    den = jnp.zeros((ATT_HEADS, 1), F32)
    for m, d, n in zip(ms, dens, nums):
        wgt = jnp.exp(m - m_all)
        num = num + wgt * n
        den = den + wgt * d
    o_ref[...] = (num / den).astype(o_ref.dtype)


def _attn_sample(qs, kns, vns, cache_k, cache_v):
    Bd, W, H, hd = cache_k.shape
    new_spec = pl.BlockSpec((None, H, hd), lambda b: (b, 0, 0))
    in_specs = [new_spec, new_spec, new_spec]
    views_k, views_v = [], []
    for (w, r) in DILATED_CONFIGS:
        assert w // r == Q_TILE and W % w == 0 and PAST_LEN >= W >= w
        views_k.append(cache_k.reshape(Bd, W // r, r, H, hd))
        views_v.append(cache_v.reshape(Bd, W // r, r, H, hd))
    for _ in range(2):
        for (w, r) in DILATED_CONFIGS:
            last = W // w - 1
            in_specs.append(pl.BlockSpec((None, Q_TILE, None, H, hd),
                                         lambda b, last=last: (b, last, 0, 0, 0)))
    blocks = [3 * _nbytes((H, hd), F32), 6 * _nbytes((Q_TILE, H, hd), F32)]
    return pl.pallas_call(
        _attn_sample_kernel,
        grid=(Bd,),
        in_specs=in_specs,
        out_specs=pl.BlockSpec((None, H, hd), lambda b: (b, 0, 0)),
        out_shape=jax.ShapeDtypeStruct((Bd, H, hd), BF16),
        compiler_params=pltpu.CompilerParams(
            dimension_semantics=("arbitrary",),
            vmem_limit_bytes=_vmem_limit(blocks, 0, 8 * _nbytes((Q_TILE, H, hd), F32))),
        name="attn_sample",
    )(qs, kns, vns, *views_k, *views_v)


def _split_bf16(a):
    hi = a.astype(BF16)
    return hi, (a - hi.astype(F32)).astype(BF16)


def _qk_sample_kernel(x_ref, g_ref, w_ref, o_ref, xhi_ref, xlo_ref):
    @pl.when(pl.program_id(0) == 0)
    def _():
        x = x_ref[...]
        xhi_ref[...], xlo_ref[...] = _split_bf16(x * _rms_scale(x) * g_ref[...])

    whi, wlo = _split_bf16(w_ref[...])
    xhi = xhi_ref[...]
    o_ref[...] = (lax.dot_general(xhi, whi, NT_DIMS, preferred_element_type=F32)
                  + lax.dot_general(xhi, wlo, NT_DIMS, preferred_element_type=F32)
                  + lax.dot_general(xlo_ref[...], whi, NT_DIMS, preferred_element_type=F32))


def _qk_sample(x2d, gain, w_t, *, tn=512):
    Bd, D = x2d.shape
    n_out = 2 * MLSTM_HEADS * MLSTM_DQK
    first = 3 * ATT_WIDTH // tn
    blocks = [_nbytes((Bd, D), F32), _nbytes((tn, D), F32), _nbytes((Bd, tn), F32)]
    return pl.pallas_call(
        _qk_sample_kernel,
        grid=(n_out // tn,),
        in_specs=[pl.BlockSpec((Bd, D), lambda j: (0, 0)),
                  pl.BlockSpec((1, D), lambda j: (0, 0)),
                  pl.BlockSpec((tn, D), lambda j: (first + j, 0))],
        out_specs=pl.BlockSpec((Bd, tn), lambda j: (0, j)),
        out_shape=jax.ShapeDtypeStruct((Bd, n_out), F32),
        scratch_shapes=[pltpu.VMEM((Bd, D), BF16), pltpu.VMEM((Bd, D), BF16)],
        compiler_params=pltpu.CompilerParams(
            dimension_semantics=("arbitrary",),
            vmem_limit_bytes=_vmem_limit(blocks, _nbytes((Bd, D), F32), 4 * _nbytes((tn, D), BF16))),
        name="qk_sample",
    )(x2d, gain, w_t)


def _mlstm_sample_kernel(q_ref, k_ref, v_ref, om_ref, g_ref, c0_ref, n0_ref, m0_ref, mhg_ref,
                         hn_ref, c_out, n_out, m_out):
    H = MLSTM_HEADS
    for h in range(H):
        qf = q_ref[h:h + 1, :] * (MLSTM_DQK ** -0.5)
        kf = k_ref[h:h + 1, :]
        vf = v_ref[h:h + 1, :]
        li = g_ref[:, h:h + 1]
        lf = _log_sigmoid(g_ref[:, H + h:H + h + 1])
        m0 = m0_ref[:, h:h + 1]
        c0 = c0_ref[h]
        n0 = n0_ref[h:h + 1, :]

        inter = lf + m0
        m_t = jnp.maximum(inter, li)
        qk = jnp.sum(qf * kf, axis=1, keepdims=True)
        sw = qk * jnp.exp(li - m_t)
        a = jnp.exp(inter - m_t)
        qc = jnp.dot(jnp.broadcast_to(qf, (8, MLSTM_DQK)), c0, precision=lax.Precision.HIGHEST,
                     preferred_element_type=F32)[0:1, :]
        num = sw * vf + a * qc
        den = sw + a * jnp.sum(qf * n0, axis=1, keepdims=True)
        hh = num / jnp.maximum(jnp.abs(den), jnp.exp(-m_t))

        ws = jnp.exp(li - m_t)
        decay = jnp.exp(inter - m_t)
        kw = kf * ws
        kcol = jnp.broadcast_to(kw, (LANES, MLSTM_DQK)).T[:, 0:1]
        c_out[h] = decay * c0 + kcol * vf
        n_out[h:h + 1, :] = decay * n0 + kw
        m_out[:, h:h + 1] = m_t

        hn = hh * _rms_scale(hh) * mhg_ref[h:h + 1, :] * _sigmoid(om_ref[h:h + 1, :])
        hn_ref[h:h + 1, :] = hn.astype(hn_ref.dtype)


def _mlstm_sample(qs, ks, vs, oms, gs, c0, n0, m0, mhg):
    Bd = qs.shape[0]
    H = MLSTM_HEADS
    blocks = [2 * _nbytes((8, MLSTM_DQK), F32), 2 * _nbytes((8, MLSTM_DV), F32),
              2 * _nbytes((H, MLSTM_DQK, MLSTM_DV), F32), 4 * _nbytes((8, MLSTM_DV), F32)]
    return pl.pallas_call(
        _mlstm_sample_kernel,
        grid=(Bd,),
        in_specs=[
            pl.BlockSpec((None, H, MLSTM_DQK), lambda b: (b, 0, 0)),
            pl.BlockSpec((None, H, MLSTM_DQK), lambda b: (b, 0, 0)),
            pl.BlockSpec((None, H, MLSTM_DV), lambda b: (b, 0, 0)),
            pl.BlockSpec((None, H, MLSTM_DV), lambda b: (b, 0, 0)),
            pl.BlockSpec((None, 1, LANES), lambda b: (b, 0, 0)),
            pl.BlockSpec((None, H, MLSTM_DQK, MLSTM_DV), lambda b: (b, 0, 0, 0)),
            pl.BlockSpec((None, H, MLSTM_DQK), lambda b: (b, 0, 0)),
            pl.BlockSpec((None, 1, H), lambda b: (b, 0, 0)),
            pl.BlockSpec((H, MLSTM_DV), lambda b: (0, 0)),
        ],
        out_specs=[
            pl.BlockSpec((None, H, MLSTM_DV), lambda b: (b, 0, 0)),
            pl.BlockSpec((None, H, MLSTM_DQK, MLSTM_DV), lambda b: (b, 0, 0, 0)),
            pl.BlockSpec((None, H, MLSTM_DQK), lambda b: (b, 0, 0)),
            pl.BlockSpec((None, 1, H), lambda b: (b, 0, 0)),
        ],
        out_shape=[
            jax.ShapeDtypeStruct((Bd, H, MLSTM_DV), BF16),
            jax.ShapeDtypeStruct((Bd, H, MLSTM_DQK, MLSTM_DV), F32),
            jax.ShapeDtypeStruct((Bd, H, MLSTM_DQK), F32),
            jax.ShapeDtypeStruct((Bd, 1, H), F32),
        ],
        compiler_params=pltpu.CompilerParams(
            dimension_semantics=("arbitrary",),
            vmem_limit_bytes=_vmem_limit(blocks, 0, 4 * _nbytes((MLSTM_DQK, MLSTM_DV), F32))),
        name="mlstm_sample",
    )(qs, ks, vs, oms, gs, c0, n0, m0, mhg)


def _rope_tables(pos):
    half = HEAD_DIM // 2
    inv = ROPE_THETA ** (-jnp.arange(half, dtype=F32) / half)
    ang = pos.astype(F32)[:, None] * inv[None, :]
    cos = jnp.cos(ang)
    sin = jnp.sin(ang)
    return jnp.concatenate([cos, cos], axis=-1), jnp.concatenate([-sin, sin], axis=-1)


def _row_tile(M, cap):
    tm = min(M, cap)
    assert M % tm == 0
    return tm


def kernel(x_prompt, x_sample, cache_k, cache_v, state_C, state_n, state_m,
           norm1_g, w_in, b_if, mh_norm_g, w_out, norm2_g, w_up, w_down, final_g):
    B, T, D = x_prompt.shape
    Bd, Td, _ = x_sample.shape
    depth = w_in.shape[0]
    assert depth == 1 and Td == 1 and D == D_MODEL
    keep = min(WIN_MAX, T)
    H = MLSTM_HEADS

    w_in_t = jnp.transpose(w_in[0])
    w_gate = jnp.pad(w_in_t[N_MAIN:], ((0, LANES - 2 * H), (0, 0))).astype(BF16)
    bias = jnp.pad(b_if[0], (0, LANES - 2 * H)).reshape(1, LANES)
    g1 =norm1_g[0].reshape(1, D)
    g2 = norm2_g[0].reshape(1, D)
    gf = final_g.reshape(1, D)
    mhg = mh_norm_g[0]

    cos_p, sin_p = _rope_tables(jnp.arange(T, dtype=jnp.int32))
    cos_s, sin_s = _rope_tables(jnp.full((Bd,), PAST_LEN, dtype=jnp.int32))

    xp2 = x_prompt.reshape(B * T, D)
    tm = _row_tile(T, 512)
    tm_w = _row_tile(T, 1024)
    xn_p, gates = _norm_rows(xp2, g1, w_gate, bias, tm=tm)
    w_main = _cast_rows_bf16(w_in_t, N_MAIN)
    z, wo, wu = _in_proj_prompt(xn_p, w_main, cos_p, sin_p, [w_out[0], w_up[0]],
                                tm=tm_w, tn=1024, pos_tiles=T // tm_w)
    z3 = z.reshape(B, T, N_MAIN)
    att = _attn_prompt(z3)

    g3 = gates.reshape(B, T, LANES)
    li_row = jnp.swapaxes(g3[:, :, :H], 1, 2).reshape(B, H, 1, T)
    fg_row = jnp.swapaxes(g3[:, :, H:2 * H], 1, 2).reshape(B, H, 1, T)
    li_col = li_row.reshape(B, H, T, 1)
    fg_col = fg_row.reshape(B, H, T, 1)
    hn, c_p, n_p, m_p = _mlstm_prompt(z3, li_row, fg_row, li_col, fg_col, mhg.reshape(1, MLSTM_WIDTH),
                                      L=min(T, 256), heads=MLSTM_HEADS_PER_STEP)

    h_p = _out_proj_prompt(att.reshape(B * T, ATT_WIDTH), hn.reshape(B * T, MLSTM_WIDTH), wo, xp2,
                           tm=tm_w, tn=1024)
    hn2_p = _norm_rows(h_p, g2, tm=tm)
    u_p, wd = _up_proj_prompt(hn2_p, wu, [w_down[0]], tm=tm_w, tn=1024)
    y_p = _down_proj(u_p, wd, h_p, gf, tm=tm, tk=1024)

    k_prompt = z3[:, T - keep:, ATT_WIDTH:2 * ATT_WIDTH].reshape(1, B, keep, ATT_HEADS, HEAD_DIM)
    v_prompt = z3[:, T - keep:, 2 * ATT_WIDTH:3 * ATT_WIDTH].reshape(1, B, keep, ATT_HEADS, HEAD_DIM)

    xs2 = x_sample.reshape(Bd, D)
    zs, gates_s = _in_proj(xs2, g1, w_main, w_gate, bias, cos_s, sin_s, tm=Bd, tn=1024, pos_tiles=1)
    qa_s = zs[:, :ATT_WIDTH].reshape(Bd, ATT_HEADS, HEAD_DIM)
    ka_s = zs[:, ATT_WIDTH:2 * ATT_WIDTH].reshape(Bd, ATT_HEADS, HEAD_DIM)
    va_s = zs[:, 2 * ATT_WIDTH:3 * ATT_WIDTH].reshape(Bd, ATT_HEADS, HEAD_DIM)
    att_s = _attn_sample(qa_s, ka_s, va_s, cache_k[0], cache_v[0])

    qk_s = _qk_sample(xs2, g1, w_in_t)
    qs = qk_s[:, :H * MLSTM_DQK].reshape(Bd, H, MLSTM_DQK)
    ks = qk_s[:, H * MLSTM_DQK:].reshape(Bd, H, MLSTM_DQK)
    o = 3 * ATT_WIDTH + 2 * H * MLSTM_DQK
    vs = zs[:, o:o + MLSTM_WIDTH].reshape(Bd, H, MLSTM_DV)
    o += MLSTM_WIDTH
    oms = zs[:, o:o + MLSTM_WIDTH].reshape(Bd, H, MLSTM_DV)
    hn_s, c_s, n_s, m_s = _mlstm_sample(qs, ks, vs, oms, gates_s.reshape(Bd, 1, LANES),
                                        state_C[0], state_n[0], state_m[0].reshape(Bd, 1, H),
                                        mhg.reshape(H, MLSTM_DV))

    mix_s = jnp.concatenate([att_s.reshape(Bd, ATT_WIDTH), hn_s.reshape(Bd, MLSTM_WIDTH)], axis=-1)
    h_s = _out_proj(mix_s, wo, xs2, tm=Bd, tn=1024)
    u_s = _up_proj(h_s, g2, wu, tm=Bd, tn=1024)
    y_s = _down_proj(u_s, wd, h_s, gf, tm=Bd, tk=1024)

    k_sample = zs[:, ATT_WIDTH:2 * ATT_WIDTH].reshape(1, Bd, 1, ATT_HEADS, HEAD_DIM)
    v_sample = zs[:, 2 * ATT_WIDTH:3 * ATT_WIDTH].reshape(1, Bd, 1, ATT_HEADS, HEAD_DIM)

    return (y_p.reshape(B, T, D), y_s.reshape(Bd, 1, D),
            k_prompt, v_prompt,
            c_p[None], n_p.reshape(1, B, H, MLSTM_DQK), m_p.reshape(1, B, H),
            k_sample, v_sample,
            c_s[None], n_s[None], m_s.reshape(1, Bd, H))
```

```python
import functools

import jax
import jax.numpy as jnp
from jax import lax
from jax.experimental import pallas as pl
from jax.experimental.pallas import tpu as pltpu

F32 = jnp.float32
BF16 = jnp.bfloat16

D_MODEL = 4096
HEAD_DIM = 128
ATT_HEADS = 8
ATT_WIDTH = ATT_HEADS * HEAD_DIM
MLSTM_HEADS = 6
MLSTM_DQK = 256
MLSTM_DV = 512
MLSTM_WIDTH = MLSTM_HEADS * MLSTM_DV
N_MAIN = 3 * ATT_WIDTH + 2 * MLSTM_HEADS * MLSTM_DQK + 2 * MLSTM_WIDTH
GATE_CAP = 15.0
DILATED_CONFIGS = ((128, 1), (512, 4), (2048, 16))
WIN_MAX = 2048
ROPE_THETA = 10000.0
EPS = 1e-6
PAST_LEN = 8192
NEG_INF = float("-inf")

LANES = 128
V7X_VMEM_BYTES = 64 * 1024 * 1024
Q_TILE = 128
ACC_COLS = 512
ATT_GROUP = 4
NT_DIMS = (((1,), (1,)), ((), ()))
BF16_SUBLANES = 16
MLSTM_HEADS_PER_STEP = 2


def _vmem_limit(block_bytes, scratch_bytes, temp_bytes):
    need = 2 * sum(block_bytes) + scratch_bytes + temp_bytes
    return int(min(max(need, 16 * 1024 * 1024), V7X_VMEM_BYTES - 4 * 1024 * 1024))


def _nbytes(shape, dtype):
    n = 1
    for s in shape:
        n *= s
    return n * jnp.dtype(dtype).itemsize


def _log_sigmoid(x):
    return jnp.minimum(x, 0.0) - jnp.log1p(jnp.exp(-jnp.abs(x)))


def _sigmoid(x):
    return 1.0 / (1.0 + jnp.exp(-x))


def _rms_scale(x):
    return lax.rsqrt(jnp.mean(x * x, axis=-1, keepdims=True) + EPS)


def _norm_gate_kernel(x_ref, g_ref, wg_ref, bias_ref, xn_ref, gate_ref):
    xn = (x_ref[...] * _rms_scale(x_ref[...]) * g_ref[...]).astype(BF16)
    xn_ref[...] = xn
    pre = lax.dot_general(xn, wg_ref[...], NT_DIMS, preferred_element_type=F32) + bias_ref[...]
    gate_ref[...] = GATE_CAP * jnp.tanh(pre / GATE_CAP)


def _norm_kernel(x_ref, g_ref, xn_ref):
    xn_ref[...] = (x_ref[...] * _rms_scale(x_ref[...]) * g_ref[...]).astype(BF16)


def _norm_rows(x2d, gain, w_gate=None, bias=None, *, tm):
    M, D = x2d.shape
    with_gates = w_gate is not None
    in_specs = [pl.BlockSpec((tm, D), lambda i: (i, 0)), pl.BlockSpec((1, D), lambda i: (0, 0))]
    out_specs = [pl.BlockSpec((tm, D), lambda i: (i, 0))]
    out_shape = [jax.ShapeDtypeStruct((M, D), BF16)]
    args = [x2d, gain]
    if with_gates:
        in_specs += [pl.BlockSpec((LANES, D), lambda i: (0, 0)), pl.BlockSpec((1, LANES), lambda i: (0, 0))]
        out_specs.append(pl.BlockSpec((tm, LANES), lambda i: (i, 0)))
        out_shape.append(jax.ShapeDtypeStruct((M, LANES), F32))
        args += [w_gate, bias]
    blocks = [_nbytes((tm, D), F32), _nbytes((tm, D), BF16), _nbytes((LANES, D), BF16), _nbytes((tm, LANES), F32)]
    out = pl.pallas_call(
        _norm_gate_kernel if with_gates else _norm_kernel,
        grid=(M // tm,),
        in_specs=in_specs,
        out_specs=out_specs,
        out_shape=out_shape,
        compiler_params=pltpu.CompilerParams(
            dimension_semantics=("arbitrary",),
            vmem_limit_bytes=_vmem_limit(blocks, 0, _nbytes((tm, D), F32))),
        name="norm_gate" if with_gates else "norm",
    )(*args)
    return out if with_gates else out[0]


def _store_with_rope(product, z_ref, cos_ref, sin_ref, j, n_rope_tiles, tn):
    @pl.when(j < n_rope_tiles)
    def _():
        acc = product()
        c = cos_ref[...]
        s = sin_ref[...]
        for t in range(tn // HEAD_DIM):
            a = acc[:, t * HEAD_DIM:(t + 1) * HEAD_DIM]
            z_ref[:, t * HEAD_DIM:(t + 1) * HEAD_DIM] = a * c + pltpu.roll(a, HEAD_DIM // 2, 1) * s

    @pl.when(j >= n_rope_tiles)
    def _():
        z_ref[...] = product()


class _SideCast:
    def __init__(self, weights, n_steps, n_inner):
        rows = weights[0].shape[0]
        assert all(w.shape[0] == rows for w in weights)
        self.weights = weights
        self.n_inner = n_inner
        self.n_blocks = min(1 << (n_steps.bit_length() - 1), rows // BF16_SUBLANES)
        self.rows = rows // self.n_blocks

    def _index(self, a, b):
        return (jnp.minimum(a * self.n_inner + b, self.n_blocks - 1), 0)

    def specs(self):
        return [pl.BlockSpec((self.rows, w.shape[1]), self._index) for w in self.weights]

    def out_shapes(self):
        return [jax.ShapeDtypeStruct(w.shape, BF16) for w in self.weights]

    def block_bytes(self):
        return [_nbytes((self.rows, w.shape[1]), F32) + _nbytes((self.rows, w.shape[1]), BF16)
                for w in self.weights]

    def run(self, src_refs, dst_refs):
        step = pl.program_id(0) * self.n_inner + pl.program_id(1)

        @pl.when(step < self.n_blocks)
        def _():
            for src, dst in zip(src_refs, dst_refs):
                dst[...] = src[...].astype(dst.dtype)


def _inproj_prompt_kernel(xn_ref, w_ref, cos_ref, sin_ref, *refs, n_rope_tiles, tn, side):
    n_side = len(side.weights)
    side_in, z_ref, side_out = refs[:n_side], refs[n_side], refs[n_side + 1:]

    def product():
        return lax.dot_general(xn_ref[...], w_ref[...], NT_DIMS, preferred_element_type=F32)

    _store_with_rope(product, z_ref, cos_ref, sin_ref, pl.program_id(1), n_rope_tiles, tn)
    side.run(side_in, side_out)


def _in_proj_prompt(xn, w_t, cosf, sinf, side_weights, *, tm, tn, pos_tiles):
    M, D = xn.shape
    N = w_t.shape[0]
    grid = (M // tm, N // tn)
    side = _SideCast(side_weights, grid[0] * grid[1], grid[1])
    blocks = [_nbytes((tm, D), BF16) // 2, _nbytes((tn, D), BF16), 2 * _nbytes((tm, LANES), F32),
              _nbytes((tm, tn), F32)] + side.block_bytes()
    return pl.pallas_call(
        functools.partial(_inproj_prompt_kernel, n_rope_tiles=2 * ATT_WIDTH // tn, tn=tn, side=side),
        grid=grid,
        in_specs=[
            pl.BlockSpec((tm, D), lambda i, j: (i, 0), pipeline_mode=pl.Buffered(1)),
            pl.BlockSpec((tn, D), lambda i, j: (j, 0)),
            pl.BlockSpec((tm, LANES), lambda i, j: (i % pos_tiles, 0)),
            pl.BlockSpec((tm, LANES), lambda i, j: (i % pos_tiles, 0)),
        ] + side.specs(),
        out_specs=[pl.BlockSpec((tm, tn), lambda i, j: (i, j))] + side.specs(),
        out_shape=[jax.ShapeDtypeStruct((M, N), F32)] + side.out_shapes(),
        compiler_params=pltpu.CompilerParams(
            dimension_semantics=("arbitrary", "arbitrary"),
            vmem_limit_bytes=_vmem_limit(blocks, 0, 3 * _nbytes((tm, tn), F32))),
        name="in_proj_prompt",
    )(xn, w_t, cosf, sinf, *side_weights)


def _upproj_prompt_kernel(xn_ref, w_ref, *refs, side):
    n_side = len(side.weights)
    side_in, u_ref, side_out = refs[:n_side], refs[n_side], refs[n_side + 1:]
    acc = jnp.dot(xn_ref[...], w_ref[...], preferred_element_type=F32)
    u_ref[...] = jnp.square(jnp.maximum(acc, 0.0)).astype(u_ref.dtype)
    side.run(side_in, side_out)


def _up_proj_prompt(xn, w_up, side_weights, *, tm, tn):
    M, D = xn.shape
    N = w_up.shape[1]
    grid = (M // tm, N // tn)
    side = _SideCast(side_weights, grid[0] * grid[1], grid[1])
    blocks = [_nbytes((tm, D), BF16), _nbytes((D, tn), BF16), _nbytes((tm, tn), BF16)] + side.block_bytes()
    return pl.pallas_call(
        functools.partial(_upproj_prompt_kernel, side=side),
        grid=grid,
        in_specs=[pl.BlockSpec((tm, D), lambda i, j: (i, 0)),
                  pl.BlockSpec((D, tn), lambda i, j: (0, j))] + side.specs(),
        out_specs=[pl.BlockSpec((tm, tn), lambda i, j: (i, j))] + side.specs(),
        out_shape=[jax.ShapeDtypeStruct((M, N), BF16)] + side.out_shapes(),
        compiler_params=pltpu.CompilerParams(
            dimension_semantics=("arbitrary", "arbitrary"),
            vmem_limit_bytes=_vmem_limit(blocks, 0, 3 * _nbytes((tm, tn), F32))),
        name="up_proj_prompt",
    )(xn, w_up, *side_weights)


def _outproj_prompt_kernel(att_ref, hn_ref, w_ref, x_ref, h_ref):
    h_ref[...] = (x_ref[...]
                  + jnp.dot(att_ref[...], w_ref[:ATT_WIDTH, :], preferred_element_type=F32)
                  + jnp.dot(hn_ref[...], w_ref[ATT_WIDTH:, :], preferred_element_type=F32))


def _out_proj_prompt(att2d, hn2d, w, x2d, *, tm, tn):
    M, D = x2d.shape
    K = w.shape[0]
    blocks = [_nbytes((tm, K), BF16), _nbytes((K, tn), BF16), 2 * _nbytes((tm, tn), F32)]
    return pl.pallas_call(
        _outproj_prompt_kernel,
        grid=(M // tm, D // tn),
        in_specs=[pl.BlockSpec((tm, ATT_WIDTH), lambda i, j: (i, 0)),
                  pl.BlockSpec((tm, MLSTM_WIDTH), lambda i, j: (i, 0)),
                  pl.BlockSpec((K, tn), lambda i, j: (0, j)),
                  pl.BlockSpec((tm, tn), lambda i, j: (i, j))],
        out_specs=pl.BlockSpec((tm, tn), lambda i, j: (i, j)),
        out_shape=jax.ShapeDtypeStruct((M, D), F32),
        compiler_params=pltpu.CompilerParams(
            dimension_semantics=("arbitrary", "arbitrary"),
            vmem_limit_bytes=_vmem_limit(blocks, 0, 3 * _nbytes((tm, tn), F32))),
        name="out_proj_prompt",
    )(att2d, hn2d, w, x2d)


def _inproj_kernel(x_ref, g_ref, w_ref, wg_ref, bias_ref, cos_ref, sin_ref,
                   z_ref, gate_ref, xn_ref, *, n_rope_tiles, tn):
    j = pl.program_id(1)

    @pl.when(j == 0)
    def _():
        x = x_ref[...]
        xn = (x * _rms_scale(x) * g_ref[...]).astype(BF16)
        xn_ref[...] = xn
        pre = lax.dot_general(xn, wg_ref[...], NT_DIMS, preferred_element_type=F32) + bias_ref[...]
        gate_ref[...] = GATE_CAP * jnp.tanh(pre / GATE_CAP)

    def product():
        return lax.dot_general(xn_ref[...], w_ref[...], NT_DIMS, preferred_element_type=F32)

    _store_with_rope(product, z_ref, cos_ref, sin_ref, j, n_rope_tiles, tn)


def _in_proj(x2d, gain, w_main, w_gate, bias, cosf, sinf, *, tm, tn, pos_tiles):
    M, D = x2d.shape
    N = w_main.shape[0]
    blocks = [_nbytes((tm, D), F32), _nbytes((tn, D), BF16), _nbytes((LANES, D), BF16),
              2 * _nbytes((tm, LANES), F32), _nbytes((tm, tn), F32), _nbytes((tm, LANES), F32)]
    return pl.pallas_call(
        functools.partial(_inproj_kernel, n_rope_tiles=2 * ATT_WIDTH // tn, tn=tn),
        grid=(M // tm, N // tn),
        in_specs=[
            pl.BlockSpec((tm, D), lambda i, j: (i, 0)),
            pl.BlockSpec((1, D), lambda i, j: (0, 0)),
            pl.BlockSpec((tn, D), lambda i, j: (j, 0)),
            pl.BlockSpec((LANES, D), lambda i, j: (0, 0)),
            pl.BlockSpec((1, LANES), lambda i, j: (0, 0)),
            pl.BlockSpec((tm, LANES), lambda i, j: (i % pos_tiles, 0)),
            pl.BlockSpec((tm, LANES), lambda i, j: (i % pos_tiles, 0)),
        ],
        out_specs=[pl.BlockSpec((tm, tn), lambda i, j: (i, j)),
                   pl.BlockSpec((tm, LANES), lambda i, j: (i, 0))],
        out_shape=[jax.ShapeDtypeStruct((M, N), F32), jax.ShapeDtypeStruct((M, LANES), F32)],
        scratch_shapes=[pltpu.VMEM((tm, D), BF16)],
        compiler_params=pltpu.CompilerParams(
            dimension_semantics=("arbitrary", "arbitrary"),
            vmem_limit_bytes=_vmem_limit(blocks, _nbytes((tm, D), BF16),
                                         _nbytes((tm, D), F32) + 2 * _nbytes((tm, tn), F32))),
        name="in_proj",
    )(x2d, gain, w_main, w_gate, bias, cosf, sinf)


def _attn_prompt_kernel(q_ref, k_ref, v_ref, o_ref, m_s, l_s, acc_s, *, T):
    scale = HEAD_DIM ** -0.5
    G = ATT_GROUP
    row = lax.broadcasted_iota(jnp.int32, (G * Q_TILE, 2 * Q_TILE), 0) & (Q_TILE - 1)
    col = lax.broadcasted_iota(jnp.int32, (G * Q_TILE, 2 * Q_TILE), 1)
    band = jnp.logical_and(col >= row, col - Q_TILE <= row)
    prev_cols = lax.broadcasted_iota(jnp.int32, (Q_TILE, 2 * Q_TILE), 1) < Q_TILE
    nt = (((1,), (1,)), ((), ()))

    for ci, (_, r) in enumerate(DILATED_CONFIGS):
        tiles_per_class = T // (r * Q_TILE)
        run = min(G, tiles_per_class)
        runs_per_group = G // run
        runs_per_class = tiles_per_class // run
        shift = runs_per_class.bit_length() - 1
        n_groups = T // (Q_TILE * G)

        def rows(ref, start, n, r=r):
            if r == 1:
                return ref[pl.ds(start, n), :]
            return ref[pl.ds(start, n, stride=r), :]

        def put(ref, start, n, val, r=r):
            if r == 1:
                ref[pl.ds(start, n), :] = val
            else:
                ref[pl.ds(start, n, stride=r), :] = val

        def body(gi, carry, r=r, ci=ci, run=run, rpg=runs_per_group, rpc=runs_per_class,
                 shift=shift, rows=rows, put=put):
            n = run * Q_TILE
            starts, vxs, scores = [], [], []
            for j in range(rpg):
                ridx = gi * rpg + j
                cls = lax.shift_right_logical(ridx, shift)
                u0 = (ridx & (rpc - 1)) * n
                start = cls + r * u0
                pstart = cls + r * jnp.maximum(u0 - Q_TILE, 0)
                no_prev = u0 == 0
                q = (rows(q_ref, start, n) * scale).astype(BF16)
                kx = jnp.concatenate([rows(k_ref, pstart, Q_TILE), rows(k_ref, start, n)], axis=0).astype(BF16)
                vx = jnp.concatenate([rows(v_ref, pstart, Q_TILE), rows(v_ref, start, n)], axis=0).astype(BF16)
                for g in range(run):
                    s = lax.dot_general(q[g * Q_TILE:(g + 1) * Q_TILE], kx[g * Q_TILE:(g + 2) * Q_TILE], nt,
                                        preferred_element_type=F32)
                    if g == 0:
                        s = jnp.where(jnp.logical_and(prev_cols, no_prev), NEG_INF, s)
                    scores.append(s)
                starts.append(start)
                vxs.append(vx)
            s_all = jnp.where(band, jnp.concatenate(scores, axis=0), NEG_INF)
            mt = jnp.max(s_all, axis=1, keepdims=True)
            if ci == 0:
                m_new = jnp.broadcast_to(mt, (G * Q_TILE, LANES))
            else:
                m_old = jnp.concatenate([rows(m_s, st, n) for st in starts], axis=0)
                m_new = jnp.maximum(m_old, mt)
                alpha = jnp.exp(m_old - m_new)
            p = jnp.exp(s_all - jnp.concatenate([m_new, m_new], axis=1))
            lt = jnp.sum(p, axis=1, keepdims=True)
            pb = p.astype(BF16)
            outs = []
            for j in range(rpg):
                for g in range(run):
                    t = j * run + g
                    outs.append(jnp.dot(pb[t * Q_TILE:(t + 1) * Q_TILE], vxs[j][g * Q_TILE:(g + 2) * Q_TILE],
                                        preferred_element_type=F32))
            ot = jnp.concatenate(outs, axis=0)
            if ci == 0:
                l_new = jnp.broadcast_to(lt, (G * Q_TILE, LANES))
            else:
                l_new = alpha * jnp.concatenate([rows(l_s, st, n) for st in starts], axis=0) + lt
                ot = alpha * jnp.concatenate([rows(acc_s, st, n) for st in starts], axis=0) + ot
            for j, st in enumerate(starts):
                put(m_s, st, n, m_new[j * n:(j + 1) * n])
                put(l_s, st, n, l_new[j * n:(j + 1) * n])
                put(acc_s, st, n, ot[j * n:(j + 1) * n])
            return carry

        lax.fori_loop(0, n_groups, body, 0)

    o_ref[...] = (acc_s[...] / l_s[...]).astype(o_ref.dtype)


def _attn_prompt(z3):
    B, T, _ = z3.shape
    assert T % (DILATED_CONFIGS[-1][1] * Q_TILE) == 0
    blk = (None, T, HEAD_DIM)
    blocks = [3 * _nbytes((T, HEAD_DIM), F32), _nbytes((T, HEAD_DIM), BF16)]
    return pl.pallas_call(
        functools.partial(_attn_prompt_kernel, T=T),
        grid=(B, ATT_HEADS),
        in_specs=[pl.BlockSpec(blk, lambda b, h: (b, 0, h)),
                  pl.BlockSpec(blk, lambda b, h: (b, 0, ATT_HEADS + h)),
                  pl.BlockSpec(blk, lambda b, h: (b, 0, 2 * ATT_HEADS + h))],
        out_specs=pl.BlockSpec(blk, lambda b, h: (b, 0, h)),
        out_shape=jax.ShapeDtypeStruct((B, T, ATT_WIDTH), BF16),
        scratch_shapes=[pltpu.VMEM((T, LANES), F32)] * 3,
        compiler_params=pltpu.CompilerParams(
            dimension_semantics=("arbitrary", "arbitrary"),
            vmem_limit_bytes=_vmem_limit(blocks, 3 * _nbytes((T, LANES), F32),
                                         2 * _nbytes((T, LANES), F32))),
        name="attn_prompt",
    )(z3, z3, z3)


def _mlstm_prompt_kernel(q_ref, k_ref, v_ref, om_ref, lir_ref, fgr_ref, lic_ref, fgc_ref, mhg_ref,
                         hn_ref, c_out, n_out, m_out, c_s, n_s, m_s, *, L, heads):
    c = pl.program_id(2)

    @pl.when(c == 0)
    def _():
        c_s[...] = jnp.zeros_like(c_s)
        n_s[...] = jnp.zeros_like(n_s)
        m_s[...] = jnp.zeros_like(m_s)

    r_idx = lax.broadcasted_iota(jnp.int32, (L, L), 0)
    c_idx = lax.broadcasted_iota(jnp.int32, (L, L), 1)
    causal = c_idx <= r_idx

    for hh in range(heads):
        qs = slice(hh * MLSTM_DQK, (hh + 1) * MLSTM_DQK)
        vs = slice(hh * MLSTM_DV, (hh + 1) * MLSTM_DV)
        li_row = lir_ref[hh]
        li_col = lic_ref[hh]
        lf_row = _log_sigmoid(fgr_ref[hh])
        lf_col = _log_sigmoid(fgc_ref[hh])
        b_col = jnp.sum(jnp.where(causal, lf_row, 0.0), axis=1, keepdims=True)
        b_row = jnp.sum(jnp.where(r_idx <= c_idx, lf_col, 0.0), axis=0, keepdims=True)

        m_prev = m_s[hh]
        dlog = jnp.where(causal, b_col - b_row + li_row, NEG_INF)
        inter = b_col + m_prev
        m_t = jnp.maximum(inter, jnp.max(dlog, axis=1, keepdims=True))
        dmat = jnp.exp(dlog - m_t)
        a = jnp.exp(inter - m_t)

        qf = q_ref[:, qs] * (MLSTM_DQK ** -0.5)
        qb = qf.astype(BF16)
        kf = k_ref[:, qs]
        kb = kf.astype(BF16)
        vb = v_ref[:, vs].astype(BF16)
        s = lax.dot_general(qb, kb, NT_DIMS, preferred_element_type=F32)
        sw = s * dmat
        c_prev = c_s[hh]
        num = (jnp.dot(sw.astype(BF16), vb, preferred_element_type=F32)
               + a * jnp.dot(qb, c_prev.astype(BF16), preferred_element_type=F32))
        den = (jnp.sum(sw, axis=1, keepdims=True)
               + a * jnp.sum(qf * n_s[hh], axis=1, keepdims=True))
        h = num / jnp.maximum(jnp.abs(den), jnp.exp(-m_t))

        m_new = m_t[L - 1:L, :]
        b_last = b_col[L - 1:L, :]
        ws = jnp.exp(b_last - b_col + li_col - m_new)
        decay = jnp.exp(b_last + m_prev - m_new)
        kw = kf * ws
        kwt = kw.T.astype(BF16)
        c_s[hh] = decay * c_prev + jnp.dot(kwt, vb, preferred_element_type=F32)
        n_s[hh] = decay * n_s[hh] + jnp.sum(kw, axis=0, keepdims=True)
        m_s[hh] = m_new

        hn = h * _rms_scale(h) * mhg_ref[:, vs] * _sigmoid(om_ref[:, vs])
        hn_ref[:, vs] = hn.astype(hn_ref.dtype)

    @pl.when(c == pl.num_programs(2) - 1)
    def _():
        c_out[...] = c_s[...]
        n_out[...] = n_s[...]
        m_out[...] = m_s[...]


def _mlstm_prompt(z3, li_row, fg_row, li_col, fg_col, mhg, *, L, heads):
    B, T, _ = z3.shape
    H = MLSTM_HEADS
    assert H % heads == 0
    wq, wv = heads * MLSTM_DQK, heads * MLSTM_DV
    q_off = 3 * ATT_WIDTH // wq
    k_off = q_off + H // heads
    v_off = (3 * ATT_WIDTH + 2 * H * MLSTM_DQK) // wv
    o_off = v_off + H // heads
    blocks = [2 * _nbytes((L, wq), F32), 2 * _nbytes((L, wv), F32),
              4 * heads * _nbytes((L, LANES), F32), _nbytes((L, wv), BF16),
              heads * _nbytes((MLSTM_DQK, MLSTM_DV), F32)]
    return pl.pallas_call(
        functools.partial(_mlstm_prompt_kernel, L=L, heads=heads),
        grid=(B, H // heads, T // L),
        in_specs=[
            pl.BlockSpec((None, L, wq), lambda b, h, c: (b, c, q_off + h)),
            pl.BlockSpec((None, L, wq), lambda b, h, c: (b, c, k_off + h)),
            pl.BlockSpec((None, L, wv), lambda b, h, c: (b, c, v_off + h)),
            pl.BlockSpec((None, L, wv), lambda b, h, c: (b, c, o_off + h)),
            pl.BlockSpec((None, heads, 1, L), lambda b, h, c: (b, h, 0, c)),
            pl.BlockSpec((None, heads, 1, L), lambda b, h, c: (b, h, 0, c)),
            pl.BlockSpec((None, heads, L, 1), lambda b, h, c: (b, h, c, 0)),
            pl.BlockSpec((None, heads, L, 1), lambda b, h, c: (b, h, c, 0)),
            pl.BlockSpec((1, wv), lambda b, h, c: (0, h)),
        ],
        out_specs=[
            pl.BlockSpec((None, L, wv), lambda b, h, c: (b, c, h)),
            pl.BlockSpec((None, heads, MLSTM_DQK, MLSTM_DV), lambda b, h, c: (b, h, 0, 0)),
            pl.BlockSpec((None, heads, 1, MLSTM_DQK), lambda b, h, c: (b, h, 0, 0)),
            pl.BlockSpec((None, heads, 1, 1), lambda b, h, c: (b, h, 0, 0)),
        ],
        out_shape=[
            jax.ShapeDtypeStruct((B, T, MLSTM_WIDTH), BF16),
            jax.ShapeDtypeStruct((B, H, MLSTM_DQK, MLSTM_DV), F32),
            jax.ShapeDtypeStruct((B, H, 1, MLSTM_DQK), F32),
            jax.ShapeDtypeStruct((B, H, 1, 1), F32),
        ],
        scratch_shapes=[pltpu.VMEM((heads, MLSTM_DQK, MLSTM_DV), F32),
                        pltpu.VMEM((heads, 1, MLSTM_DQK), F32),
                        pltpu.VMEM((heads, 1, 1), F32)],
        compiler_params=pltpu.CompilerParams(
            dimension_semantics=("arbitrary", "arbitrary", "arbitrary"),
            vmem_limit_bytes=_vmem_limit(blocks, heads * _nbytes((MLSTM_DQK, MLSTM_DV), F32),
                                         heads * (8 * _nbytes((L, MLSTM_DV), F32) + 8 * _nbytes((L, L), F32)))),
        name="mlstm_prompt",
    )(z3, z3, z3, z3, li_row, fg_row, li_col, fg_col, mhg)


def _outproj_kernel(mix_ref, w_ref, x_ref, h_ref):
    h_ref[...] = x_ref[...] + jnp.dot(mix_ref[...], w_ref[...], preferred_element_type=F32)


def _out_proj(mix2d, w, x2d, *, tm, tn):
    M, D = x2d.shape
    K = w.shape[0]
    blocks = [_nbytes((tm, K), BF16), _nbytes((K, tn), BF16), 2 * _nbytes((tm, tn), F32)]
    return pl.pallas_call(
        _outproj_kernel,
        grid=(M // tm, D // tn),
        in_specs=[
            pl.BlockSpec((tm, K), lambda i, j: (i, 0)),
            pl.BlockSpec((K, tn), lambda i, j: (0, j)),
            pl.BlockSpec((tm, tn), lambda i, j: (i, j)),
        ],
        out_specs=pl.BlockSpec((tm, tn), lambda i, j: (i, j)),
        out_shape=jax.ShapeDtypeStruct((M, D), F32),
        compiler_params=pltpu.CompilerParams(
            dimension_semantics=("arbitrary", "arbitrary"),
            vmem_limit_bytes=_vmem_limit(blocks, 0, 2 * _nbytes((tm, tn), F32))),
        name="out_proj",
    )(mix2d, w, x2d)


def _upproj_kernel(x_ref, g_ref, w_ref, u_ref, xn_ref):
    @pl.when(pl.program_id(1) == 0)
    def _():
        x = x_ref[...]
        xn_ref[...] = (x * _rms_scale(x) * g_ref[...]).astype(BF16)

    acc = jnp.dot(xn_ref[...], w_ref[...], preferred_element_type=F32)
    u_ref[...] = jnp.square(jnp.maximum(acc, 0.0)).astype(u_ref.dtype)


def _up_proj(h2d, gain, w_up, *, tm, tn):
    M, D = h2d.shape
    N = w_up.shape[1]
    blocks = [_nbytes((tm, D), F32), _nbytes((D, tn), BF16), _nbytes((tm, tn), BF16)]
    return pl.pallas_call(
        _upproj_kernel,
        grid=(M // tm, N // tn),
        in_specs=[pl.BlockSpec((tm, D), lambda i, j: (i, 0)),
                  pl.BlockSpec((1, D), lambda i, j: (0, 0)),
                  pl.BlockSpec((D, tn), lambda i, j: (0, j))],
        out_specs=pl.BlockSpec((tm, tn), lambda i, j: (i, j)),
        out_shape=jax.ShapeDtypeStruct((M, N), BF16),
        scratch_shapes=[pltpu.VMEM((tm, D), BF16)],
        compiler_params=pltpu.CompilerParams(
            dimension_semantics=("arbitrary", "arbitrary"),
            vmem_limit_bytes=_vmem_limit(blocks, _nbytes((tm, D), BF16),
                                         _nbytes((tm, D), F32) + 2 * _nbytes((tm, tn), F32))),
        name="up_proj",
    )(h2d, gain, w_up)


def _downproj_kernel(u_ref, w_ref, h_ref, g_ref, y_ref):
    k = pl.program_id(1)

    @pl.when(k == 0)
    def _():
        y_ref[...] = h_ref[...]

    u = u_ref[...]
    for n in range(0, y_ref.shape[1], ACC_COLS):
        y_ref[:, n:n + ACC_COLS] += jnp.dot(u, w_ref[:, n:n + ACC_COLS], preferred_element_type=F32)

    @pl.when(k == pl.num_programs(1) - 1)
    def _():
        scale = _rms_scale(y_ref[...])
        y_ref[...] = y_ref[...] * scale * g_ref[...]


def _down_proj(u2d, w_down, h2d, gain, *, tm, tk):
    M, D = h2d.shape
    K = u2d.shape[1]
    blocks = [_nbytes((tm, tk), BF16), _nbytes((tk, D), BF16), 2 * _nbytes((tm, D), F32)]
    return pl.pallas_call(
        _downproj_kernel,
        grid=(M // tm, K // tk),
        in_specs=[pl.BlockSpec((tm, tk), lambda i, k: (i, k)),
                  pl.BlockSpec((tk, D), lambda i, k: (k, 0)),
                  pl.BlockSpec((tm, D), lambda i, k: (i, 0)),
                  pl.BlockSpec((1, D), lambda i, k: (0, 0))],
        out_specs=pl.BlockSpec((tm, D), lambda i, k: (i, 0)),
        out_shape=jax.ShapeDtypeStruct((M, D), F32),
        compiler_params=pltpu.CompilerParams(
            dimension_semantics=("arbitrary", "arbitrary"),
            vmem_limit_bytes=_vmem_limit(blocks, 0, _nbytes((tm, D), F32) + _nbytes((tm, ACC_COLS), F32))),
        name="down_proj",
    )(u2d, w_down, h2d, gain)


def _attn_sample_kernel(q_ref, kn_ref, vn_ref, k1_ref, k4_ref, k16_ref, v1_ref, v4_ref, v16_ref, o_ref):
    q = q_ref[...] * (HEAD_DIM ** -0.5)
    kn = kn_ref[...]
    vn = vn_ref[...]
    s0 = jnp.sum(q * kn, axis=-1, keepdims=True)

    ms, dens, nums = [], [], []
    for kt_ref, vt_ref in ((k1_ref, v1_ref), (k4_ref, v4_ref), (k16_ref, v16_ref)):
        kt = kt_ref[...]
        vt = vt_ref[...]
        s = jnp.sum(q[None] * kt, axis=-1, keepdims=True)
        m = jnp.maximum(jnp.max(s, axis=0), s0)
        p = jnp.exp(s - m[None])
        p0 = jnp.exp(s0 - m)
        dens.append(jnp.sum(p, axis=0) + p0)
        nums.append(jnp.sum(p * vt, axis=0) + p0 * vn)
        ms.append(m)
    m_all = jnp.maximum(jnp.maximum(ms[0], ms[1]), ms[2])
    num = jnp.zeros((ATT_HEADS, HEAD_DIM), F32)
    den = jnp.zeros((ATT_HEADS, 1), F32)
    for m, d, n in zip(ms, dens, nums):
        wgt = jnp.exp(m - m_all)
        num = num + wgt * n
        den = den + wgt * d
    o_ref[...] = (num / den).astype(o_ref.dtype)


def _attn_sample(qs, kns, vns, cache_k, cache_v):
    Bd, W, H, hd = cache_k.shape
    new_spec = pl.BlockSpec((None, H, hd), lambda b: (b, 0, 0))
    in_specs = [new_spec, new_spec, new_spec]
    views_k, views_v = [], []
    for (w, r) in DILATED_CONFIGS:
        assert w // r == Q_TILE and W % w == 0 and PAST_LEN >= W >= w
        views_k.append(cache_k.reshape(Bd, W // r, r, H, hd))
        views_v.append(cache_v.reshape(Bd, W // r, r, H, hd))
    for _ in range(2):
        for (w, r) in DILATED_CONFIGS:
            last = W // w - 1
            in_specs.append(pl.BlockSpec((None, Q_TILE, None, H, hd),
                                         lambda b, last=last: (b, last, 0, 0, 0)))
    blocks = [3 * _nbytes((H, hd), F32), 6 * _nbytes((Q_TILE, H, hd), F32)]
    return pl.pallas_call(
        _attn_sample_kernel,
        grid=(Bd,),
        in_specs=in_specs,
        out_specs=pl.BlockSpec((None, H, hd), lambda b: (b, 0, 0)),
        out_shape=jax.ShapeDtypeStruct((Bd, H, hd), BF16),
        compiler_params=pltpu.CompilerParams(
            dimension_semantics=("arbitrary",),
            vmem_limit_bytes=_vmem_limit(blocks, 0, 8 * _nbytes((Q_TILE, H, hd), F32))),
        name="attn_sample",
    )(qs, kns, vns, *views_k, *views_v)


def _split_bf16(a):
    hi = a.astype(BF16)
    return hi, (a - hi.astype(F32)).astype(BF16)


def _qk_sample_kernel(x_ref, g_ref, w_ref, o_ref, xhi_ref, xlo_ref):
    @pl.when(pl.program_id(0) == 0)
    def _():
        x = x_ref[...]
        xhi_ref[...], xlo_ref[...] = _split_bf16(x * _rms_scale(x) * g_ref[...])

    whi, wlo = _split_bf16(w_ref[...])
    xhi = xhi_ref[...]
    o_ref[...] = (lax.dot_general(xhi, whi, NT_DIMS, preferred_element_type=F32)
                  + lax.dot_general(xhi, wlo, NT_DIMS, preferred_element_type=F32)
                  + lax.dot_general(xlo_ref[...], whi, NT_DIMS, preferred_element_type=F32))


def _qk_sample(x2d, gain, w_t, *, tn=512):
    Bd, D = x2d.shape
    n_out = w_t.shape[0]
    blocks = [_nbytes((Bd, D), F32), _nbytes((tn, D), F32), _nbytes((Bd, tn), F32)]
    return pl.pallas_call(
        _qk_sample_kernel,
        grid=(n_out // tn,),
        in_specs=[pl.BlockSpec((Bd, D), lambda j: (0, 0)),
                  pl.BlockSpec((1, D), lambda j: (0, 0)),
                  pl.BlockSpec((tn, D), lambda j: (j, 0))],
        out_specs=pl.BlockSpec((Bd, tn), lambda j: (0, j)),
        out_shape=jax.ShapeDtypeStruct((Bd, n_out), F32),
        scratch_shapes=[pltpu.VMEM((Bd, D), BF16), pltpu.VMEM((Bd, D), BF16)],
        compiler_params=pltpu.CompilerParams(
            dimension_semantics=("arbitrary",),
            vmem_limit_bytes=_vmem_limit(blocks, _nbytes((Bd, D), F32), 4 * _nbytes((tn, D), BF16))),
        name="qk_sample",
    )(x2d, gain, w_t)


def _mlstm_sample_kernel(q_ref, k_ref, v_ref, om_ref, g_ref, c0_ref, n0_ref, m0_ref, mhg_ref,
                         hn_ref, c_out, n_out, m_out):
    H = MLSTM_HEADS
    for h in range(H):
        qf = q_ref[h:h + 1, :] * (MLSTM_DQK ** -0.5)
        kf = k_ref[h:h + 1, :]
        vf = v_ref[h:h + 1, :]
        li = g_ref[:, h:h + 1]
        lf = _log_sigmoid(g_ref[:, H + h:H + h + 1])
        m0 = m0_ref[:, h:h + 1]
        c0 = c0_ref[h]
        n0 = n0_ref[h:h + 1, :]

        inter = lf + m0
        m_t = jnp.maximum(inter, li)
        qk = jnp.sum(qf * kf, axis=1, keepdims=True)
        sw = qk * jnp.exp(li - m_t)
        a = jnp.exp(inter - m_t)
        qc = jnp.dot(jnp.broadcast_to(qf, (8, MLSTM_DQK)), c0, precision=lax.Precision.HIGHEST,
                     preferred_element_type=F32)[0:1, :]
        num = sw * vf + a * qc
        den = sw + a * jnp.sum(qf * n0, axis=1, keepdims=True)
        hh = num / jnp.maximum(jnp.abs(den), jnp.exp(-m_t))

        ws = jnp.exp(li - m_t)
        decay = jnp.exp(inter - m_t)
        kw = kf * ws
        kcol = jnp.broadcast_to(kw, (LANES, MLSTM_DQK)).T[:, 0:1]
        c_out[h] = decay * c0 + kcol * vf
        n_out[h:h + 1, :] = decay * n0 + kw
        m_out[:, h:h + 1] = m_t

        hn = hh * _rms_scale(hh) * mhg_ref[h:h + 1, :] * _sigmoid(om_ref[h:h + 1, :])
        hn_ref[h:h + 1, :] = hn.astype(hn_ref.dtype)


def _mlstm_sample(qs, ks, vs, oms, gs, c0, n0, m0, mhg):
    Bd = qs.shape[0]
    H = MLSTM_HEADS
    blocks = [2 * _nbytes((8, MLSTM_DQK), F32), 2 * _nbytes((8, MLSTM_DV), F32),
              2 * _nbytes((H, MLSTM_DQK, MLSTM_DV), F32), 4 * _nbytes((8, MLSTM_DV), F32)]
    return pl.pallas_call(
        _mlstm_sample_kernel,
        grid=(Bd,),
        in_specs=[
            pl.BlockSpec((None, H, MLSTM_DQK), lambda b: (b, 0, 0)),
            pl.BlockSpec((None, H, MLSTM_DQK), lambda b: (b, 0, 0)),
            pl.BlockSpec((None, H, MLSTM_DV), lambda b: (b, 0, 0)),
            pl.BlockSpec((None, H, MLSTM_DV), lambda b: (b, 0, 0)),
            pl.BlockSpec((None, 1, LANES), lambda b: (b, 0, 0)),
            pl.BlockSpec((None, H, MLSTM_DQK, MLSTM_DV), lambda b: (b, 0, 0, 0)),
            pl.BlockSpec((None, H, MLSTM_DQK), lambda b: (b, 0, 0)),
            pl.BlockSpec((None, 1, H), lambda b: (b, 0, 0)),
            pl.BlockSpec((H, MLSTM_DV), lambda b: (0, 0)),
        ],
        out_specs=[
            pl.BlockSpec((None, H, MLSTM_DV), lambda b: (b, 0, 0)),
            pl.BlockSpec((None, H, MLSTM_DQK, MLSTM_DV), lambda b: (b, 0, 0, 0)),
            pl.BlockSpec((None, H, MLSTM_DQK), lambda b: (b, 0, 0)),
            pl.BlockSpec((None, 1, H), lambda b: (b, 0, 0)),
        ],
        out_shape=[
            jax.ShapeDtypeStruct((Bd, H, MLSTM_DV), BF16),
            jax.ShapeDtypeStruct((Bd, H, MLSTM_DQK, MLSTM_DV), F32),
            jax.ShapeDtypeStruct((Bd, H, MLSTM_DQK), F32),
            jax.ShapeDtypeStruct((Bd, 1, H), F32),
        ],
        compiler_params=pltpu.CompilerParams(
            dimension_semantics=("arbitrary",),
            vmem_limit_bytes=_vmem_limit(blocks, 0, 4 * _nbytes((MLSTM_DQK, MLSTM_DV), F32))),
        name="mlstm_sample",
    )(qs, ks, vs, oms, gs, c0, n0, m0, mhg)


def _rope_tables(pos):
    half = HEAD_DIM // 2
    inv = ROPE_THETA ** (-jnp.arange(half, dtype=F32) / half)
    ang = pos.astype(F32)[:, None] * inv[None, :]
    cos = jnp.cos(ang)
    sin = jnp.sin(ang)
    return jnp.concatenate([cos, cos], axis=-1), jnp.concatenate([-sin, sin], axis=-1)


def _row_tile(M, cap):
    tm = min(M, cap)
    assert M % tm == 0
    return tm


def kernel(x_prompt, x_sample, cache_k, cache_v, state_C, state_n, state_m,
           norm1_g, w_in, b_if, mh_norm_g, w_out, norm2_g, w_up, w_down, final_g):
    B, T, D = x_prompt.shape
    Bd, Td, _ = x_sample.shape
    depth = w_in.shape[0]
    assert depth == 1 and Td == 1 and D == D_MODEL
    keep = min(WIN_MAX, T)
    H = MLSTM_HEADS

    w_main = jnp.transpose(w_in[0, :, :N_MAIN]).astype(BF16)
    w_gate = jnp.pad(jnp.transpose(w_in[0, :, N_MAIN:]), ((0, LANES - 2 * H), (0, 0))).astype(BF16)
    qk_lo = 3 * ATT_WIDTH
    w_qk_t = jnp.transpose(w_in[0, :, qk_lo:qk_lo + 2 * H * MLSTM_DQK])
    bias = jnp.pad(b_if[0], (0, LANES - 2 * H)).reshape(1, LANES)
    g1 =norm1_g[0].reshape(1, D)
    g2 = norm2_g[0].reshape(1, D)
    gf = final_g.reshape(1, D)
    mhg = mh_norm_g[0]

    cos_p, sin_p = _rope_tables(jnp.arange(T, dtype=jnp.int32))
    cos_s, sin_s = _rope_tables(jnp.full((Bd,), PAST_LEN, dtype=jnp.int32))

    xp2 = x_prompt.reshape(B * T, D)
    tm = _row_tile(T, 512)
    tm_w = _row_tile(T, 1024)
    xn_p, gates = _norm_rows(xp2, g1, w_gate, bias, tm=tm)
    z, wo, wu = _in_proj_prompt(xn_p, w_main, cos_p, sin_p, [w_out[0], w_up[0]],
                                tm=tm_w, tn=1024, pos_tiles=T // tm_w)
    z3 = z.reshape(B, T, N_MAIN)
    att = _attn_prompt(z3)

    g3 = gates.reshape(B, T, LANES)
    li_row = jnp.swapaxes(g3[:, :, :H], 1, 2).reshape(B, H, 1, T)
    fg_row = jnp.swapaxes(g3[:, :, H:2 * H], 1, 2).reshape(B, H, 1, T)
    li_col = li_row.reshape(B, H, T, 1)
    fg_col = fg_row.reshape(B, H, T, 1)
    hn, c_p, n_p, m_p = _mlstm_prompt(z3, li_row, fg_row, li_col, fg_col, mhg.reshape(1, MLSTM_WIDTH),
                                      L=min(T, 256), heads=MLSTM_HEADS_PER_STEP)

    h_p = _out_proj_prompt(att.reshape(B * T, ATT_WIDTH), hn.reshape(B * T, MLSTM_WIDTH), wo, xp2,
                           tm=tm_w, tn=1024)
    hn2_p = _norm_rows(h_p, g2, tm=tm)
    u_p, wd = _up_proj_prompt(hn2_p, wu, [w_down[0]], tm=tm_w, tn=1024)
    y_p = _down_proj(u_p, wd, h_p, gf, tm=tm, tk=1024)

    k_prompt = z3[:, T - keep:, ATT_WIDTH:2 * ATT_WIDTH].reshape(1, B, keep, ATT_HEADS, HEAD_DIM)
    v_prompt = z3[:, T - keep:, 2 * ATT_WIDTH:3 * ATT_WIDTH].reshape(1, B, keep, ATT_HEADS, HEAD_DIM)

    xs2 = x_sample.reshape(Bd, D)
    zs, gates_s = _in_proj(xs2, g1, w_main, w_gate, bias, cos_s, sin_s, tm=Bd, tn=1024, pos_tiles=1)
    qa_s = zs[:, :ATT_WIDTH].reshape(Bd, ATT_HEADS, HEAD_DIM)
    ka_s = zs[:, ATT_WIDTH:2 * ATT_WIDTH].reshape(Bd, ATT_HEADS, HEAD_DIM)
    va_s = zs[:, 2 * ATT_WIDTH:3 * ATT_WIDTH].reshape(Bd, ATT_HEADS, HEAD_DIM)
    att_s = _attn_sample(qa_s, ka_s, va_s, cache_k[0], cache_v[0])

    qk_s = _qk_sample(xs2, g1, w_qk_t)
    qs = qk_s[:, :H * MLSTM_DQK].reshape(Bd, H, MLSTM_DQK)
    ks = qk_s[:, H * MLSTM_DQK:].reshape(Bd, H, MLSTM_DQK)
    o = 3 * ATT_WIDTH + 2 * H * MLSTM_DQK
    vs = zs[:, o:o + MLSTM_WIDTH].reshape(Bd, H, MLSTM_DV)
    o += MLSTM_WIDTH
    oms = zs[:, o:o + MLSTM_WIDTH].reshape(Bd, H, MLSTM_DV)
    hn_s, c_s, n_s, m_s = _mlstm_sample(qs, ks, vs, oms, gates_s.reshape(Bd, 1, LANES),
                                        state_C[0], state_n[0], state_m[0].reshape(Bd, 1, H),
                                        mhg.reshape(H, MLSTM_DV))

    mix_s = jnp.concatenate([att_s.reshape(Bd, ATT_WIDTH), hn_s.reshape(Bd, MLSTM_WIDTH)], axis=-1)
    h_s = _out_proj(mix_s, wo, xs2, tm=Bd, tn=1024)
    u_s = _up_proj(h_s, g2, wu, tm=Bd, tn=1024)
    y_s = _down_proj(u_s, wd, h_s, gf, tm=Bd, tk=1024)

    k_sample = zs[:, ATT_WIDTH:2 * ATT_WIDTH].reshape(1, Bd, 1, ATT_HEADS, HEAD_DIM)
    v_sample = zs[:, 2 * ATT_WIDTH:3 * ATT_WIDTH].reshape(1, Bd, 1, ATT_HEADS, HEAD_DIM)

    return (y_p.reshape(B, T, D), y_s.reshape(Bd, 1, D),
            k_prompt, v_prompt,
            c_p[None], n_p.reshape(1, B, H, MLSTM_DQK), m_p.reshape(1, B, H),
            k_sample, v_sample,
            c_s[None], n_s[None], m_s.reshape(1, Bd, H))
```

```python
import functools

import jax
import jax.numpy as jnp
from jax import lax
from jax.experimental import pallas as pl
from jax.experimental.pallas import tpu as pltpu

F32 = jnp.float32
BF16 = jnp.bfloat16

D_MODEL = 4096
HEAD_DIM = 128
ATT_HEADS = 8
ATT_WIDTH = ATT_HEADS * HEAD_DIM
MLSTM_HEADS = 6
MLSTM_DQK = 256
MLSTM_DV = 512
MLSTM_WIDTH = MLSTM_HEADS * MLSTM_DV
N_MAIN = 3 * ATT_WIDTH + 2 * MLSTM_HEADS * MLSTM_DQK + 2 * MLSTM_WIDTH
GATE_CAP = 15.0
DILATED_CONFIGS = ((128, 1), (512, 4), (2048, 16))
WIN_MAX = 2048
ROPE_THETA = 10000.0
EPS = 1e-6
PAST_LEN = 8192
NEG_INF = float("-inf")

LANES = 128
V7X_VMEM_BYTES = 64 * 1024 * 1024
Q_TILE = 128
ACC_COLS = 512
ATT_GROUP = 8
NT_DIMS = (((1,), (1,)), ((), ()))
BF16_SUBLANES = 16
MLSTM_HEADS_PER_STEP = 6


def _vmem_limit(block_bytes, scratch_bytes, temp_bytes):
    need = 2 * sum(block_bytes) + scratch_bytes + temp_bytes
    return int(min(max(need, 16 * 1024 * 1024), V7X_VMEM_BYTES - 4 * 1024 * 1024))


def _nbytes(shape, dtype):
    n = 1
    for s in shape:
        n *= s
    return n * jnp.dtype(dtype).itemsize


def _log_sigmoid(x):
    return jnp.minimum(x, 0.0) - jnp.log1p(jnp.exp(-jnp.abs(x)))


def _sigmoid(x):
    return 1.0 / (1.0 + jnp.exp(-x))


def _rms_scale(x):
    return lax.rsqrt(jnp.mean(x * x, axis=-1, keepdims=True) + EPS)


def _norm_gate_kernel(x_ref, g_ref, wg_ref, bias_ref, xn_ref, gate_ref):
    xn = (x_ref[...] * _rms_scale(x_ref[...]) * g_ref[...]).astype(BF16)
    xn_ref[...] = xn
    pre = lax.dot_general(xn, wg_ref[...], NT_DIMS, preferred_element_type=F32) + bias_ref[...]
    gate_ref[...] = GATE_CAP * jnp.tanh(pre / GATE_CAP)


def _norm_kernel(x_ref, g_ref, xn_ref):
    xn_ref[...] = (x_ref[...] * _rms_scale(x_ref[...]) * g_ref[...]).astype(BF16)


def _norm_rows(x2d, gain, w_gate=None, bias=None, *, tm):
    M, D = x2d.shape
    with_gates = w_gate is not None
    in_specs = [pl.BlockSpec((tm, D), lambda i: (i, 0)), pl.BlockSpec((1, D), lambda i: (0, 0))]
    out_specs = [pl.BlockSpec((tm, D), lambda i: (i, 0))]
    out_shape = [jax.ShapeDtypeStruct((M, D), BF16)]
    args = [x2d, gain]
    if with_gates:
        in_specs += [pl.BlockSpec((LANES, D), lambda i: (0, 0)), pl.BlockSpec((1, LANES), lambda i: (0, 0))]
        out_specs.append(pl.BlockSpec((tm, LANES), lambda i: (i, 0)))
        out_shape.append(jax.ShapeDtypeStruct((M, LANES), F32))
        args += [w_gate, bias]
    blocks = [_nbytes((tm, D), F32), _nbytes((tm, D), BF16), _nbytes((LANES, D), BF16), _nbytes((tm, LANES), F32)]
    out = pl.pallas_call(
        _norm_gate_kernel if with_gates else _norm_kernel,
        grid=(M // tm,),
        in_specs=in_specs,
        out_specs=out_specs,
        out_shape=out_shape,
        compiler_params=pltpu.CompilerParams(
            dimension_semantics=("arbitrary",),
            vmem_limit_bytes=_vmem_limit(blocks, 0, _nbytes((tm, D), F32))),
        name="norm_gate" if with_gates else "norm",
    )(*args)
    return out if with_gates else out[0]


def _cast_kernel(w_ref, o_ref):
    o_ref[...] = w_ref[...].astype(o_ref.dtype)


def _cast_rows_bf16(w, n_rows, *, tr=1024, tc=1024):
    _, C = w.shape
    assert n_rows % tr == 0 and C % tc == 0
    return pl.pallas_call(
        _cast_kernel,
        grid=(n_rows // tr, C // tc),
        in_specs=[pl.BlockSpec((tr, tc), lambda i, j: (i, j))],
        out_specs=pl.BlockSpec((tr, tc), lambda i, j: (i, j)),
        out_shape=jax.ShapeDtypeStruct((n_rows, C), BF16),
        compiler_params=pltpu.CompilerParams(
            dimension_semantics=("arbitrary", "arbitrary"),
            vmem_limit_bytes=_vmem_limit([_nbytes((tr, tc), F32), _nbytes((tr, tc), BF16)], 0, 0)),
        name="cast_bf16",
    )(w)


def _store_with_rope(product, z_ref, cos_ref, sin_ref, j, n_rope_tiles, tn):
    @pl.when(j < n_rope_tiles)
    def _():
        acc = product()
        c = cos_ref[...]
        s = sin_ref[...]
        for t in range(tn // HEAD_DIM):
            a = acc[:, t * HEAD_DIM:(t + 1) * HEAD_DIM]
            z_ref[:, t * HEAD_DIM:(t + 1) * HEAD_DIM] = a * c + pltpu.roll(a, HEAD_DIM // 2, 1) * s

    @pl.when(j >= n_rope_tiles)
    def _():
        z_ref[...] = product()


class _SideCast:
    def __init__(self, weights, n_steps, n_inner):
        rows = weights[0].shape[0]
        assert all(w.shape[0] == rows for w in weights)
        self.weights = weights
        self.n_inner = n_inner
        self.n_blocks = min(1 << (n_steps.bit_length() - 1), rows // BF16_SUBLANES)
        self.rows = rows // self.n_blocks

    def _index(self, a, b):
        return (jnp.minimum(a * self.n_inner + b, self.n_blocks - 1), 0)

    def specs(self):
        return [pl.BlockSpec((self.rows, w.shape[1]), self._index) for w in self.weights]

    def out_shapes(self):
        return [jax.ShapeDtypeStruct(w.shape, BF16) for w in self.weights]

    def block_bytes(self):
        return [_nbytes((self.rows, w.shape[1]), F32) + _nbytes((self.rows, w.shape[1]), BF16)
                for w in self.weights]

    def run(self, src_refs, dst_refs):
        step = pl.program_id(0) * self.n_inner + pl.program_id(1)

        @pl.when(step < self.n_blocks)
        def _():
            for src, dst in zip(src_refs, dst_refs):
                dst[...] = src[...].astype(dst.dtype)


def _inproj_prompt_kernel(xn_ref, w_ref, cos_ref, sin_ref, *refs, n_rope_tiles, tn, side):
    n_side = len(side.weights)
    side_in, z_ref, side_out = refs[:n_side], refs[n_side], refs[n_side + 1:]

    def product():
        return lax.dot_general(xn_ref[...], w_ref[...], NT_DIMS, preferred_element_type=F32)

    _store_with_rope(product, z_ref, cos_ref, sin_ref, pl.program_id(1), n_rope_tiles, tn)
    side.run(side_in, side_out)


def _in_proj_prompt(xn, w_t, cosf, sinf, side_weights, *, tm, tn, pos_tiles):
    M, D = xn.shape
    N = w_t.shape[0]
    grid = (M // tm, N // tn)
    side = _SideCast(side_weights, grid[0] * grid[1], grid[1])
    blocks = [_nbytes((tm, D), BF16) // 2, _nbytes((tn, D), BF16), 2 * _nbytes((tm, LANES), F32),
              _nbytes((tm, tn), F32)] + side.block_bytes()
    return pl.pallas_call(
        functools.partial(_inproj_prompt_kernel, n_rope_tiles=2 * ATT_WIDTH // tn, tn=tn, side=side),
        grid=grid,
        in_specs=[
            pl.BlockSpec((tm, D), lambda i, j: (i, 0), pipeline_mode=pl.Buffered(1)),
            pl.BlockSpec((tn, D), lambda i, j: (j, 0)),
            pl.BlockSpec((tm, LANES), lambda i, j: (i % pos_tiles, 0)),
            pl.BlockSpec((tm, LANES), lambda i, j: (i % pos_tiles, 0)),
        ] + side.specs(),
        out_specs=[pl.BlockSpec((tm, tn), lambda i, j: (i, j))] + side.specs(),
        out_shape=[jax.ShapeDtypeStruct((M, N), F32)] + side.out_shapes(),
        compiler_params=pltpu.CompilerParams(
            dimension_semantics=("arbitrary", "arbitrary"),
            vmem_limit_bytes=_vmem_limit(blocks, 0, 3 * _nbytes((tm, tn), F32))),
        name="in_proj_prompt",
    )(xn, w_t, cosf, sinf, *side_weights)


def _upproj_prompt_kernel(xn_ref, w_ref, *refs, side):
    n_side = len(side.weights)
    side_in, u_ref, side_out = refs[:n_side], refs[n_side], refs[n_side + 1:]
    acc = jnp.dot(xn_ref[...], w_ref[...], preferred_element_type=F32)
    u_ref[...] = jnp.square(jnp.maximum(acc, 0.0)).astype(u_ref.dtype)
    side.run(side_in, side_out)


def _up_proj_prompt(xn, w_up, side_weights, *, tm, tn):
    M, D = xn.shape
    N = w_up.shape[1]
    grid = (M // tm, N // tn)
    side = _SideCast(side_weights, grid[0] * grid[1], grid[1])
    blocks = [_nbytes((tm, D), BF16), _nbytes((D, tn), BF16), _nbytes((tm, tn), BF16)] + side.block_bytes()
    return pl.pallas_call(
        functools.partial(_upproj_prompt_kernel, side=side),
        grid=grid,
        in_specs=[pl.BlockSpec((tm, D), lambda i, j: (i, 0)),
                  pl.BlockSpec((D, tn), lambda i, j: (0, j))] + side.specs(),
        out_specs=[pl.BlockSpec((tm, tn), lambda i, j: (i, j))] + side.specs(),
        out_shape=[jax.ShapeDtypeStruct((M, N), BF16)] + side.out_shapes(),
        compiler_params=pltpu.CompilerParams(
            dimension_semantics=("arbitrary", "arbitrary"),
            vmem_limit_bytes=_vmem_limit(blocks, 0, 3 * _nbytes((tm, tn), F32))),
        name="up_proj_prompt",
    )(xn, w_up, *side_weights)


def _outproj_prompt_kernel(att_ref, hn_ref, w_ref, x_ref, h_ref):
    h_ref[...] = (x_ref[...]
                  + jnp.dot(att_ref[...], w_ref[:ATT_WIDTH, :], preferred_element_type=F32)
                  + jnp.dot(hn_ref[...], w_ref[ATT_WIDTH:, :], preferred_element_type=F32))


def _out_proj_prompt(att2d, hn2d, w, x2d, *, tm, tn):
    M, D = x2d.shape
    K = w.shape[0]
    blocks = [_nbytes((tm, K), BF16), _nbytes((K, tn), BF16), 2 * _nbytes((tm, tn), F32)]
    return pl.pallas_call(
        _outproj_prompt_kernel,
        grid=(M // tm, D // tn),
        in_specs=[pl.BlockSpec((tm, ATT_WIDTH), lambda i, j: (i, 0)),
                  pl.BlockSpec((tm, MLSTM_WIDTH), lambda i, j: (i, 0)),
                  pl.BlockSpec((K, tn), lambda i, j: (0, j)),
                  pl.BlockSpec((tm, tn), lambda i, j: (i, j))],
        out_specs=pl.BlockSpec((tm, tn), lambda i, j: (i, j)),
        out_shape=jax.ShapeDtypeStruct((M, D), F32),
        compiler_params=pltpu.CompilerParams(
            dimension_semantics=("arbitrary", "arbitrary"),
            vmem_limit_bytes=_vmem_limit(blocks, 0, 3 * _nbytes((tm, tn), F32))),
        name="out_proj_prompt",
    )(att2d, hn2d, w, x2d)


def _inproj_kernel(x_ref, g_ref, w_ref, wg_ref, bias_ref, cos_ref, sin_ref,
                   z_ref, gate_ref, xn_ref, *, n_rope_tiles, tn):
    j = pl.program_id(1)

    @pl.when(j == 0)
    def _():
        x = x_ref[...]
        xn = (x * _rms_scale(x) * g_ref[...]).astype(BF16)
        xn_ref[...] = xn
        pre = lax.dot_general(xn, wg_ref[...], NT_DIMS, preferred_element_type=F32) + bias_ref[...]
        gate_ref[...] = GATE_CAP * jnp.tanh(pre / GATE_CAP)

    def product():
        return lax.dot_general(xn_ref[...], w_ref[...], NT_DIMS, preferred_element_type=F32)

    _store_with_rope(product, z_ref, cos_ref, sin_ref, j, n_rope_tiles, tn)


def _in_proj(x2d, gain, w_main, w_gate, bias, cosf, sinf, *, tm, tn, pos_tiles):
    M, D = x2d.shape
    N = w_main.shape[0]
    blocks = [_nbytes((tm, D), F32), _nbytes((tn, D), BF16), _nbytes((LANES, D), BF16),
              2 * _nbytes((tm, LANES), F32), _nbytes((tm, tn), F32), _nbytes((tm, LANES), F32)]
    return pl.pallas_call(
        functools.partial(_inproj_kernel, n_rope_tiles=2 * ATT_WIDTH // tn, tn=tn),
        grid=(M // tm, N // tn),
        in_specs=[
            pl.BlockSpec((tm, D), lambda i, j: (i, 0)),
            pl.BlockSpec((1, D), lambda i, j: (0, 0)),
            pl.BlockSpec((tn, D), lambda i, j: (j, 0)),
            pl.BlockSpec((LANES, D), lambda i, j: (0, 0)),
            pl.BlockSpec((1, LANES), lambda i, j: (0, 0)),
            pl.BlockSpec((tm, LANES), lambda i, j: (i % pos_tiles, 0)),
            pl.BlockSpec((tm, LANES), lambda i, j: (i % pos_tiles, 0)),
        ],
        out_specs=[pl.BlockSpec((tm, tn), lambda i, j: (i, j)),
                   pl.BlockSpec((tm, LANES), lambda i, j: (i, 0))],
        out_shape=[jax.ShapeDtypeStruct((M, N), F32), jax.ShapeDtypeStruct((M, LANES), F32)],
        scratch_shapes=[pltpu.VMEM((tm, D), BF16)],
        compiler_params=pltpu.CompilerParams(
            dimension_semantics=("arbitrary", "arbitrary"),
            vmem_limit_bytes=_vmem_limit(blocks, _nbytes((tm, D), BF16),
                                         _nbytes((tm, D), F32) + 2 * _nbytes((tm, tn), F32))),
        name="in_proj",
    )(x2d, gain, w_main, w_gate, bias, cosf, sinf)


def _attn_prompt_kernel(q_ref, k_ref, v_ref, o_ref, m_s, l_s, acc_s, *, T):
    scale = HEAD_DIM ** -0.5
    G = ATT_GROUP
    row = lax.broadcasted_iota(jnp.int32, (G * Q_TILE, 2 * Q_TILE), 0) & (Q_TILE - 1)
    col = lax.broadcasted_iota(jnp.int32, (G * Q_TILE, 2 * Q_TILE), 1)
    band = jnp.logical_and(col >= row, col - Q_TILE <= row)
    prev_cols = lax.broadcasted_iota(jnp.int32, (Q_TILE, 2 * Q_TILE), 1) < Q_TILE
    nt = (((1,), (1,)), ((), ()))

    for ci, (_, r) in enumerate(DILATED_CONFIGS):
        tiles_per_class = T // (r * Q_TILE)
        run = min(G, tiles_per_class)
        runs_per_group = G // run
        runs_per_class = tiles_per_class // run
        shift = runs_per_class.bit_length() - 1
        n_groups = T // (Q_TILE * G)

        def rows(ref, start, n, r=r):
            if r == 1:
                return ref[pl.ds(start, n), :]
            return ref[pl.ds(start, n, stride=r), :]

        def put(ref, start, n, val, r=r):
            if r == 1:
                ref[pl.ds(start, n), :] = val
            else:
                ref[pl.ds(start, n, stride=r), :] = val

        def body(gi, carry, r=r, ci=ci, run=run, rpg=runs_per_group, rpc=runs_per_class,
                 shift=shift, rows=rows, put=put):
            n = run * Q_TILE
            starts, vxs, scores = [], [], []
            for j in range(rpg):
                ridx = gi * rpg + j
                cls = lax.shift_right_logical(ridx, shift)
                u0 = (ridx & (rpc - 1)) * n
                start = cls + r * u0
                pstart = cls + r * jnp.maximum(u0 - Q_TILE, 0)
                no_prev = u0 == 0
                q = (rows(q_ref, start, n) * scale).astype(BF16)
                kx = jnp.concatenate([rows(k_ref, pstart, Q_TILE), rows(k_ref, start, n)], axis=0).astype(BF16)
                vx = jnp.concatenate([rows(v_ref, pstart, Q_TILE), rows(v_ref, start, n)], axis=0).astype(BF16)
                for g in range(run):
                    s = lax.dot_general(q[g * Q_TILE:(g + 1) * Q_TILE], kx[g * Q_TILE:(g + 2) * Q_TILE], nt,
                                        preferred_element_type=F32)
                    if g == 0:
                        s = jnp.where(jnp.logical_and(prev_cols, no_prev), NEG_INF, s)
                    scores.append(s)
                starts.append(start)
                vxs.append(vx)
            s_all = jnp.where(band, jnp.concatenate(scores, axis=0), NEG_INF)
            mt = jnp.max(s_all, axis=1, keepdims=True)
            if ci == 0:
                m_new = jnp.broadcast_to(mt, (G * Q_TILE, LANES))
            else:
                m_old = jnp.concatenate([rows(m_s, st, n) for st in starts], axis=0)
                m_new = jnp.maximum(m_old, mt)
                alpha = jnp.exp(m_old - m_new)
            p = jnp.exp(s_all - jnp.concatenate([m_new, m_new], axis=1))
            lt = jnp.sum(p, axis=1, keepdims=True)
            pb = p.astype(BF16)
            outs = []
            for j in range(rpg):
                for g in range(run):
                    t = j * run + g
                    outs.append(jnp.dot(pb[t * Q_TILE:(t + 1) * Q_TILE], vxs[j][g * Q_TILE:(g + 2) * Q_TILE],
                                        preferred_element_type=F32))
            ot = jnp.concatenate(outs, axis=0)
            if ci == 0:
                l_new = jnp.broadcast_to(lt, (G * Q_TILE, LANES))
            else:
                l_new = alpha * jnp.concatenate([rows(l_s, st, n) for st in starts], axis=0) + lt
                ot = alpha * jnp.concatenate([rows(acc_s, st, n) for st in starts], axis=0) + ot
            for j, st in enumerate(starts):
                put(m_s, st, n, m_new[j * n:(j + 1) * n])
                put(l_s, st, n, l_new[j * n:(j + 1) * n])
                put(acc_s, st, n, ot[j * n:(j + 1) * n])
            return carry

        lax.fori_loop(0, n_groups, body, 0)

    o_ref[...] = (acc_s[...] / l_s[...]).astype(o_ref.dtype)


def _attn_prompt(z3):
    B, T, _ = z3.shape
    assert T % (DILATED_CONFIGS[-1][1] * Q_TILE) == 0
    blk = (None, T, HEAD_DIM)
    blocks = [3 * _nbytes((T, HEAD_DIM), F32), _nbytes((T, HEAD_DIM), BF16)]
    return pl.pallas_call(
        functools.partial(_attn_prompt_kernel, T=T),
        grid=(B, ATT_HEADS),
        in_specs=[pl.BlockSpec(blk, lambda b, h: (b, 0, h)),
                  pl.BlockSpec(blk, lambda b, h: (b, 0, ATT_HEADS + h)),
                  pl.BlockSpec(blk, lambda b, h: (b, 0, 2 * ATT_HEADS + h))],
        out_specs=pl.BlockSpec(blk, lambda b, h: (b, 0, h)),
        out_shape=jax.ShapeDtypeStruct((B, T, ATT_WIDTH), BF16),
        scratch_shapes=[pltpu.VMEM((T, LANES), F32)] * 3,
        compiler_params=pltpu.CompilerParams(
            dimension_semantics=("arbitrary", "arbitrary"),
            vmem_limit_bytes=_vmem_limit(blocks, 3 * _nbytes((T, LANES), F32),
                                         2 * _nbytes((T, LANES), F32))),
        name="attn_prompt",
    )(z3, z3, z3)


def _mlstm_prompt_kernel(q_ref, k_ref, v_ref, om_ref, lir_ref, fgr_ref, lic_ref, fgc_ref, mhg_ref,
                         hn_ref, c_out, n_out, m_out, c_s, n_s, m_s, *, L, heads):
    c = pl.program_id(2)

    @pl.when(c == 0)
    def _():
        c_s[...] = jnp.zeros_like(c_s)
        n_s[...] = jnp.zeros_like(n_s)
        m_s[...] = jnp.zeros_like(m_s)

    r_idx = lax.broadcasted_iota(jnp.int32, (L, L), 0)
    c_idx = lax.broadcasted_iota(jnp.int32, (L, L), 1)
    causal = c_idx <= r_idx

    for hh in range(heads):
        qs = slice(hh * MLSTM_DQK, (hh + 1) * MLSTM_DQK)
        vs = slice(hh * MLSTM_DV, (hh + 1) * MLSTM_DV)
        li_row = lir_ref[hh]
        li_col = lic_ref[hh]
        lf_row = _log_sigmoid(fgr_ref[hh])
        lf_col = _log_sigmoid(fgc_ref[hh])
        b_col = jnp.sum(jnp.where(causal, lf_row, 0.0), axis=1, keepdims=True)
        b_row = jnp.sum(jnp.where(r_idx <= c_idx, lf_col, 0.0), axis=0, keepdims=True)

        m_prev = m_s[hh]
        dlog = jnp.where(causal, b_col - b_row + li_row, NEG_INF)
        inter = b_col + m_prev
        m_t = jnp.maximum(inter, jnp.max(dlog, axis=1, keepdims=True))
        dmat = jnp.exp(dlog - m_t)
        a = jnp.exp(inter - m_t)

        qf = q_ref[:, qs] * (MLSTM_DQK ** -0.5)
        qb = qf.astype(BF16)
        kf = k_ref[:, qs]
        kb = kf.astype(BF16)
        vb = v_ref[:, vs].astype(BF16)
        s = lax.dot_general(qb, kb, NT_DIMS, preferred_element_type=F32)
        sw = s * dmat
        c_prev = c_s[hh]
        num = (jnp.dot(sw.astype(BF16), vb, preferred_element_type=F32)
               + a * jnp.dot(qb, c_prev.astype(BF16), preferred_element_type=F32))
        den = (jnp.sum(sw, axis=1, keepdims=True)
               + a * jnp.sum(qf * n_s[hh], axis=1, keepdims=True))
        h = num / jnp.maximum(jnp.abs(den), jnp.exp(-m_t))

        m_new = m_t[L - 1:L, :]
        b_last = b_col[L - 1:L, :]
        ws = jnp.exp(b_last - b_col + li_col - m_new)
        decay = jnp.exp(b_last + m_prev - m_new)
        kw = kf * ws
        kwt = kw.T.astype(BF16)
        c_s[hh] = decay * c_prev + jnp.dot(kwt, vb, preferred_element_type=F32)
        n_s[hh] = decay * n_s[hh] + jnp.sum(kw, axis=0, keepdims=True)
        m_s[hh] = m_new

        hn = h * _rms_scale(h) * mhg_ref[:, vs] * _sigmoid(om_ref[:, vs])
        hn_ref[:, vs] = hn.astype(hn_ref.dtype)

    @pl.when(c == pl.num_programs(2) - 1)
    def _():
        c_out[...] = c_s[...]
        n_out[...] = n_s[...]
        m_out[...] = m_s[...]


def _mlstm_prompt(z3, li_row, fg_row, li_col, fg_col, mhg, *, L, heads):
    B, T, _ = z3.shape
    H = MLSTM_HEADS
    assert H % heads == 0
    wq, wv = heads * MLSTM_DQK, heads * MLSTM_DV
    q_off = 3 * ATT_WIDTH // wq
    k_off = q_off + H // heads
    v_off = (3 * ATT_WIDTH + 2 * H * MLSTM_DQK) // wv
    o_off = v_off + H // heads
    blocks = [2 * _nbytes((L, wq), F32), 2 * _nbytes((L, wv), F32),
              4 * heads * _nbytes((L, LANES), F32), _nbytes((L, wv), BF16),
              heads * _nbytes((MLSTM_DQK, MLSTM_DV), F32)]
    return pl.pallas_call(
        functools.partial(_mlstm_prompt_kernel, L=L, heads=heads),
        grid=(B, H // heads, T // L),
        in_specs=[
            pl.BlockSpec((None, L, wq), lambda b, h, c: (b, c, q_off + h)),
            pl.BlockSpec((None, L, wq), lambda b, h, c: (b, c, k_off + h)),
            pl.BlockSpec((None, L, wv), lambda b, h, c: (b, c, v_off + h)),
            pl.BlockSpec((None, L, wv), lambda b, h, c: (b, c, o_off + h)),
            pl.BlockSpec((None, heads, 1, L), lambda b, h, c: (b, h, 0, c)),
            pl.BlockSpec((None, heads, 1, L), lambda b, h, c: (b, h, 0, c)),
            pl.BlockSpec((None, heads, L, 1), lambda b, h, c: (b, h, c, 0)),
            pl.BlockSpec((None, heads, L, 1), lambda b, h, c: (b, h, c, 0)),
            pl.BlockSpec((1, wv), lambda b, h, c: (0, h)),
        ],
        out_specs=[
            pl.BlockSpec((None, L, wv), lambda b, h, c: (b, c, h)),
            pl.BlockSpec((None, heads, MLSTM_DQK, MLSTM_DV), lambda b, h, c: (b, h, 0, 0)),
            pl.BlockSpec((None, heads, 1, MLSTM_DQK), lambda b, h, c: (b, h, 0, 0)),
            pl.BlockSpec((None, heads, 1, 1), lambda b, h, c: (b, h, 0, 0)),
        ],
        out_shape=[
            jax.ShapeDtypeStruct((B, T, MLSTM_WIDTH), BF16),
            jax.ShapeDtypeStruct((B, H, MLSTM_DQK, MLSTM_DV), F32),
            jax.ShapeDtypeStruct((B, H, 1, MLSTM_DQK), F32),
            jax.ShapeDtypeStruct((B, H, 1, 1), F32),
        ],
        scratch_shapes=[pltpu.VMEM((heads, MLSTM_DQK, MLSTM_DV), F32),
                        pltpu.VMEM((heads, 1, MLSTM_DQK), F32),
                        pltpu.VMEM((heads, 1, 1), F32)],
        compiler_params=pltpu.CompilerParams(
            dimension_semantics=("arbitrary", "arbitrary", "arbitrary"),
            vmem_limit_bytes=_vmem_limit(blocks, heads * _nbytes((MLSTM_DQK, MLSTM_DV), F32),
                                         heads * (8 * _nbytes((L, MLSTM_DV), F32) + 8 * _nbytes((L, L), F32)))),
        name="mlstm_prompt",
    )(z3, z3, z3, z3, li_row, fg_row, li_col, fg_col, mhg)


def _outproj_kernel(mix_ref, w_ref, x_ref, h_ref):
    h_ref[...] = x_ref[...] + jnp.dot(mix_ref[...], w_ref[...], preferred_element_type=F32)


def _out_proj(mix2d, w, x2d, *, tm, tn):
    M, D = x2d.shape
    K = w.shape[0]
    blocks = [_nbytes((tm, K), BF16), _nbytes((K, tn), BF16), 2 * _nbytes((tm, tn), F32)]
    return pl.pallas_call(
        _outproj_kernel,
        grid=(M // tm, D // tn),
        in_specs=[
            pl.BlockSpec((tm, K), lambda i, j: (i, 0)),
            pl.BlockSpec((K, tn), lambda i, j: (0, j)),
            pl.BlockSpec((tm, tn), lambda i, j: (i, j)),
        ],
        out_specs=pl.BlockSpec((tm, tn), lambda i, j: (i, j)),
        out_shape=jax.ShapeDtypeStruct((M, D), F32),
        compiler_params=pltpu.CompilerParams(
            dimension_semantics=("arbitrary", "arbitrary"),
            vmem_limit_bytes=_vmem_limit(blocks, 0, 2 * _nbytes((tm, tn), F32))),
        name="out_proj",
    )(mix2d, w, x2d)


def _upproj_kernel(x_ref, g_ref, w_ref, u_ref, xn_ref):
    @pl.when(pl.program_id(1) == 0)
    def _():
        x = x_ref[...]
        xn_ref[...] = (x * _rms_scale(x) * g_ref[...]).astype(BF16)

    acc = jnp.dot(xn_ref[...], w_ref[...], preferred_element_type=F32)
    u_ref[...] = jnp.square(jnp.maximum(acc, 0.0)).astype(u_ref.dtype)


def _up_proj(h2d, gain, w_up, *, tm, tn):
    M, D = h2d.shape
    N = w_up.shape[1]
    blocks = [_nbytes((tm, D), F32), _nbytes((D, tn), BF16), _nbytes((tm, tn), BF16)]
    return pl.pallas_call(
        _upproj_kernel,
        grid=(M // tm, N // tn),
        in_specs=[pl.BlockSpec((tm, D), lambda i, j: (i, 0)),
                  pl.BlockSpec((1, D), lambda i, j: (0, 0)),
                  pl.BlockSpec((D, tn), lambda i, j: (0, j))],
        out_specs=pl.BlockSpec((tm, tn), lambda i, j: (i, j)),
        out_shape=jax.ShapeDtypeStruct((M, N), BF16),
        scratch_shapes=[pltpu.VMEM((tm, D), BF16)],
        compiler_params=pltpu.CompilerParams(
            dimension_semantics=("arbitrary", "arbitrary"),
            vmem_limit_bytes=_vmem_limit(blocks, _nbytes((tm, D), BF16),
                                         _nbytes((tm, D), F32) + 2 * _nbytes((tm, tn), F32))),
        name="up_proj",
    )(h2d, gain, w_up)


def _downproj_kernel(u_ref, w_ref, h_ref, g_ref, y_ref):
    k = pl.program_id(1)

    @pl.when(k == 0)
    def _():
        y_ref[...] = h_ref[...]

    u = u_ref[...]
    for n in range(0, y_ref.shape[1], ACC_COLS):
        y_ref[:, n:n + ACC_COLS] += jnp.dot(u, w_ref[:, n:n + ACC_COLS], preferred_element_type=F32)

    @pl.when(k == pl.num_programs(1) - 1)
    def _():
        scale = _rms_scale(y_ref[...])
        y_ref[...] = y_ref[...] * scale * g_ref[...]


def _down_proj(u2d, w_down, h2d, gain, *, tm, tk):
    M, D = h2d.shape
    K = u2d.shape[1]
    blocks = [_nbytes((tm, tk), BF16), _nbytes((tk, D), BF16), 2 * _nbytes((tm, D), F32)]
    return pl.pallas_call(
        _downproj_kernel,
        grid=(M // tm, K // tk),
        in_specs=[pl.BlockSpec((tm, tk), lambda i, k: (i, k)),
                  pl.BlockSpec((tk, D), lambda i, k: (k, 0)),
                  pl.BlockSpec((tm, D), lambda i, k: (i, 0)),
                  pl.BlockSpec((1, D), lambda i, k: (0, 0))],
        out_specs=pl.BlockSpec((tm, D), lambda i, k: (i, 0)),
        out_shape=jax.ShapeDtypeStruct((M, D), F32),
        compiler_params=pltpu.CompilerParams(
            dimension_semantics=("arbitrary", "arbitrary"),
            vmem_limit_bytes=_vmem_limit(blocks, 0, _nbytes((tm, D), F32) + _nbytes((tm, ACC_COLS), F32))),
        name="down_proj",
    )(u2d, w_down, h2d, gain)


def _attn_sample_kernel(q_ref, kn_ref, vn_ref, k1_ref, k4_ref, k16_ref, v1_ref, v4_ref, v16_ref, o_ref):
    q = q_ref[...] * (HEAD_DIM ** -0.5)
    kn = kn_ref[...]
    vn = vn_ref[...]
    s0 = jnp.sum(q * kn, axis=-1, keepdims=True)

    ms, dens, nums = [], [], []
    for kt_ref, vt_ref in ((k1_ref, v1_ref), (k4_ref, v4_ref), (k16_ref, v16_ref)):
        kt = kt_ref[...]
        vt = vt_ref[...]
        s = jnp.sum(q[None] * kt, axis=-1, keepdims=True)
        m = jnp.maximum(jnp.max(s, axis=0), s0)
        p = jnp.exp(s - m[None])
        p0 = jnp.exp(s0 - m)
        dens.append(jnp.sum(p, axis=0) + p0)
        nums.append(jnp.sum(p * vt, axis=0) + p0 * vn)
        ms.append(m)
    m_all = jnp.maximum(jnp.maximum(ms[0], ms[1]), ms[2])
    num = jnp.zeros((ATT_HEADS, HEAD_DIM), F32)
    den = jnp.zeros((ATT_HEADS, 1), F32)
    for m, d, n in zip(ms, dens, nums):
        wgt = jnp.exp(m - m_all)
        num = num + wgt * n
        den = den + wgt * d
    o_ref[...] = (num / den).astype(o_ref.dtype)


def _attn_sample(qs, kns, vns, cache_k, cache_v):
    Bd, W, H, hd = cache_k.shape
    new_spec = pl.BlockSpec((None, H, hd), lambda b: (b, 0, 0))
    in_specs = [new_spec, new_spec, new_spec]
    views_k, views_v = [], []
    for (w, r) in DILATED_CONFIGS:
        assert w // r == Q_TILE and W % w == 0 and PAST_LEN >= W >= w
        views_k.append(cache_k.reshape(Bd, W // r, r, H, hd))
        views_v.append(cache_v.reshape(Bd, W // r, r, H, hd))
    for _ in range(2):
        for (w, r) in DILATED_CONFIGS:
            last = W // w - 1
            in_specs.append(pl.BlockSpec((None, Q_TILE, None, H, hd),
                                         lambda b, last=last: (b, last, 0, 0, 0)))
    blocks = [3 * _nbytes((H, hd), F32), 6 * _nbytes((Q_TILE, H, hd), F32)]
    return pl.pallas_call(
        _attn_sample_kernel,
        grid=(Bd,),
        in_specs=in_specs,
        out_specs=pl.BlockSpec((None, H, hd), lambda b: (b, 0, 0)),
        out_shape=jax.ShapeDtypeStruct((Bd, H, hd), BF16),
        compiler_params=pltpu.CompilerParams(
            dimension_semantics=("arbitrary",),
            vmem_limit_bytes=_vmem_limit(blocks, 0, 8 * _nbytes((Q_TILE, H, hd), F32))),
        name="attn_sample",
    )(qs, kns, vns, *views_k, *views_v)


def _split_bf16(a):
    hi = a.astype(BF16)
    return hi, (a - hi.astype(F32)).astype(BF16)


def _qk_sample_kernel(x_ref, g_ref, w_ref, o_ref, xhi_ref, xlo_ref):
    @pl.when(pl.program_id(0) == 0)
    def _():
        x = x_ref[...]
        xhi_ref[...], xlo_ref[...] = _split_bf16(x * _rms_scale(x) * g_ref[...])

    whi, wlo = _split_bf16(w_ref[...])
    xhi = xhi_ref[...]
    o_ref[...] = (lax.dot_general(xhi, whi, NT_DIMS, preferred_element_type=F32)
                  + lax.dot_general(xhi, wlo, NT_DIMS, preferred_element_type=F32)
                  + lax.dot_general(xlo_ref[...], whi, NT_DIMS, preferred_element_type=F32))


def _qk_sample(x2d, gain, w_t, *, tn=512):
    Bd, D = x2d.shape
    n_out = 2 * MLSTM_HEADS * MLSTM_DQK
    first = 3 * ATT_WIDTH // tn
    blocks = [_nbytes((Bd, D), F32), _nbytes((tn, D), F32), _nbytes((Bd, tn), F32)]
    return pl.pallas_call(
        _qk_sample_kernel,
        grid=(n_out // tn,),
        in_specs=[pl.BlockSpec((Bd, D), lambda j: (0, 0)),
                  pl.BlockSpec((1, D), lambda j: (0, 0)),
                  pl.BlockSpec((tn, D), lambda j: (first + j, 0))],
        out_specs=pl.BlockSpec((Bd, tn), lambda j: (0, j)),
        out_shape=jax.ShapeDtypeStruct((Bd, n_out), F32),
        scratch_shapes=[pltpu.VMEM((Bd, D), BF16), pltpu.VMEM((Bd, D), BF16)],
        compiler_params=pltpu.CompilerParams(
            dimension_semantics=("arbitrary",),
            vmem_limit_bytes=_vmem_limit(blocks, _nbytes((Bd, D), F32), 4 * _nbytes((tn, D), BF16))),
        name="qk_sample",
    )(x2d, gain, w_t)


def _mlstm_sample_kernel(q_ref, k_ref, v_ref, om_ref, g_ref, c0_ref, n0_ref, m0_ref, mhg_ref,
                         hn_ref, c_out, n_out, m_out):
    H = MLSTM_HEADS
    for h in range(H):
        qf = q_ref[h:h + 1, :] * (MLSTM_DQK ** -0.5)
        kf = k_ref[h:h + 1, :]
        vf = v_ref[h:h + 1, :]
        li = g_ref[:, h:h + 1]
        lf = _log_sigmoid(g_ref[:, H + h:H + h + 1])
        m0 = m0_ref[:, h:h + 1]
        c0 = c0_ref[h]
        n0 = n0_ref[h:h + 1, :]

        inter = lf + m0
        m_t = jnp.maximum(inter, li)
        qk = jnp.sum(qf * kf, axis=1, keepdims=True)
        sw = qk * jnp.exp(li - m_t)
        a = jnp.exp(inter - m_t)
        qhi, qlo = _split_bf16(jnp.broadcast_to(qf, (BF16_SUBLANES, MLSTM_DQK)))
        chi, clo = _split_bf16(c0)
        qc = (jnp.dot(qhi, chi, preferred_element_type=F32)
              + jnp.dot(qhi, clo, preferred_element_type=F32)
              + jnp.dot(qlo, chi, preferred_element_type=F32))[0:1, :]
        num = sw * vf + a * qc
        den = sw + a * jnp.sum(qf * n0, axis=1, keepdims=True)
        hh = num / jnp.maximum(jnp.abs(den), jnp.exp(-m_t))

        ws = jnp.exp(li - m_t)
        decay = jnp.exp(inter - m_t)
        kw = kf * ws
        kcol = jnp.broadcast_to(kw, (LANES, MLSTM_DQK)).T[:, 0:1]
        c_out[h] = decay * c0 + kcol * vf
        n_out[h:h + 1, :] = decay * n0 + kw
        m_out[:, h:h + 1] = m_t

        hn = hh * _rms_scale(hh) * mhg_ref[h:h + 1, :] * _sigmoid(om_ref[h:h + 1, :])
        hn_ref[h:h + 1, :] = hn.astype(hn_ref.dtype)


def _mlstm_sample(qs, ks, vs, oms, gs, c0, n0, m0, mhg):
    Bd = qs.shape[0]
    H = MLSTM_HEADS
    blocks = [2 * _nbytes((8, MLSTM_DQK), F32), 2 * _nbytes((8, MLSTM_DV), F32),
              2 * _nbytes((H, MLSTM_DQK, MLSTM_DV), F32), 4 * _nbytes((8, MLSTM_DV), F32)]
    return pl.pallas_call(
        _mlstm_sample_kernel,
        grid=(Bd,),
        in_specs=[
            pl.BlockSpec((None, H, MLSTM_DQK), lambda b: (b, 0, 0)),
            pl.BlockSpec((None, H, MLSTM_DQK), lambda b: (b, 0, 0)),
            pl.BlockSpec((None, H, MLSTM_DV), lambda b: (b, 0, 0)),
            pl.BlockSpec((None, H, MLSTM_DV), lambda b: (b, 0, 0)),
            pl.BlockSpec((None, 1, LANES), lambda b: (b, 0, 0)),
            pl.BlockSpec((None, H, MLSTM_DQK, MLSTM_DV), lambda b: (b, 0, 0, 0)),
            pl.BlockSpec((None, H, MLSTM_DQK), lambda b: (b, 0, 0)),
            pl.BlockSpec((None, 1, H), lambda b: (b, 0, 0)),
            pl.BlockSpec((H, MLSTM_DV), lambda b: (0, 0)),
        ],
        out_specs=[
            pl.BlockSpec((None, H, MLSTM_DV), lambda b: (b, 0, 0)),
            pl.BlockSpec((None, H, MLSTM_DQK, MLSTM_DV), lambda b: (b, 0, 0, 0)),
            pl.BlockSpec((None, H, MLSTM_DQK), lambda b: (b, 0, 0)),
            pl.BlockSpec((None, 1, H), lambda b: (b, 0, 0)),
        ],
        out_shape=[
            jax.ShapeDtypeStruct((Bd, H, MLSTM_DV), BF16),
            jax.ShapeDtypeStruct((Bd, H, MLSTM_DQK, MLSTM_DV), F32),
            jax.ShapeDtypeStruct((Bd, H, MLSTM_DQK), F32),
            jax.ShapeDtypeStruct((Bd, 1, H), F32),
        ],
        compiler_params=pltpu.CompilerParams(
            dimension_semantics=("arbitrary",),
            vmem_limit_bytes=_vmem_limit(blocks, 0, 4 * _nbytes((MLSTM_DQK, MLSTM_DV), F32))),
        name="mlstm_sample",
    )(qs, ks, vs, oms, gs, c0, n0, m0, mhg)


def _rope_tables(pos):
    half = HEAD_DIM // 2
    inv = ROPE_THETA ** (-jnp.arange(half, dtype=F32) / half)
    ang = pos.astype(F32)[:, None] * inv[None, :]
    cos = jnp.cos(ang)
    sin = jnp.sin(ang)
    return jnp.concatenate([cos, cos], axis=-1), jnp.concatenate([-sin, sin], axis=-1)


def _row_tile(M, cap):
    tm = min(M, cap)
    assert M % tm == 0
    return tm


def kernel(x_prompt, x_sample, cache_k, cache_v, state_C, state_n, state_m,
           norm1_g, w_in, b_if, mh_norm_g, w_out, norm2_g, w_up, w_down, final_g):
    B, T, D = x_prompt.shape
    Bd, Td, _ = x_sample.shape
    depth = w_in.shape[0]
    assert depth == 1 and Td == 1 and D == D_MODEL
    keep = min(WIN_MAX, T)
    H = MLSTM_HEADS

    w_in_t = jnp.transpose(w_in[0])
    w_gate = jnp.pad(w_in_t[N_MAIN:], ((0, LANES - 2 * H), (0, 0))).astype(BF16)
    bias = jnp.pad(b_if[0], (0, LANES - 2 * H)).reshape(1, LANES)
    g1 =norm1_g[0].reshape(1, D)
    g2 = norm2_g[0].reshape(1, D)
    gf = final_g.reshape(1, D)
    mhg = mh_norm_g[0]

    cos_p, sin_p = _rope_tables(jnp.arange(T, dtype=jnp.int32))
    cos_s, sin_s = _rope_tables(jnp.full((Bd,), PAST_LEN, dtype=jnp.int32))

    xp2 = x_prompt.reshape(B * T, D)
    tm = _row_tile(T, 512)
    tm_w = _row_tile(T, 1024)
    xn_p, gates = _norm_rows(xp2, g1, w_gate, bias, tm=tm)
    w_main = _cast_rows_bf16(w_in_t, N_MAIN)
    z, wo, wu = _in_proj_prompt(xn_p, w_main, cos_p, sin_p, [w_out[0], w_up[0]],
                                tm=tm_w, tn=1024, pos_tiles=T // tm_w)
    z3 = z.reshape(B, T, N_MAIN)
    att = _attn_prompt(z3)

    g3 = gates.reshape(B, T, LANES)
    li_row = jnp.swapaxes(g3[:, :, :H], 1, 2).reshape(B, H, 1, T)
    fg_row = jnp.swapaxes(g3[:, :, H:2 * H], 1, 2).reshape(B, H, 1, T)
    li_col = li_row.reshape(B, H, T, 1)
    fg_col = fg_row.reshape(B, H, T, 1)
    hn, c_p, n_p, m_p = _mlstm_prompt(z3, li_row, fg_row, li_col, fg_col, mhg.reshape(1, MLSTM_WIDTH),
                                      L=min(T, 256), heads=MLSTM_HEADS_PER_STEP)

    h_p = _out_proj_prompt(att.reshape(B * T, ATT_WIDTH), hn.reshape(B * T, MLSTM_WIDTH), wo, xp2,
                           tm=tm_w, tn=1024)
    hn2_p = _norm_rows(h_p, g2, tm=tm)
    u_p, wd = _up_proj_prompt(hn2_p, wu, [w_down[0]], tm=tm_w, tn=1024)
    y_p = _down_proj(u_p, wd, h_p, gf, tm=tm, tk=1024)

    k_prompt = z3[:, T - keep:, ATT_WIDTH:2 * ATT_WIDTH].reshape(1, B, keep, ATT_HEADS, HEAD_DIM)
    v_prompt = z3[:, T - keep:, 2 * ATT_WIDTH:3 * ATT_WIDTH].reshape(1, B, keep, ATT_HEADS, HEAD_DIM)

    xs2 = x_sample.reshape(Bd, D)
    zs, gates_s = _in_proj(xs2, g1, w_main, w_gate, bias, cos_s, sin_s, tm=Bd, tn=1024, pos_tiles=1)
    qa_s = zs[:, :ATT_WIDTH].reshape(Bd, ATT_HEADS, HEAD_DIM)
    ka_s = zs[:, ATT_WIDTH:2 * ATT_WIDTH].reshape(Bd, ATT_HEADS, HEAD_DIM)
    va_s = zs[:, 2 * ATT_WIDTH:3 * ATT_WIDTH].reshape(Bd, ATT_HEADS, HEAD_DIM)
    att_s = _attn_sample(qa_s, ka_s, va_s, cache_k[0], cache_v[0])

    qk_s = _qk_sample(xs2, g1, w_in_t)
    qs = qk_s[:, :H * MLSTM_DQK].reshape(Bd, H, MLSTM_DQK)
    ks = qk_s[:, H * MLSTM_DQK:].reshape(Bd, H, MLSTM_DQK)
    o = 3 * ATT_WIDTH + 2 * H * MLSTM_DQK
    vs = zs[:, o:o + MLSTM_WIDTH].reshape(Bd, H, MLSTM_DV)
    o += MLSTM_WIDTH
    oms = zs[:, o:o + MLSTM_WIDTH].reshape(Bd, H, MLSTM_DV)
    hn_s, c_s, n_s, m_s = _mlstm_sample(qs, ks, vs, oms, gates_s.reshape(Bd, 1, LANES),
                                        state_C[0], state_n[0], state_m[0].reshape(Bd, 1, H),
                                        mhg.reshape(H, MLSTM_DV))

    mix_s = jnp.concatenate([att_s.reshape(Bd, ATT_WIDTH), hn_s.reshape(Bd, MLSTM_WIDTH)], axis=-1)
    h_s = _out_proj(mix_s, wo, xs2, tm=Bd, tn=1024)
    u_s = _up_proj(h_s, g2, wu, tm=Bd, tn=1024)
    y_s = _down_proj(u_s, wd, h_s, gf, tm=Bd, tk=1024)

    k_sample = zs[:, ATT_WIDTH:2 * ATT_WIDTH].reshape(1, Bd, 1, ATT_HEADS, HEAD_DIM)
    v_sample = zs[:, 2 * ATT_WIDTH:3 * ATT_WIDTH].reshape(1, Bd, 1, ATT_HEADS, HEAD_DIM)

    return (y_p.reshape(B, T, D), y_s.reshape(Bd, 1, D),
            k_prompt, v_prompt,
            c_p[None], n_p.reshape(1, B, H, MLSTM_DQK), m_p.reshape(1, B, H),
            k_sample, v_sample,
            c_s[None], n_s[None], m_s.reshape(1, Bd, H))
```

```python
import functools

import jax
import jax.numpy as jnp
from jax import lax
from jax.experimental import pallas as pl
from jax.experimental.pallas import tpu as pltpu

F32 = jnp.float32
BF16 = jnp.bfloat16

D_MODEL = 4096
HEAD_DIM = 128
ATT_HEADS = 8
ATT_WIDTH = ATT_HEADS * HEAD_DIM
MLSTM_HEADS = 6
MLSTM_DQK = 256
MLSTM_DV = 512
MLSTM_WIDTH = MLSTM_HEADS * MLSTM_DV
N_MAIN = 3 * ATT_WIDTH + 2 * MLSTM_HEADS * MLSTM_DQK + 2 * MLSTM_WIDTH
GATE_CAP = 15.0
DILATED_CONFIGS = ((128, 1), (512, 4), (2048, 16))
WIN_MAX = 2048
ROPE_THETA = 10000.0
EPS = 1e-6
PAST_LEN = 8192
NEG_INF = float("-inf")

LANES = 128
V7X_VMEM_BYTES = 64 * 1024 * 1024
Q_TILE = 128
ACC_COLS = 512
ATT_GROUP = 8
NT_DIMS = (((1,), (1,)), ((), ()))
BF16_SUBLANES = 16
MLSTM_HEADS_PER_STEP = 6


def _vmem_limit(block_bytes, scratch_bytes, temp_bytes):
    need = 2 * sum(block_bytes) + scratch_bytes + temp_bytes
    return int(min(max(need, 16 * 1024 * 1024), V7X_VMEM_BYTES - 4 * 1024 * 1024))


def _nbytes(shape, dtype):
    n = 1
    for s in shape:
        n *= s
    return n * jnp.dtype(dtype).itemsize


def _log_sigmoid(x):
    return jnp.minimum(x, 0.0) - jnp.log1p(jnp.exp(-jnp.abs(x)))


def _sigmoid(x):
    return 1.0 / (1.0 + jnp.exp(-x))


def _rms_scale(x):
    return lax.rsqrt(jnp.mean(x * x, axis=-1, keepdims=True) + EPS)


def _norm_gate_kernel(x_ref, g_ref, wg_ref, bias_ref, xn_ref, gate_ref):
    xn = (x_ref[...] * _rms_scale(x_ref[...]) * g_ref[...]).astype(BF16)
    xn_ref[...] = xn
    pre = lax.dot_general(xn, wg_ref[...], NT_DIMS, preferred_element_type=F32) + bias_ref[...]
    gate_ref[...] = GATE_CAP * jnp.tanh(pre / GATE_CAP)


def _norm_kernel(x_ref, g_ref, xn_ref):
    xn_ref[...] = (x_ref[...] * _rms_scale(x_ref[...]) * g_ref[...]).astype(BF16)


def _norm_rows(x2d, gain, w_gate=None, bias=None, *, tm):
    M, D = x2d.shape
    with_gates = w_gate is not None
    in_specs = [pl.BlockSpec((tm, D), lambda i: (i, 0)), pl.BlockSpec((1, D), lambda i: (0, 0))]
    out_specs = [pl.BlockSpec((tm, D), lambda i: (i, 0))]
    out_shape = [jax.ShapeDtypeStruct((M, D), BF16)]
    args = [x2d, gain]
    if with_gates:
        in_specs += [pl.BlockSpec((LANES, D), lambda i: (0, 0)), pl.BlockSpec((1, LANES), lambda i: (0, 0))]
        out_specs.append(pl.BlockSpec((tm, LANES), lambda i: (i, 0)))
        out_shape.append(jax.ShapeDtypeStruct((M, LANES), F32))
        args += [w_gate, bias]
    blocks = [_nbytes((tm, D), F32), _nbytes((tm, D), BF16), _nbytes((LANES, D), BF16), _nbytes((tm, LANES), F32)]
    out = pl.pallas_call(
        _norm_gate_kernel if with_gates else _norm_kernel,
        grid=(M // tm,),
        in_specs=in_specs,
        out_specs=out_specs,
        out_shape=out_shape,
        compiler_params=pltpu.CompilerParams(
            dimension_semantics=("arbitrary",),
            vmem_limit_bytes=_vmem_limit(blocks, 0, _nbytes((tm, D), F32))),
        name="norm_gate" if with_gates else "norm",
    )(*args)
    return out if with_gates else out[0]


def _cast_kernel(w_ref, o_ref):
    o_ref[...] = w_ref[...].astype(o_ref.dtype)


def _cast_rows_bf16(w, n_rows, *, tr=1024, tc=1024):
    _, C = w.shape
    assert n_rows % tr == 0 and C % tc == 0
    return pl.pallas_call(
        _cast_kernel,
        grid=(n_rows // tr, C // tc),
        in_specs=[pl.BlockSpec((tr, tc), lambda i, j: (i, j))],
        out_specs=pl.BlockSpec((tr, tc), lambda i, j: (i, j)),
        out_shape=jax.ShapeDtypeStruct((n_rows, C), BF16),
        compiler_params=pltpu.CompilerParams(
            dimension_semantics=("arbitrary", "arbitrary"),
            vmem_limit_bytes=_vmem_limit([_nbytes((tr, tc), F32), _nbytes((tr, tc), BF16)], 0, 0)),
        name="cast_bf16",
    )(w)


def _store_with_rope(product, z_ref, cos_ref, sin_ref, j, n_rope_tiles, tn):
    @pl.when(j < n_rope_tiles)
    def _():
        acc = product()
        c = cos_ref[...]
        s = sin_ref[...]
        for t in range(tn // HEAD_DIM):
            a = acc[:, t * HEAD_DIM:(t + 1) * HEAD_DIM]
            z_ref[:, t * HEAD_DIM:(t + 1) * HEAD_DIM] = a * c + pltpu.roll(a, HEAD_DIM // 2, 1) * s

    @pl.when(j >= n_rope_tiles)
    def _():
        z_ref[...] = product()


class _SideCast:
    def __init__(self, weights, n_steps, n_inner):
        rows = weights[0].shape[0]
        assert all(w.shape[0] == rows for w in weights)
        self.weights = weights
        self.n_inner = n_inner
        self.n_blocks = min(1 << (n_steps.bit_length() - 1), rows // BF16_SUBLANES)
        self.rows = rows // self.n_blocks

    def _index(self, a, b):
        return (jnp.minimum(a * self.n_inner + b, self.n_blocks - 1), 0)

    def specs(self):
        return [pl.BlockSpec((self.rows, w.shape[1]), self._index) for w in self.weights]

    def out_shapes(self):
        return [jax.ShapeDtypeStruct(w.shape, BF16) for w in self.weights]

    def block_bytes(self):
        return [_nbytes((self.rows, w.shape[1]), F32) + _nbytes((self.rows, w.shape[1]), BF16)
                for w in self.weights]

    def run(self, src_refs, dst_refs):
        step = pl.program_id(0) * self.n_inner + pl.program_id(1)

        @pl.when(step < self.n_blocks)
        def _():
            for src, dst in zip(src_refs, dst_refs):
                dst[...] = src[...].astype(dst.dtype)


def _inproj_prompt_kernel(xn_ref, w_ref, cos_ref, sin_ref, *refs, n_rope_tiles, tn, side):
    n_side = len(side.weights)
    side_in, z_ref, side_out = refs[:n_side], refs[n_side], refs[n_side + 1:]

    def product():
        return lax.dot_general(xn_ref[...], w_ref[...], NT_DIMS, preferred_element_type=F32)

    _store_with_rope(product, z_ref, cos_ref, sin_ref, pl.program_id(1), n_rope_tiles, tn)
    side.run(side_in, side_out)


def _in_proj_prompt(xn, w_t, cosf, sinf, side_weights, *, tm, tn, pos_tiles):
    M, D = xn.shape
    N = w_t.shape[0]
    grid = (M // tm, N // tn)
    side = _SideCast(side_weights, grid[0] * grid[1], grid[1])
    blocks = [_nbytes((tm, D), BF16) // 2, _nbytes((tn, D), BF16), 2 * _nbytes((tm, LANES), F32),
              _nbytes((tm, tn), F32)] + side.block_bytes()
    return pl.pallas_call(
        functools.partial(_inproj_prompt_kernel, n_rope_tiles=2 * ATT_WIDTH // tn, tn=tn, side=side),
        grid=grid,
        in_specs=[
            pl.BlockSpec((tm, D), lambda i, j: (i, 0), pipeline_mode=pl.Buffered(1)),
            pl.BlockSpec((tn, D), lambda i, j: (j, 0)),
            pl.BlockSpec((tm, LANES), lambda i, j: (i % pos_tiles, 0)),
            pl.BlockSpec((tm, LANES), lambda i, j: (i % pos_tiles, 0)),
        ] + side.specs(),
        out_specs=[pl.BlockSpec((tm, tn), lambda i, j: (i, j))] + side.specs(),
        out_shape=[jax.ShapeDtypeStruct((M, N), F32)] + side.out_shapes(),
        compiler_params=pltpu.CompilerParams(
            dimension_semantics=("arbitrary", "arbitrary"),
            vmem_limit_bytes=_vmem_limit(blocks, 0, 3 * _nbytes((tm, tn), F32))),
        name="in_proj_prompt",
    )(xn, w_t, cosf, sinf, *side_weights)


def _upproj_prompt_kernel(xn_ref, w_ref, *refs, side):
    n_side = len(side.weights)
    side_in, u_ref, side_out = refs[:n_side], refs[n_side], refs[n_side + 1:]
    acc = jnp.dot(xn_ref[...], w_ref[...], preferred_element_type=F32)
    u_ref[...] = jnp.square(jnp.maximum(acc, 0.0)).astype(u_ref.dtype)
    side.run(side_in, side_out)


def _up_proj_prompt(xn, w_up, side_weights, *, tm, tn):
    M, D = xn.shape
    N = w_up.shape[1]
    grid = (M // tm, N // tn)
    side = _SideCast(side_weights, grid[0] * grid[1], grid[1])
    blocks = [_nbytes((tm, D), BF16), _nbytes((D, tn), BF16), _nbytes((tm, tn), BF16)] + side.block_bytes()
    return pl.pallas_call(
        functools.partial(_upproj_prompt_kernel, side=side),
        grid=grid,
        in_specs=[pl.BlockSpec((tm, D), lambda i, j: (i, 0)),
                  pl.BlockSpec((D, tn), lambda i, j: (0, j))] + side.specs(),
        out_specs=[pl.BlockSpec((tm, tn), lambda i, j: (i, j))] + side.specs(),
        out_shape=[jax.ShapeDtypeStruct((M, N), BF16)] + side.out_shapes(),
        compiler_params=pltpu.CompilerParams(
            dimension_semantics=("arbitrary", "arbitrary"),
            vmem_limit_bytes=_vmem_limit(blocks, 0, 3 * _nbytes((tm, tn), F32))),
        name="up_proj_prompt",
    )(xn, w_up, *side_weights)


def _outproj_prompt_kernel(att_ref, hn_ref, w_ref, x_ref, h_ref):
    h_ref[...] = (x_ref[...]
                  + jnp.dot(att_ref[...], w_ref[:ATT_WIDTH, :], preferred_element_type=F32)
                  + jnp.dot(hn_ref[...], w_ref[ATT_WIDTH:, :], preferred_element_type=F32))


def _out_proj_prompt(att2d, hn2d, w, x2d, *, tm, tn):
    M, D = x2d.shape
    K = w.shape[0]
    blocks = [_nbytes((tm, K), BF16), _nbytes((K, tn), BF16), 2 * _nbytes((tm, tn), F32)]
    return pl.pallas_call(
        _outproj_prompt_kernel,
        grid=(M // tm, D // tn),
        in_specs=[pl.BlockSpec((tm, ATT_WIDTH), lambda i, j: (i, 0)),
                  pl.BlockSpec((tm, MLSTM_WIDTH), lambda i, j: (i, 0)),
                  pl.BlockSpec((K, tn), lambda i, j: (0, j)),
                  pl.BlockSpec((tm, tn), lambda i, j: (i, j))],
        out_specs=pl.BlockSpec((tm, tn), lambda i, j: (i, j)),
        out_shape=jax.ShapeDtypeStruct((M, D), F32),
        compiler_params=pltpu.CompilerParams(
            dimension_semantics=("arbitrary", "arbitrary"),
            vmem_limit_bytes=_vmem_limit(blocks, 0, 3 * _nbytes((tm, tn), F32))),
        name="out_proj_prompt",
    )(att2d, hn2d, w, x2d)


def _inproj_kernel(x_ref, g_ref, w_ref, wg_ref, bias_ref, cos_ref, sin_ref,
                   z_ref, gate_ref, xn_ref, *, n_rope_tiles, tn):
    j = pl.program_id(1)

    @pl.when(j == 0)
    def _():
        x = x_ref[...]
        xn = (x * _rms_scale(x) * g_ref[...]).astype(BF16)
        xn_ref[...] = xn
        pre = lax.dot_general(xn, wg_ref[...], NT_DIMS, preferred_element_type=F32) + bias_ref[...]
        gate_ref[...] = GATE_CAP * jnp.tanh(pre / GATE_CAP)

    def product():
        return lax.dot_general(xn_ref[...], w_ref[...], NT_DIMS, preferred_element_type=F32)

    _store_with_rope(product, z_ref, cos_ref, sin_ref, j, n_rope_tiles, tn)


def _in_proj(x2d, gain, w_main, w_gate, bias, cosf, sinf, *, tm, tn, pos_tiles):
    M, D = x2d.shape
    N = w_main.shape[0]
    blocks = [_nbytes((tm, D), F32), _nbytes((tn, D), BF16), _nbytes((LANES, D), BF16),
              2 * _nbytes((tm, LANES), F32), _nbytes((tm, tn), F32), _nbytes((tm, LANES), F32)]
    return pl.pallas_call(
        functools.partial(_inproj_kernel, n_rope_tiles=2 * ATT_WIDTH // tn, tn=tn),
        grid=(M // tm, N // tn),
        in_specs=[
            pl.BlockSpec((tm, D), lambda i, j: (i, 0)),
            pl.BlockSpec((1, D), lambda i, j: (0, 0)),
            pl.BlockSpec((tn, D), lambda i, j: (j, 0)),
            pl.BlockSpec((LANES, D), lambda i, j: (0, 0)),
            pl.BlockSpec((1, LANES), lambda i, j: (0, 0)),
            pl.BlockSpec((tm, LANES), lambda i, j: (i % pos_tiles, 0)),
            pl.BlockSpec((tm, LANES), lambda i, j: (i % pos_tiles, 0)),
        ],
        out_specs=[pl.BlockSpec((tm, tn), lambda i, j: (i, j)),
                   pl.BlockSpec((tm, LANES), lambda i, j: (i, 0))],
        out_shape=[jax.ShapeDtypeStruct((M, N), F32), jax.ShapeDtypeStruct((M, LANES), F32)],
        scratch_shapes=[pltpu.VMEM((tm, D), BF16)],
        compiler_params=pltpu.CompilerParams(
            dimension_semantics=("arbitrary", "arbitrary"),
            vmem_limit_bytes=_vmem_limit(blocks, _nbytes((tm, D), BF16),
                                         _nbytes((tm, D), F32) + 2 * _nbytes((tm, tn), F32))),
        name="in_proj",
    )(x2d, gain, w_main, w_gate, bias, cosf, sinf)


def _attn_prompt_kernel(q_ref, k_ref, v_ref, o_ref, m_s, l_s, acc_s, *, T):
    scale = HEAD_DIM ** -0.5
    G = ATT_GROUP
    row = lax.broadcasted_iota(jnp.int32, (G * Q_TILE, 2 * Q_TILE), 0) & (Q_TILE - 1)
    col = lax.broadcasted_iota(jnp.int32, (G * Q_TILE, 2 * Q_TILE), 1)
    band = jnp.logical_and(col >= row, col - Q_TILE <= row)
    prev_cols = lax.broadcasted_iota(jnp.int32, (Q_TILE, 2 * Q_TILE), 1) < Q_TILE
    nt = (((1,), (1,)), ((), ()))

    assert DILATED_CONFIGS[0][1] == 1
    for ci, (_, r) in enumerate(reversed(DILATED_CONFIGS)):
        tiles_per_class = T // (r * Q_TILE)
        run = min(G, tiles_per_class)
        runs_per_group = G // run
        runs_per_class = tiles_per_class // run
        shift = runs_per_class.bit_length() - 1
        n_groups = T // (Q_TILE * G)

        def rows(ref, start, n, r=r):
            if r == 1:
                return ref[pl.ds(start, n), :]
            return ref[pl.ds(start, n, stride=r), :]

        def put(ref, start, n, val, r=r):
            if r == 1:
                ref[pl.ds(start, n), :] = val
            else:
                ref[pl.ds(start, n, stride=r), :] = val

        def body(gi, carry, r=r, ci=ci, run=run, rpg=runs_per_group, rpc=runs_per_class,
                 shift=shift, rows=rows, put=put):
            n = run * Q_TILE
            starts, vxs, scores = [], [], []
            for j in range(rpg):
                ridx = gi * rpg + j
                cls = lax.shift_right_logical(ridx, shift)
                u0 = (ridx & (rpc - 1)) * n
                start = cls + r * u0
                pstart = cls + r * jnp.maximum(u0 - Q_TILE, 0)
                no_prev = u0 == 0
                q = (rows(q_ref, start, n) * scale).astype(BF16)
                kx = jnp.concatenate([rows(k_ref, pstart, Q_TILE), rows(k_ref, start, n)], axis=0).astype(BF16)
                vx = jnp.concatenate([rows(v_ref, pstart, Q_TILE), rows(v_ref, start, n)], axis=0).astype(BF16)
                for g in range(run):
                    s = lax.dot_general(q[g * Q_TILE:(g + 1) * Q_TILE], kx[g * Q_TILE:(g + 2) * Q_TILE], nt,
                                        preferred_element_type=F32)
                    if g == 0:
                        s = jnp.where(jnp.logical_and(prev_cols, no_prev), NEG_INF, s)
                    scores.append(s)
                starts.append(start)
                vxs.append(vx)
            s_all = jnp.where(band, jnp.concatenate(scores, axis=0), NEG_INF)
            mt = jnp.max(s_all, axis=1, keepdims=True)
            if ci == 0:
                m_new = jnp.broadcast_to(mt, (G * Q_TILE, LANES))
            else:
                m_old = jnp.concatenate([rows(m_s, st, n) for st in starts], axis=0)
                m_new = jnp.maximum(m_old, mt)
                alpha = jnp.exp(m_old - m_new)
            p = jnp.exp(s_all - jnp.concatenate([m_new, m_new], axis=1))
            lt = jnp.sum(p, axis=1, keepdims=True)
            pb = p.astype(BF16)
            outs = []
            for j in range(rpg):
                for g in range(run):
                    t = j * run + g
                    outs.append(jnp.dot(pb[t * Q_TILE:(t + 1) * Q_TILE], vxs[j][g * Q_TILE:(g + 2) * Q_TILE],
                                        preferred_element_type=F32))
            ot = jnp.concatenate(outs, axis=0)
            if ci == 0:
                l_new = jnp.broadcast_to(lt, (G * Q_TILE, LANES))
            else:
                l_new = alpha * jnp.concatenate([rows(l_s, st, n) for st in starts], axis=0) + lt
                ot = alpha * jnp.concatenate([rows(acc_s, st, n) for st in starts], axis=0) + ot
            if ci == len(DILATED_CONFIGS) - 1:
                out = (ot / l_new).astype(o_ref.dtype)
                for j, st in enumerate(starts):
                    o_ref[pl.ds(pl.multiple_of(st, Q_TILE), n), :] = out[j * n:(j + 1) * n]
            else:
                for j, st in enumerate(starts):
                    put(m_s, st, n, m_new[j * n:(j + 1) * n])
                    put(l_s, st, n, l_new[j * n:(j + 1) * n])
                    put(acc_s, st, n, ot[j * n:(j + 1) * n])
            return carry

        lax.fori_loop(0, n_groups, body, 0)


def _attn_prompt(z3):
    B, T, _ = z3.shape
    assert T % (DILATED_CONFIGS[-1][1] * Q_TILE) == 0
    blk = (None, T, HEAD_DIM)
    blocks = [3 * _nbytes((T, HEAD_DIM), F32), _nbytes((T, HEAD_DIM), BF16)]
    return pl.pallas_call(
        functools.partial(_attn_prompt_kernel, T=T),
        grid=(B, ATT_HEADS),
        in_specs=[pl.BlockSpec(blk, lambda b, h: (b, 0, h)),
                  pl.BlockSpec(blk, lambda b, h: (b, 0, ATT_HEADS + h)),
                  pl.BlockSpec(blk, lambda b, h: (b, 0, 2 * ATT_HEADS + h))],
        out_specs=pl.BlockSpec(blk, lambda b, h: (b, 0, h)),
        out_shape=jax.ShapeDtypeStruct((B, T, ATT_WIDTH), BF16),
        scratch_shapes=[pltpu.VMEM((T, LANES), F32)] * 3,
        compiler_params=pltpu.CompilerParams(
            dimension_semantics=("arbitrary", "arbitrary"),
            vmem_limit_bytes=_vmem_limit(blocks, 3 * _nbytes((T, LANES), F32),
                                         2 * _nbytes((T, LANES), F32)
                                         + 12 * _nbytes((ATT_GROUP * Q_TILE, 2 * Q_TILE), F32))),
        name="attn_prompt",
    )(z3, z3, z3)


def _mlstm_prompt_kernel(q_ref, k_ref, v_ref, om_ref, lir_ref, fgr_ref, lic_ref, fgc_ref, mhg_ref,
                         hn_ref, c_out, n_out, m_out, c_s, n_s, m_s, *, L, heads):
    c = pl.program_id(2)

    @pl.when(c == 0)
    def _():
        c_s[...] = jnp.zeros_like(c_s)
        n_s[...] = jnp.zeros_like(n_s)
        m_s[...] = jnp.zeros_like(m_s)

    r_idx = lax.broadcasted_iota(jnp.int32, (L, L), 0)
    c_idx = lax.broadcasted_iota(jnp.int32, (L, L), 1)
    causal = c_idx <= r_idx

    for hh in range(heads):
        qs = slice(hh * MLSTM_DQK, (hh + 1) * MLSTM_DQK)
        vs = slice(hh * MLSTM_DV, (hh + 1) * MLSTM_DV)
        li_row = lir_ref[hh]
        li_col = lic_ref[hh]
        lf_row = _log_sigmoid(fgr_ref[hh])
        lf_col = _log_sigmoid(fgc_ref[hh])
        b_col = jnp.sum(jnp.where(causal, lf_row, 0.0), axis=1, keepdims=True)
        b_row = jnp.sum(jnp.where(r_idx <= c_idx, lf_col, 0.0), axis=0, keepdims=True)

        m_prev = m_s[hh]
        dlog = jnp.where(causal, b_col - b_row + li_row, NEG_INF)
        inter = b_col + m_prev
        m_t = jnp.maximum(inter, jnp.max(dlog, axis=1, keepdims=True))
        dmat = jnp.exp(dlog - m_t)
        a = jnp.exp(inter - m_t)

        qf = q_ref[:, qs] * (MLSTM_DQK ** -0.5)
        qb = qf.astype(BF16)
        kf = k_ref[:, qs]
        kb = kf.astype(BF16)
        vb = v_ref[:, vs].astype(BF16)
        s = lax.dot_general(qb, kb, NT_DIMS, preferred_element_type=F32)
        sw = s * dmat
        c_prev = c_s[hh]
        num = (jnp.dot(sw.astype(BF16), vb, preferred_element_type=F32)
               + a * jnp.dot(qb, c_prev.astype(BF16), preferred_element_type=F32))
        den = (jnp.sum(sw, axis=1, keepdims=True)
               + a * jnp.sum(qf * n_s[hh], axis=1, keepdims=True))
        h = num / jnp.maximum(jnp.abs(den), jnp.exp(-m_t))

        m_new = m_t[L - 1:L, :]
        b_last = b_col[L - 1:L, :]
        ws = jnp.exp(b_last - b_col + li_col - m_new)
        decay = jnp.exp(b_last + m_prev - m_new)
        kw = kf * ws
        kwt = kw.T.astype(BF16)
        c_s[hh] = decay * c_prev + jnp.dot(kwt, vb, preferred_element_type=F32)
        n_s[hh] = decay * n_s[hh] + jnp.sum(kw, axis=0, keepdims=True)
        m_s[hh] = m_new

        hn = h * _rms_scale(h) * mhg_ref[:, vs] * _sigmoid(om_ref[:, vs])
        hn_ref[:, vs] = hn.astype(hn_ref.dtype)

    @pl.when(c == pl.num_programs(2) - 1)
    def _():
        c_out[...] = c_s[...]
        n_out[...] = n_s[...]
        m_out[...] = m_s[...]


def _mlstm_prompt(z3, li_row, fg_row, li_col, fg_col, mhg, *, L, heads):
    B, T, _ = z3.shape
    H = MLSTM_HEADS
    assert H % heads == 0
    wq, wv = heads * MLSTM_DQK, heads * MLSTM_DV
    q_off = 3 * ATT_WIDTH // wq
    k_off = q_off + H // heads
    v_off = (3 * ATT_WIDTH + 2 * H * MLSTM_DQK) // wv
    o_off = v_off + H // heads
    blocks = [2 * _nbytes((L, wq), F32), 2 * _nbytes((L, wv), F32),
              4 * heads * _nbytes((L, LANES), F32), _nbytes((L, wv), BF16),
              heads * _nbytes((MLSTM_DQK, MLSTM_DV), F32)]
    return pl.pallas_call(
        functools.partial(_mlstm_prompt_kernel, L=L, heads=heads),
        grid=(B, H // heads, T // L),
        in_specs=[
            pl.BlockSpec((None, L, wq), lambda b, h, c: (b, c, q_off + h)),
            pl.BlockSpec((None, L, wq), lambda b, h, c: (b, c, k_off + h)),
            pl.BlockSpec((None, L, wv), lambda b, h, c: (b, c, v_off + h)),
            pl.BlockSpec((None, L, wv), lambda b, h, c: (b, c, o_off + h)),
            pl.BlockSpec((None, heads, 1, L), lambda b, h, c: (b, h, 0, c)),
            pl.BlockSpec((None, heads, 1, L), lambda b, h, c: (b, h, 0, c)),
            pl.BlockSpec((None, heads, L, 1), lambda b, h, c: (b, h, c, 0)),
            pl.BlockSpec((None, heads, L, 1), lambda b, h, c: (b, h, c, 0)),
            pl.BlockSpec((1, wv), lambda b, h, c: (0, h)),
        ],
        out_specs=[
            pl.BlockSpec((None, L, wv), lambda b, h, c: (b, c, h)),
            pl.BlockSpec((None, heads, MLSTM_DQK, MLSTM_DV), lambda b, h, c: (b, h, 0, 0)),
            pl.BlockSpec((None, heads, 1, MLSTM_DQK), lambda b, h, c: (b, h, 0, 0)),
            pl.BlockSpec((None, heads, 1, 1), lambda b, h, c: (b, h, 0, 0)),
        ],
        out_shape=[
            jax.ShapeDtypeStruct((B, T, MLSTM_WIDTH), BF16),
            jax.ShapeDtypeStruct((B, H, MLSTM_DQK, MLSTM_DV), F32),
            jax.ShapeDtypeStruct((B, H, 1, MLSTM_DQK), F32),
            jax.ShapeDtypeStruct((B, H, 1, 1), F32),
        ],
        scratch_shapes=[pltpu.VMEM((heads, MLSTM_DQK, MLSTM_DV), F32),
                        pltpu.VMEM((heads, 1, MLSTM_DQK), F32),
                        pltpu.VMEM((heads, 1, 1), F32)],
        compiler_params=pltpu.CompilerParams(
            dimension_semantics=("arbitrary", "arbitrary", "arbitrary"),
            vmem_limit_bytes=_vmem_limit(blocks, heads * _nbytes((MLSTM_DQK, MLSTM_DV), F32),
                                         heads * (8 * _nbytes((L, MLSTM_DV), F32) + 8 * _nbytes((L, L), F32)))),
        name="mlstm_prompt",
    )(z3, z3, z3, z3, li_row, fg_row, li_col, fg_col, mhg)


def _outproj_kernel(mix_ref, w_ref, x_ref, h_ref):
    h_ref[...] = x_ref[...] + jnp.dot(mix_ref[...], w_ref[...], preferred_element_type=F32)


def _out_proj(mix2d, w, x2d, *, tm, tn):
    M, D = x2d.shape
    K = w.shape[0]
    blocks = [_nbytes((tm, K), BF16), _nbytes((K, tn), BF16), 2 * _nbytes((tm, tn), F32)]
    return pl.pallas_call(
        _outproj_kernel,
        grid=(M // tm, D // tn),
        in_specs=[
            pl.BlockSpec((tm, K), lambda i, j: (i, 0)),
            pl.BlockSpec((K, tn), lambda i, j: (0, j)),
            pl.BlockSpec((tm, tn), lambda i, j: (i, j)),
        ],
        out_specs=pl.BlockSpec((tm, tn), lambda i, j: (i, j)),
        out_shape=jax.ShapeDtypeStruct((M, D), F32),
        compiler_params=pltpu.CompilerParams(
            dimension_semantics=("arbitrary", "arbitrary"),
            vmem_limit_bytes=_vmem_limit(blocks, 0, 2 * _nbytes((tm, tn), F32))),
        name="out_proj",
    )(mix2d, w, x2d)


def _upproj_kernel(x_ref, g_ref, w_ref, u_ref, xn_ref):
    @pl.when(pl.program_id(1) == 0)
    def _():
        x = x_ref[...]
        xn_ref[...] = (x * _rms_scale(x) * g_ref[...]).astype(BF16)

    acc = jnp.dot(xn_ref[...], w_ref[...], preferred_element_type=F32)
    u_ref[...] = jnp.square(jnp.maximum(acc, 0.0)).astype(u_ref.dtype)


def _up_proj(h2d, gain, w_up, *, tm, tn):
    M, D = h2d.shape
    N = w_up.shape[1]
    blocks = [_nbytes((tm, D), F32), _nbytes((D, tn), BF16), _nbytes((tm, tn), BF16)]
    return pl.pallas_call(
        _upproj_kernel,
        grid=(M // tm, N // tn),
        in_specs=[pl.BlockSpec((tm, D), lambda i, j: (i, 0)),
                  pl.BlockSpec((1, D), lambda i, j: (0, 0)),
                  pl.BlockSpec((D, tn), lambda i, j: (0, j))],
        out_specs=pl.BlockSpec((tm, tn), lambda i, j: (i, j)),
        out_shape=jax.ShapeDtypeStruct((M, N), BF16),
        scratch_shapes=[pltpu.VMEM((tm, D), BF16)],
        compiler_params=pltpu.CompilerParams(
            dimension_semantics=("arbitrary", "arbitrary"),
            vmem_limit_bytes=_vmem_limit(blocks, _nbytes((tm, D), BF16),
                                         _nbytes((tm, D), F32) + 2 * _nbytes((tm, tn), F32))),
        name="up_proj",
    )(h2d, gain, w_up)


def _downproj_kernel(u_ref, w_ref, h_ref, g_ref, y_ref):
    k = pl.program_id(1)

    @pl.when(k == 0)
    def _():
        y_ref[...] = h_ref[...]

    u = u_ref[...]
    for n in range(0, y_ref.shape[1], ACC_COLS):
        y_ref[:, n:n + ACC_COLS] += jnp.dot(u, w_ref[:, n:n + ACC_COLS], preferred_element_type=F32)

    @pl.when(k == pl.num_programs(1) - 1)
    def _():
        scale = _rms_scale(y_ref[...])
        y_ref[...] = y_ref[...] * scale * g_ref[...]


def _down_proj(u2d, w_down, h2d, gain, *, tm, tk):
    M, D = h2d.shape
    K = u2d.shape[1]
    blocks = [_nbytes((tm, tk), BF16), _nbytes((tk, D), BF16), 2 * _nbytes((tm, D), F32)]
    return pl.pallas_call(
        _downproj_kernel,
        grid=(M // tm, K // tk),
        in_specs=[pl.BlockSpec((tm, tk), lambda i, k: (i, k)),
                  pl.BlockSpec((tk, D), lambda i, k: (k, 0)),
                  pl.BlockSpec((tm, D), lambda i, k: (i, 0)),
                  pl.BlockSpec((1, D), lambda i, k: (0, 0))],
        out_specs=pl.BlockSpec((tm, D), lambda i, k: (i, 0)),
        out_shape=jax.ShapeDtypeStruct((M, D), F32),
        compiler_params=pltpu.CompilerParams(
            dimension_semantics=("arbitrary", "arbitrary"),
            vmem_limit_bytes=_vmem_limit(blocks, 0, _nbytes((tm, D), F32) + _nbytes((tm, ACC_COLS), F32))),
        name="down_proj",
    )(u2d, w_down, h2d, gain)


def _attn_sample_kernel(q_ref, kn_ref, vn_ref, k1_ref, k4_ref, k16_ref, v1_ref, v4_ref, v16_ref, o_ref):
    q = q_ref[...] * (HEAD_DIM ** -0.5)
    kn = kn_ref[...]
    vn = vn_ref[...]
    s0 = jnp.sum(q * kn, axis=-1, keepdims=True)

    ms, dens, nums = [], [], []
    for kt_ref, vt_ref in ((k1_ref, v1_ref), (k4_ref, v4_ref), (k16_ref, v16_ref)):
        kt = kt_ref[...]
        vt = vt_ref[...]
        s = jnp.sum(q[None] * kt, axis=-1, keepdims=True)
        m = jnp.maximum(jnp.max(s, axis=0), s0)
        p = jnp.exp(s - m[None])
        p0 = jnp.exp(s0 - m)
        dens.append(jnp.sum(p, axis=0) + p0)
        nums.append(jnp.sum(p * vt, axis=0) + p0 * vn)
        ms.append(m)
    m_all = jnp.maximum(jnp.maximum(ms[0], ms[1]), ms[2])
    num = jnp.zeros((ATT_HEADS, HEAD_DIM), F32)
    den = jnp.zeros((ATT_HEADS, 1), F32)
    for m, d, n in zip(ms, dens, nums):
        wgt = jnp.exp(m - m_all)
        num = num + wgt * n
        den = den + wgt * d
    o_ref[...] = (num / den).astype(o_ref.dtype)


def _attn_sample(qs, kns, vns, cache_k, cache_v):
    Bd, W, H, hd = cache_k.shape
    new_spec = pl.BlockSpec((None, H, hd), lambda b: (b, 0, 0))
    in_specs = [new_spec, new_spec, new_spec]
    views_k, views_v = [], []
    for (w, r) in DILATED_CONFIGS:
        assert w // r == Q_TILE and W % w == 0 and PAST_LEN >= W >= w
        views_k.append(cache_k.reshape(Bd, W // r, r, H, hd))
        views_v.append(cache_v.reshape(Bd, W // r, r, H, hd))
    for _ in range(2):
        for (w, r) in DILATED_CONFIGS:
            last = W // w - 1
            in_specs.append(pl.BlockSpec((None, Q_TILE, None, H, hd),
                                         lambda b, last=last: (b, last, 0, 0, 0)))
    blocks = [3 * _nbytes((H, hd), F32), 6 * _nbytes((Q_TILE, H, hd), F32)]
    return pl.pallas_call(
        _attn_sample_kernel,
        grid=(Bd,),
        in_specs=in_specs,
        out_specs=pl.BlockSpec((None, H, hd), lambda b: (b, 0, 0)),
        out_shape=jax.ShapeDtypeStruct((Bd, H, hd), BF16),
        compiler_params=pltpu.CompilerParams(
            dimension_semantics=("arbitrary",),
            vmem_limit_bytes=_vmem_limit(blocks, 0, 8 * _nbytes((Q_TILE, H, hd), F32))),
        name="attn_sample",
    )(qs, kns, vns, *views_k, *views_v)


def _split_bf16(a):
    hi = a.astype(BF16)
    return hi, (a - hi.astype(F32)).astype(BF16)


def _qk_sample_kernel(x_ref, g_ref, w_ref, o_ref, xhi_ref, xlo_ref):
    @pl.when(pl.program_id(0) == 0)
    def _():
        x = x_ref[...]
        xhi_ref[...], xlo_ref[...] = _split_bf16(x * _rms_scale(x) * g_ref[...])

    whi, wlo = _split_bf16(w_ref[...])
    xhi = xhi_ref[...]
    o_ref[...] = (lax.dot_general(xhi, whi, NT_DIMS, preferred_element_type=F32)
                  + lax.dot_general(xhi, wlo, NT_DIMS, preferred_element_type=F32)
                  + lax.dot_general(xlo_ref[...], whi, NT_DIMS, preferred_element_type=F32))


def _qk_sample(x2d, gain, w_t, *, tn=512):
    Bd, D = x2d.shape
    n_out = 2 * MLSTM_HEADS * MLSTM_DQK
    first = 3 * ATT_WIDTH // tn
    blocks = [_nbytes((Bd, D), F32), _nbytes((tn, D), F32), _nbytes((Bd, tn), F32)]
    return pl.pallas_call(
        _qk_sample_kernel,
        grid=(n_out // tn,),
        in_specs=[pl.BlockSpec((Bd, D), lambda j: (0, 0)),
                  pl.BlockSpec((1, D), lambda j: (0, 0)),
                  pl.BlockSpec((tn, D), lambda j: (first + j, 0))],
        out_specs=pl.BlockSpec((Bd, tn), lambda j: (0, j)),
        out_shape=jax.ShapeDtypeStruct((Bd, n_out), F32),
        scratch_shapes=[pltpu.VMEM((Bd, D), BF16), pltpu.VMEM((Bd, D), BF16)],
        compiler_params=pltpu.CompilerParams(
            dimension_semantics=("arbitrary",),
            vmem_limit_bytes=_vmem_limit(blocks, _nbytes((Bd, D), F32), 4 * _nbytes((tn, D), BF16))),
        name="qk_sample",
    )(x2d, gain, w_t)


def _mlstm_sample_kernel(q_ref, k_ref, v_ref, om_ref, g_ref, c0_ref, n0_ref, m0_ref, mhg_ref,
                         hn_ref, c_out, n_out, m_out):
    H = MLSTM_HEADS
    for h in range(H):
        qf = q_ref[h:h + 1, :] * (MLSTM_DQK ** -0.5)
        kf = k_ref[h:h + 1, :]
        vf = v_ref[h:h + 1, :]
        li = g_ref[:, h:h + 1]
        lf = _log_sigmoid(g_ref[:, H + h:H + h + 1])
        m0 = m0_ref[:, h:h + 1]
        c0 = c0_ref[h]
        n0 = n0_ref[h:h + 1, :]

        inter = lf + m0
        m_t = jnp.maximum(inter, li)
        qk = jnp.sum(qf * kf, axis=1, keepdims=True)
        sw = qk * jnp.exp(li - m_t)
        a = jnp.exp(inter - m_t)
        qhi, qlo = _split_bf16(jnp.broadcast_to(qf, (BF16_SUBLANES, MLSTM_DQK)))
        chi, clo = _split_bf16(c0)
        qc = (jnp.dot(qhi, chi, preferred_element_type=F32)
              + jnp.dot(qhi, clo, preferred_element_type=F32)
              + jnp.dot(qlo, chi, preferred_element_type=F32))[0:1, :]
        num = sw * vf + a * qc
        den = sw + a * jnp.sum(qf * n0, axis=1, keepdims=True)
        hh = num / jnp.maximum(jnp.abs(den), jnp.exp(-m_t))

        ws = jnp.exp(li - m_t)
        decay = jnp.exp(inter - m_t)
        kw = kf * ws
        kcol = jnp.broadcast_to(kw, (LANES, MLSTM_DQK)).T[:, 0:1]
        c_out[h] = decay * c0 + kcol * vf
        n_out[h:h + 1, :] = decay * n0 + kw
        m_out[:, h:h + 1] = m_t

        hn = hh * _rms_scale(hh) * mhg_ref[h:h + 1, :] * _sigmoid(om_ref[h:h + 1, :])
        hn_ref[h:h + 1, :] = hn.astype(hn_ref.dtype)


def _mlstm_sample(qs, ks, vs, oms, gs, c0, n0, m0, mhg):
    Bd = qs.shape[0]
    H = MLSTM_HEADS
    blocks = [2 * _nbytes((8, MLSTM_DQK), F32), 2 * _nbytes((8, MLSTM_DV), F32),
              2 * _nbytes((H, MLSTM_DQK, MLSTM_DV), F32), 4 * _nbytes((8, MLSTM_DV), F32)]
    return pl.pallas_call(
        _mlstm_sample_kernel,
        grid=(Bd,),
        in_specs=[
            pl.BlockSpec((None, H, MLSTM_DQK), lambda b: (b, 0, 0)),
            pl.BlockSpec((None, H, MLSTM_DQK), lambda b: (b, 0, 0)),
            pl.BlockSpec((None, H, MLSTM_DV), lambda b: (b, 0, 0)),
            pl.BlockSpec((None, H, MLSTM_DV), lambda b: (b, 0, 0)),
            pl.BlockSpec((None, 1, LANES), lambda b: (b, 0, 0)),
            pl.BlockSpec((None, H, MLSTM_DQK, MLSTM_DV), lambda b: (b, 0, 0, 0)),
            pl.BlockSpec((None, H, MLSTM_DQK), lambda b: (b, 0, 0)),
            pl.BlockSpec((None, 1, H), lambda b: (b, 0, 0)),
            pl.BlockSpec((H, MLSTM_DV), lambda b: (0, 0)),
        ],
        out_specs=[
            pl.BlockSpec((None, H, MLSTM_DV), lambda b: (b, 0, 0)),
            pl.BlockSpec((None, H, MLSTM_DQK, MLSTM_DV), lambda b: (b, 0, 0, 0)),
            pl.BlockSpec((None, H, MLSTM_DQK), lambda b: (b, 0, 0)),
            pl.BlockSpec((None, 1, H), lambda b: (b, 0, 0)),
        ],
        out_shape=[
            jax.ShapeDtypeStruct((Bd, H, MLSTM_DV), BF16),
            jax.ShapeDtypeStruct((Bd, H, MLSTM_DQK, MLSTM_DV), F32),
            jax.ShapeDtypeStruct((Bd, H, MLSTM_DQK), F32),
            jax.ShapeDtypeStruct((Bd, 1, H), F32),
        ],
        compiler_params=pltpu.CompilerParams(
            dimension_semantics=("arbitrary",),
            vmem_limit_bytes=_vmem_limit(blocks, 0, 4 * _nbytes((MLSTM_DQK, MLSTM_DV), F32))),
        name="mlstm_sample",
    )(qs, ks, vs, oms, gs, c0, n0, m0, mhg)


def _rope_tables(pos):
    half = HEAD_DIM // 2
    inv = ROPE_THETA ** (-jnp.arange(half, dtype=F32) / half)
    ang = pos.astype(F32)[:, None] * inv[None, :]
    cos = jnp.cos(ang)
    sin = jnp.sin(ang)
    return jnp.concatenate([cos, cos], axis=-1), jnp.concatenate([-sin, sin], axis=-1)


def _row_tile(M, cap):
    tm = min(M, cap)
    assert M % tm == 0
    return tm


def kernel(x_prompt, x_sample, cache_k, cache_v, state_C, state_n, state_m,
           norm1_g, w_in, b_if, mh_norm_g, w_out, norm2_g, w_up, w_down, final_g):
    B, T, D = x_prompt.shape
    Bd, Td, _ = x_sample.shape
    depth = w_in.shape[0]
    assert depth == 1 and Td == 1 and D == D_MODEL
    keep = min(WIN_MAX, T)
    H = MLSTM_HEADS

    w_in_t = jnp.transpose(w_in[0])
    w_gate = jnp.pad(w_in_t[N_MAIN:], ((0, LANES - 2 * H), (0, 0))).astype(BF16)
    bias = jnp.pad(b_if[0], (0, LANES - 2 * H)).reshape(1, LANES)
    g1 =norm1_g[0].reshape(1, D)
    g2 = norm2_g[0].reshape(1, D)
    gf = final_g.reshape(1, D)
    mhg = mh_norm_g[0]

    cos_p, sin_p = _rope_tables(jnp.arange(T, dtype=jnp.int32))
    cos_s, sin_s = _rope_tables(jnp.full((Bd,), PAST_LEN, dtype=jnp.int32))

    xp2 = x_prompt.reshape(B * T, D)
    tm = _row_tile(T, 512)
    tm_w = _row_tile(T, 1024)
    xn_p, gates = _norm_rows(xp2, g1, w_gate, bias, tm=tm)
    w_main = _cast_rows_bf16(w_in_t, N_MAIN)
    z, wo, wu = _in_proj_prompt(xn_p, w_main, cos_p, sin_p, [w_out[0], w_up[0]],
                                tm=tm_w, tn=1024, pos_tiles=T // tm_w)
    z3 = z.reshape(B, T, N_MAIN)
    att = _attn_prompt(z3)

    g3 = gates.reshape(B, T, LANES)
    li_row = jnp.swapaxes(g3[:, :, :H], 1, 2).reshape(B, H, 1, T)
    fg_row = jnp.swapaxes(g3[:, :, H:2 * H], 1, 2).reshape(B, H, 1, T)
    li_col = li_row.reshape(B, H, T, 1)
    fg_col = fg_row.reshape(B, H, T, 1)
    hn, c_p, n_p, m_p = _mlstm_prompt(z3, li_row, fg_row, li_col, fg_col, mhg.reshape(1, MLSTM_WIDTH),
                                      L=min(T, 256), heads=MLSTM_HEADS_PER_STEP)

    h_p = _out_proj_prompt(att.reshape(B * T, ATT_WIDTH), hn.reshape(B * T, MLSTM_WIDTH), wo, xp2,
                           tm=tm_w, tn=1024)
    hn2_p = _norm_rows(h_p, g2, tm=tm)
    u_p, wd = _up_proj_prompt(hn2_p, wu, [w_down[0]], tm=tm_w, tn=1024)
    y_p = _down_proj(u_p, wd, h_p, gf, tm=tm, tk=1024)

    k_prompt = z3[:, T - keep:, ATT_WIDTH:2 * ATT_WIDTH].reshape(1, B, keep, ATT_HEADS, HEAD_DIM)
    v_prompt = z3[:, T - keep:, 2 * ATT_WIDTH:3 * ATT_WIDTH].reshape(1, B, keep, ATT_HEADS, HEAD_DIM)

    xs2 = x_sample.reshape(Bd, D)
    zs, gates_s = _in_proj(xs2, g1, w_main, w_gate, bias, cos_s, sin_s, tm=Bd, tn=1024, pos_tiles=1)
    qa_s = zs[:, :ATT_WIDTH].reshape(Bd, ATT_HEADS, HEAD_DIM)
    ka_s = zs[:, ATT_WIDTH:2 * ATT_WIDTH].reshape(Bd, ATT_HEADS, HEAD_DIM)
    va_s = zs[:, 2 * ATT_WIDTH:3 * ATT_WIDTH].reshape(Bd, ATT_HEADS, HEAD_DIM)
    att_s = _attn_sample(qa_s, ka_s, va_s, cache_k[0], cache_v[0])

    qk_s = _qk_sample(xs2, g1, w_in_t)
    qs = qk_s[:, :H * MLSTM_DQK].reshape(Bd, H, MLSTM_DQK)
    ks = qk_s[:, H * MLSTM_DQK:].reshape(Bd, H, MLSTM_DQK)
    o = 3 * ATT_WIDTH + 2 * H * MLSTM_DQK
    vs = zs[:, o:o + MLSTM_WIDTH].reshape(Bd, H, MLSTM_DV)
    o += MLSTM_WIDTH
    oms = zs[:, o:o + MLSTM_WIDTH].reshape(Bd, H, MLSTM_DV)
    hn_s, c_s, n_s, m_s = _mlstm_sample(qs, ks, vs, oms, gates_s.reshape(Bd, 1, LANES),
                                        state_C[0], state_n[0], state_m[0].reshape(Bd, 1, H),
                                        mhg.reshape(H, MLSTM_DV))

    mix_s = jnp.concatenate([att_s.reshape(Bd, ATT_WIDTH), hn_s.reshape(Bd, MLSTM_WIDTH)], axis=-1)
    h_s = _out_proj(mix_s, wo, xs2, tm=Bd, tn=1024)
    u_s = _up_proj(h_s, g2, wu, tm=Bd, tn=1024)
    y_s = _down_proj(u_s, wd, h_s, gf, tm=Bd, tk=1024)

    k_sample = zs[:, ATT_WIDTH:2 * ATT_WIDTH].reshape(1, Bd, 1, ATT_HEADS, HEAD_DIM)
    v_sample = zs[:, 2 * ATT_WIDTH:3 * ATT_WIDTH].reshape(1, Bd, 1, ATT_HEADS, HEAD_DIM)

    return (y_p.reshape(B, T, D), y_s.reshape(Bd, 1, D),
            k_prompt, v_prompt,
            c_p[None], n_p.reshape(1, B, H, MLSTM_DQK), m_p.reshape(1, B, H),
            k_sample, v_sample,
            c_s[None], n_s[None], m_s.reshape(1, Bd, H))
```

```python
import functools

import jax
import jax.numpy as jnp
from jax import lax
from jax.experimental import pallas as pl
from jax.experimental.pallas import tpu as pltpu

F32 = jnp.float32
BF16 = jnp.bfloat16

D_MODEL = 4096
HEAD_DIM = 128
ATT_HEADS = 8
ATT_WIDTH = ATT_HEADS * HEAD_DIM
MLSTM_HEADS = 6
MLSTM_DQK = 256
MLSTM_DV = 512
MLSTM_WIDTH = MLSTM_HEADS * MLSTM_DV
N_MAIN = 3 * ATT_WIDTH + 2 * MLSTM_HEADS * MLSTM_DQK + 2 * MLSTM_WIDTH
GATE_CAP = 15.0
DILATED_CONFIGS = ((128, 1), (512, 4), (2048, 16))
WIN_MAX = 2048
ROPE_THETA = 10000.0
EPS = 1e-6
PAST_LEN = 8192
NEG_INF = float("-inf")

LANES = 128
V7X_VMEM_BYTES = 64 * 1024 * 1024
Q_TILE = 128
ACC_COLS = 512
ATT_GROUP = 8
NT_DIMS = (((1,), (1,)), ((), ()))
BF16_SUBLANES = 16
MLSTM_HEADS_PER_STEP = 6


def _vmem_limit(block_bytes, scratch_bytes, temp_bytes):
    need = 2 * sum(block_bytes) + scratch_bytes + temp_bytes
    return int(min(max(need, 16 * 1024 * 1024), V7X_VMEM_BYTES - 4 * 1024 * 1024))


def _nbytes(shape, dtype):
    n = 1
    for s in shape:
        n *= s
    return n * jnp.dtype(dtype).itemsize


def _log_sigmoid(x):
    return jnp.minimum(x, 0.0) - jnp.log1p(jnp.exp(-jnp.abs(x)))


def _sigmoid(x):
    return 1.0 / (1.0 + jnp.exp(-x))


def _rms_scale(x):
    return lax.rsqrt(jnp.mean(x * x, axis=-1, keepdims=True) + EPS)


def _norm_gate_kernel(x_ref, g_ref, wg_ref, bias_ref, xn_ref, gate_ref):
    xn = (x_ref[...] * _rms_scale(x_ref[...]) * g_ref[...]).astype(BF16)
    xn_ref[...] = xn
    pre = lax.dot_general(xn, wg_ref[...], NT_DIMS, preferred_element_type=F32) + bias_ref[...]
    gate_ref[...] = GATE_CAP * jnp.tanh(pre / GATE_CAP)


def _norm_kernel(x_ref, g_ref, xn_ref):
    xn_ref[...] = (x_ref[...] * _rms_scale(x_ref[...]) * g_ref[...]).astype(BF16)


def _norm_rows(x2d, gain, w_gate=None, bias=None, *, tm):
    M, D = x2d.shape
    with_gates = w_gate is not None
    in_specs = [pl.BlockSpec((tm, D), lambda i: (i, 0)), pl.BlockSpec((1, D), lambda i: (0, 0))]
    out_specs = [pl.BlockSpec((tm, D), lambda i: (i, 0))]
    out_shape = [jax.ShapeDtypeStruct((M, D), BF16)]
    args = [x2d, gain]
    if with_gates:
        in_specs += [pl.BlockSpec((LANES, D), lambda i: (0, 0)), pl.BlockSpec((1, LANES), lambda i: (0, 0))]
        out_specs.append(pl.BlockSpec((tm, LANES), lambda i: (i, 0)))
        out_shape.append(jax.ShapeDtypeStruct((M, LANES), F32))
        args += [w_gate, bias]
    blocks = [_nbytes((tm, D), F32), _nbytes((tm, D), BF16), _nbytes((LANES, D), BF16), _nbytes((tm, LANES), F32)]
    out = pl.pallas_call(
        _norm_gate_kernel if with_gates else _norm_kernel,
        grid=(M // tm,),
        in_specs=in_specs,
        out_specs=out_specs,
        out_shape=out_shape,
        compiler_params=pltpu.CompilerParams(
            dimension_semantics=("arbitrary",),
            vmem_limit_bytes=_vmem_limit(blocks, 0, _nbytes((tm, D), F32))),
        name="norm_gate" if with_gates else "norm",
    )(*args)
    return out if with_gates else out[0]


def _store_with_rope(product, z_ref, cos_ref, sin_ref, j, n_rope_tiles, tn):
    @pl.when(j < n_rope_tiles)
    def _():
        acc = product()
        c = cos_ref[...]
        s = sin_ref[...]
        for t in range(tn // HEAD_DIM):
            a = acc[:, t * HEAD_DIM:(t + 1) * HEAD_DIM]
            z_ref[:, t * HEAD_DIM:(t + 1) * HEAD_DIM] = a * c + pltpu.roll(a, HEAD_DIM // 2, 1) * s

    @pl.when(j >= n_rope_tiles)
    def _():
        z_ref[...] = product()


class _SideCast:
    def __init__(self, weights, n_steps, n_inner):
        rows = weights[0].shape[0]
        assert all(w.shape[0] == rows for w in weights)
        self.weights = weights
        self.n_inner = n_inner
        self.n_blocks = min(1 << (n_steps.bit_length() - 1), rows // BF16_SUBLANES)
        self.rows = rows // self.n_blocks

    def _index(self, a, b):
        return (jnp.minimum(a * self.n_inner + b, self.n_blocks - 1), 0)

    def specs(self):
        return [pl.BlockSpec((self.rows, w.shape[1]), self._index) for w in self.weights]

    def out_shapes(self):
        return [jax.ShapeDtypeStruct(w.shape, BF16) for w in self.weights]

    def block_bytes(self):
        return [_nbytes((self.rows, w.shape[1]), F32) + _nbytes((self.rows, w.shape[1]), BF16)
                for w in self.weights]

    def run(self, src_refs, dst_refs):
        step = pl.program_id(0) * self.n_inner + pl.program_id(1)

        @pl.when(step < self.n_blocks)
        def _():
            for src, dst in zip(src_refs, dst_refs):
                dst[...] = src[...].astype(dst.dtype)


def _inproj_prompt_kernel(xn_ref, w_ref, cos_ref, sin_ref, *refs, n_rope_tiles, tn, side):
    n_side = len(side.weights)
    side_in, z_ref, side_out = refs[:n_side], refs[n_side], refs[n_side + 1:]

    def product():
        return lax.dot_general(xn_ref[...], w_ref[...], NT_DIMS, preferred_element_type=F32)

    _store_with_rope(product, z_ref, cos_ref, sin_ref, pl.program_id(1), n_rope_tiles, tn)
    side.run(side_in, side_out)


def _in_proj_prompt(xn, w_t, cosf, sinf, side_weights, *, tm, tn, pos_tiles):
    M, D = xn.shape
    N = w_t.shape[0]
    grid = (M // tm, N // tn)
    side = _SideCast(side_weights, grid[0] * grid[1], grid[1])
    blocks = [_nbytes((tm, D), BF16) // 2, _nbytes((tn, D), BF16), 2 * _nbytes((tm, LANES), F32),
              _nbytes((tm, tn), F32)] + side.block_bytes()
    return pl.pallas_call(
        functools.partial(_inproj_prompt_kernel, n_rope_tiles=2 * ATT_WIDTH // tn, tn=tn, side=side),
        grid=grid,
        in_specs=[
            pl.BlockSpec((tm, D), lambda i, j: (i, 0), pipeline_mode=pl.Buffered(1)),
            pl.BlockSpec((tn, D), lambda i, j: (j, 0)),
            pl.BlockSpec((tm, LANES), lambda i, j: (i % pos_tiles, 0)),
            pl.BlockSpec((tm, LANES), lambda i, j: (i % pos_tiles, 0)),
        ] + side.specs(),
        out_specs=[pl.BlockSpec((tm, tn), lambda i, j: (i, j))] + side.specs(),
        out_shape=[jax.ShapeDtypeStruct((M, N), F32)] + side.out_shapes(),
        compiler_params=pltpu.CompilerParams(
            dimension_semantics=("arbitrary", "arbitrary"),
            vmem_limit_bytes=_vmem_limit(blocks, 0, 3 * _nbytes((tm, tn), F32))),
        name="in_proj_prompt",
    )(xn, w_t, cosf, sinf, *side_weights)


def _upproj_prompt_kernel(xn_ref, w_ref, *refs, side):
    n_side = len(side.weights)
    side_in, u_ref, side_out = refs[:n_side], refs[n_side], refs[n_side + 1:]
    acc = jnp.dot(xn_ref[...], w_ref[...], preferred_element_type=F32)
    u_ref[...] = jnp.square(jnp.maximum(acc, 0.0)).astype(u_ref.dtype)
    side.run(side_in, side_out)


def _up_proj_prompt(xn, w_up, side_weights, *, tm, tn):
    M, D = xn.shape
    N = w_up.shape[1]
    grid = (M // tm, N // tn)
    side = _SideCast(side_weights, grid[0] * grid[1], grid[1])
    blocks = [_nbytes((tm, D), BF16), _nbytes((D, tn), BF16), _nbytes((tm, tn), BF16)] + side.block_bytes()
    return pl.pallas_call(
        functools.partial(_upproj_prompt_kernel, side=side),
        grid=grid,
        in_specs=[pl.BlockSpec((tm, D), lambda i, j: (i, 0)),
                  pl.BlockSpec((D, tn), lambda i, j: (0, j))] + side.specs(),
        out_specs=[pl.BlockSpec((tm, tn), lambda i, j: (i, j))] + side.specs(),
        out_shape=[jax.ShapeDtypeStruct((M, N), BF16)] + side.out_shapes(),
        compiler_params=pltpu.CompilerParams(
            dimension_semantics=("arbitrary", "arbitrary"),
            vmem_limit_bytes=_vmem_limit(blocks, 0, 3 * _nbytes((tm, tn), F32))),
        name="up_proj_prompt",
    )(xn, w_up, *side_weights)


def _outproj_prompt_kernel(att_ref, hn_ref, w_ref, x_ref, h_ref):
    h_ref[...] = (x_ref[...]
                  + jnp.dot(att_ref[...], w_ref[:ATT_WIDTH, :], preferred_element_type=F32)
                  + jnp.dot(hn_ref[...], w_ref[ATT_WIDTH:, :], preferred_element_type=F32))


def _out_proj_prompt(att2d, hn2d, w, x2d, *, tm, tn):
    M, D = x2d.shape
    K = w.shape[0]
    blocks = [_nbytes((tm, K), BF16), _nbytes((K, tn), BF16), 2 * _nbytes((tm, tn), F32)]
    return pl.pallas_call(
        _outproj_prompt_kernel,
        grid=(M // tm, D // tn),
        in_specs=[pl.BlockSpec((tm, ATT_WIDTH), lambda i, j: (i, 0)),
                  pl.BlockSpec((tm, MLSTM_WIDTH), lambda i, j: (i, 0)),
                  pl.BlockSpec((K, tn), lambda i, j: (0, j)),
                  pl.BlockSpec((tm, tn), lambda i, j: (i, j))],
        out_specs=pl.BlockSpec((tm, tn), lambda i, j: (i, j)),
        out_shape=jax.ShapeDtypeStruct((M, D), F32),
        compiler_params=pltpu.CompilerParams(
            dimension_semantics=("arbitrary", "arbitrary"),
            vmem_limit_bytes=_vmem_limit(blocks, 0, 3 * _nbytes((tm, tn), F32))),
        name="out_proj_prompt",
    )(att2d, hn2d, w, x2d)


def _inproj_kernel(x_ref, g_ref, w_ref, wg_ref, bias_ref, cos_ref, sin_ref,
                   z_ref, gate_ref, xn_ref, *, n_rope_tiles, tn):
    j = pl.program_id(1)

    @pl.when(j == 0)
    def _():
        x = x_ref[...]
        xn = (x * _rms_scale(x) * g_ref[...]).astype(BF16)
        xn_ref[...] = xn
        pre = lax.dot_general(xn, wg_ref[...], NT_DIMS, preferred_element_type=F32) + bias_ref[...]
        gate_ref[...] = GATE_CAP * jnp.tanh(pre / GATE_CAP)

    def product():
        return lax.dot_general(xn_ref[...], w_ref[...], NT_DIMS, preferred_element_type=F32)

    _store_with_rope(product, z_ref, cos_ref, sin_ref, j, n_rope_tiles, tn)


def _in_proj(x2d, gain, w_main, w_gate, bias, cosf, sinf, *, tm, tn, pos_tiles):
    M, D = x2d.shape
    N = w_main.shape[0]
    blocks = [_nbytes((tm, D), F32), _nbytes((tn, D), BF16), _nbytes((LANES, D), BF16),
              2 * _nbytes((tm, LANES), F32), _nbytes((tm, tn), F32), _nbytes((tm, LANES), F32)]
    return pl.pallas_call(
        functools.partial(_inproj_kernel, n_rope_tiles=2 * ATT_WIDTH // tn, tn=tn),
        grid=(M // tm, N // tn),
        in_specs=[
            pl.BlockSpec((tm, D), lambda i, j: (i, 0)),
            pl.BlockSpec((1, D), lambda i, j: (0, 0)),
            pl.BlockSpec((tn, D), lambda i, j: (j, 0)),
            pl.BlockSpec((LANES, D), lambda i, j: (0, 0)),
            pl.BlockSpec((1, LANES), lambda i, j: (0, 0)),
            pl.BlockSpec((tm, LANES), lambda i, j: (i % pos_tiles, 0)),
            pl.BlockSpec((tm, LANES), lambda i, j: (i % pos_tiles, 0)),
        ],
        out_specs=[pl.BlockSpec((tm, tn), lambda i, j: (i, j)),
                   pl.BlockSpec((tm, LANES), lambda i, j: (i, 0))],
        out_shape=[jax.ShapeDtypeStruct((M, N), F32), jax.ShapeDtypeStruct((M, LANES), F32)],
        scratch_shapes=[pltpu.VMEM((tm, D), BF16)],
        compiler_params=pltpu.CompilerParams(
            dimension_semantics=("arbitrary", "arbitrary"),
            vmem_limit_bytes=_vmem_limit(blocks, _nbytes((tm, D), BF16),
                                         _nbytes((tm, D), F32) + 2 * _nbytes((tm, tn), F32))),
        name="in_proj",
    )(x2d, gain, w_main, w_gate, bias, cosf, sinf)


def _attn_prompt_kernel(q_ref, k_ref, v_ref, o_ref, m_s, l_s, acc_s, *, T):
    scale = HEAD_DIM ** -0.5
    G = ATT_GROUP
    row = lax.broadcasted_iota(jnp.int32, (G * Q_TILE, 2 * Q_TILE), 0) & (Q_TILE - 1)
    col = lax.broadcasted_iota(jnp.int32, (G * Q_TILE, 2 * Q_TILE), 1)
    band = jnp.logical_and(col >= row, col - Q_TILE <= row)
    prev_cols = lax.broadcasted_iota(jnp.int32, (Q_TILE, 2 * Q_TILE), 1) < Q_TILE
    nt = (((1,), (1,)), ((), ()))

    assert DILATED_CONFIGS[0][1] == 1
    for ci, (_, r) in enumerate(reversed(DILATED_CONFIGS)):
        tiles_per_class = T // (r * Q_TILE)
        run = min(G, tiles_per_class)
        runs_per_group = G // run
        runs_per_class = tiles_per_class // run
        shift = runs_per_class.bit_length() - 1
        n_groups = T // (Q_TILE * G)

        def rows(ref, start, n, r=r):
            if r == 1:
                return ref[pl.ds(start, n), :]
            return ref[pl.ds(start, n, stride=r), :]

        def put(ref, start, n, val, r=r):
            if r == 1:
                ref[pl.ds(start, n), :] = val
            else:
                ref[pl.ds(start, n, stride=r), :] = val

        def body(gi, carry, r=r, ci=ci, run=run, rpg=runs_per_group, rpc=runs_per_class,
                 shift=shift, rows=rows, put=put):
            n = run * Q_TILE
            starts, vxs, scores = [], [], []
            for j in range(rpg):
                ridx = gi * rpg + j
                cls = lax.shift_right_logical(ridx, shift)
                u0 = (ridx & (rpc - 1)) * n
                start = cls + r * u0
                pstart = cls + r * jnp.maximum(u0 - Q_TILE, 0)
                no_prev = u0 == 0
                q = (rows(q_ref, start, n) * scale).astype(BF16)
                kx = jnp.concatenate([rows(k_ref, pstart, Q_TILE), rows(k_ref, start, n)], axis=0).astype(BF16)
                vx = jnp.concatenate([rows(v_ref, pstart, Q_TILE), rows(v_ref, start, n)], axis=0).astype(BF16)
                for g in range(run):
                    s = lax.dot_general(q[g * Q_TILE:(g + 1) * Q_TILE], kx[g * Q_TILE:(g + 2) * Q_TILE], nt,
                                        preferred_element_type=F32)
                    if g == 0:
                        s = jnp.where(jnp.logical_and(prev_cols, no_prev), NEG_INF, s)
                    scores.append(s)
                starts.append(start)
                vxs.append(vx)
            s_all = jnp.where(band, jnp.concatenate(scores, axis=0), NEG_INF)
            mt = jnp.max(s_all, axis=1, keepdims=True)
            if ci == 0:
                m_new = jnp.broadcast_to(mt, (G * Q_TILE, LANES))
            else:
                m_old = jnp.concatenate([rows(m_s, st, n) for st in starts], axis=0)
                m_new = jnp.maximum(m_old, mt)
                alpha = jnp.exp(m_old - m_new)
            p = jnp.exp(s_all - jnp.concatenate([m_new, m_new], axis=1))
            lt = jnp.sum(p, axis=1, keepdims=True)
            pb = p.astype(BF16)
            outs = []
            for j in range(rpg):
                for g in range(run):
                    t = j * run + g
                    outs.append(jnp.dot(pb[t * Q_TILE:(t + 1) * Q_TILE], vxs[j][g * Q_TILE:(g + 2) * Q_TILE],
                                        preferred_element_type=F32))
            ot = jnp.concatenate(outs, axis=0)
            if ci == 0:
                l_new = jnp.broadcast_to(lt, (G * Q_TILE, LANES))
            else:
                l_new = alpha * jnp.concatenate([rows(l_s, st, n) for st in starts], axis=0) + lt
                ot = alpha * jnp.concatenate([rows(acc_s, st, n) for st in starts], axis=0) + ot
            if ci == len(DILATED_CONFIGS) - 1:
                out = (ot / l_new).astype(o_ref.dtype)
                for j, st in enumerate(starts):
                    o_ref[pl.ds(pl.multiple_of(st, Q_TILE), n), :] = out[j * n:(j + 1) * n]
            else:
                for j, st in enumerate(starts):
                    put(m_s, st, n, m_new[j * n:(j + 1) * n])
                    put(l_s, st, n, l_new[j * n:(j + 1) * n])
                    put(acc_s, st, n, ot[j * n:(j + 1) * n])
            return carry

        lax.fori_loop(0, n_groups, body, 0)


def _attn_prompt(z3):
    B, T, _ = z3.shape
    assert T % (DILATED_CONFIGS[-1][1] * Q_TILE) == 0
    blk = (None, T, HEAD_DIM)
    blocks = [3 * _nbytes((T, HEAD_DIM), F32), _nbytes((T, HEAD_DIM), BF16)]
    return pl.pallas_call(
        functools.partial(_attn_prompt_kernel, T=T),
        grid=(B, ATT_HEADS),
        in_specs=[pl.BlockSpec(blk, lambda b, h: (b, 0, h)),
                  pl.BlockSpec(blk, lambda b, h: (b, 0, ATT_HEADS + h)),
                  pl.BlockSpec(blk, lambda b, h: (b, 0, 2 * ATT_HEADS + h))],
        out_specs=pl.BlockSpec(blk, lambda b, h: (b, 0, h)),
        out_shape=jax.ShapeDtypeStruct((B, T, ATT_WIDTH), BF16),
        scratch_shapes=[pltpu.VMEM((T, LANES), F32)] * 3,
        compiler_params=pltpu.CompilerParams(
            dimension_semantics=("arbitrary", "arbitrary"),
            vmem_limit_bytes=_vmem_limit(blocks, 3 * _nbytes((T, LANES), F32),
                                         2 * _nbytes((T, LANES), F32)
                                         + 12 * _nbytes((ATT_GROUP * Q_TILE, 2 * Q_TILE), F32))),
        name="attn_prompt",
    )(z3, z3, z3)


def _mlstm_prompt_kernel(q_ref, k_ref, v_ref, om_ref, lir_ref, fgr_ref, lic_ref, fgc_ref, mhg_ref,
                         hn_ref, c_out, n_out, m_out, c_s, n_s, m_s, *, L, heads):
    c = pl.program_id(2)

    @pl.when(c == 0)
    def _():
        c_s[...] = jnp.zeros_like(c_s)
        n_s[...] = jnp.zeros_like(n_s)
        m_s[...] = jnp.zeros_like(m_s)

    r_idx = lax.broadcasted_iota(jnp.int32, (L, L), 0)
    c_idx = lax.broadcasted_iota(jnp.int32, (L, L), 1)
    causal = c_idx <= r_idx

    for hh in range(heads):
        qs = slice(hh * MLSTM_DQK, (hh + 1) * MLSTM_DQK)
        vs = slice(hh * MLSTM_DV, (hh + 1) * MLSTM_DV)
        li_row = lir_ref[hh]
        li_col = lic_ref[hh]
        lf_row = _log_sigmoid(fgr_ref[hh])
        lf_col = _log_sigmoid(fgc_ref[hh])
        b_col = jnp.sum(jnp.where(causal, lf_row, 0.0), axis=1, keepdims=True)
        b_row = jnp.sum(jnp.where(r_idx <= c_idx, lf_col, 0.0), axis=0, keepdims=True)

        m_prev = m_s[hh]
        dlog = jnp.where(causal, b_col - b_row + li_row, NEG_INF)
        inter = b_col + m_prev
        m_t = jnp.maximum(inter, jnp.max(dlog, axis=1, keepdims=True))
        dmat = jnp.exp(dlog - m_t)
        a = jnp.exp(inter - m_t)

        qf = q_ref[:, qs] * (MLSTM_DQK ** -0.5)
        qb = qf.astype(BF16)
        kf = k_ref[:, qs]
        kb = kf.astype(BF16)
        vb = v_ref[:, vs].astype(BF16)
        s = lax.dot_general(qb, kb, NT_DIMS, preferred_element_type=F32)
        sw = s * dmat
        c_prev = c_s[hh]
        num = (jnp.dot(sw.astype(BF16), vb, preferred_element_type=F32)
               + a * jnp.dot(qb, c_prev.astype(BF16), preferred_element_type=F32))
        den = (jnp.sum(sw, axis=1, keepdims=True)
               + a * jnp.sum(qf * n_s[hh], axis=1, keepdims=True))
        h = num / jnp.maximum(jnp.abs(den), jnp.exp(-m_t))

        m_new = m_t[L - 1:L, :]
        b_last = b_col[L - 1:L, :]
        ws = jnp.exp(b_last - b_col + li_col - m_new)
        decay = jnp.exp(b_last + m_prev - m_new)
        kw = kf * ws
        kwt = kw.T.astype(BF16)
        c_s[hh] = decay * c_prev + jnp.dot(kwt, vb, preferred_element_type=F32)
        n_s[hh] = decay * n_s[hh] + jnp.sum(kw, axis=0, keepdims=True)
        m_s[hh] = m_new

        hn = h * _rms_scale(h) * mhg_ref[:, vs] * _sigmoid(om_ref[:, vs])
        hn_ref[:, vs] = hn.astype(hn_ref.dtype)

    @pl.when(c == pl.num_programs(2) - 1)
    def _():
        c_out[...] = c_s[...]
        n_out[...] = n_s[...]
        m_out[...] = m_s[...]


def _mlstm_prompt(z3, li_row, fg_row, li_col, fg_col, mhg, *, L, heads):
    B, T, _ = z3.shape
    H = MLSTM_HEADS
    assert H % heads == 0
    wq, wv = heads * MLSTM_DQK, heads * MLSTM_DV
    q_off = 3 * ATT_WIDTH // wq
    k_off = q_off + H // heads
    v_off = (3 * ATT_WIDTH + 2 * H * MLSTM_DQK) // wv
    o_off = v_off + H // heads
    blocks = [2 * _nbytes((L, wq), F32), 2 * _nbytes((L, wv), F32),
              4 * heads * _nbytes((L, LANES), F32), _nbytes((L, wv), BF16),
              heads * _nbytes((MLSTM_DQK, MLSTM_DV), F32)]
    return pl.pallas_call(
        functools.partial(_mlstm_prompt_kernel, L=L, heads=heads),
        grid=(B, H // heads, T // L),
        in_specs=[
            pl.BlockSpec((None, L, wq), lambda b, h, c: (b, c, q_off + h)),
            pl.BlockSpec((None, L, wq), lambda b, h, c: (b, c, k_off + h)),
            pl.BlockSpec((None, L, wv), lambda b, h, c: (b, c, v_off + h)),
            pl.BlockSpec((None, L, wv), lambda b, h, c: (b, c, o_off + h)),
            pl.BlockSpec((None, heads, 1, L), lambda b, h, c: (b, h, 0, c)),
            pl.BlockSpec((None, heads, 1, L), lambda b, h, c: (b, h, 0, c)),
            pl.BlockSpec((None, heads, L, 1), lambda b, h, c: (b, h, c, 0)),
            pl.BlockSpec((None, heads, L, 1), lambda b, h, c: (b, h, c, 0)),
            pl.BlockSpec((1, wv), lambda b, h, c: (0, h)),
        ],
        out_specs=[
            pl.BlockSpec((None, L, wv), lambda b, h, c: (b, c, h)),
            pl.BlockSpec((None, heads, MLSTM_DQK, MLSTM_DV), lambda b, h, c: (b, h, 0, 0)),
            pl.BlockSpec((None, heads, 1, MLSTM_DQK), lambda b, h, c: (b, h, 0, 0)),
            pl.BlockSpec((None, heads, 1, 1), lambda b, h, c: (b, h, 0, 0)),
        ],
        out_shape=[
            jax.ShapeDtypeStruct((B, T, MLSTM_WIDTH), BF16),
            jax.ShapeDtypeStruct((B, H, MLSTM_DQK, MLSTM_DV), F32),
            jax.ShapeDtypeStruct((B, H, 1, MLSTM_DQK), F32),
            jax.ShapeDtypeStruct((B, H, 1, 1), F32),
        ],
        scratch_shapes=[pltpu.VMEM((heads, MLSTM_DQK, MLSTM_DV), F32),
                        pltpu.VMEM((heads, 1, MLSTM_DQK), F32),
                        pltpu.VMEM((heads, 1, 1), F32)],
        compiler_params=pltpu.CompilerParams(
            dimension_semantics=("arbitrary", "arbitrary", "arbitrary"),
            vmem_limit_bytes=_vmem_limit(blocks, heads * _nbytes((MLSTM_DQK, MLSTM_DV), F32),
                                         heads * (8 * _nbytes((L, MLSTM_DV), F32) + 8 * _nbytes((L, L), F32)))),
        name="mlstm_prompt",
    )(z3, z3, z3, z3, li_row, fg_row, li_col, fg_col, mhg)


def _outproj_kernel(mix_ref, w_ref, x_ref, h_ref):
    h_ref[...] = x_ref[...] + jnp.dot(mix_ref[...], w_ref[...], preferred_element_type=F32)


def _out_proj(mix2d, w, x2d, *, tm, tn):
    M, D = x2d.shape
    K = w.shape[0]
    blocks = [_nbytes((tm, K), BF16), _nbytes((K, tn), BF16), 2 * _nbytes((tm, tn), F32)]
    return pl.pallas_call(
        _outproj_kernel,
        grid=(M // tm, D // tn),
        in_specs=[
            pl.BlockSpec((tm, K), lambda i, j: (i, 0)),
            pl.BlockSpec((K, tn), lambda i, j: (0, j)),
            pl.BlockSpec((tm, tn), lambda i, j: (i, j)),
        ],
        out_specs=pl.BlockSpec((tm, tn), lambda i, j: (i, j)),
        out_shape=jax.ShapeDtypeStruct((M, D), F32),
        compiler_params=pltpu.CompilerParams(
            dimension_semantics=("arbitrary", "arbitrary"),
            vmem_limit_bytes=_vmem_limit(blocks, 0, 2 * _nbytes((tm, tn), F32))),
        name="out_proj",
    )(mix2d, w, x2d)


def _upproj_kernel(x_ref, g_ref, w_ref, u_ref, xn_ref):
    @pl.when(pl.program_id(1) == 0)
    def _():
        x = x_ref[...]
        xn_ref[...] = (x * _rms_scale(x) * g_ref[...]).astype(BF16)

    acc = jnp.dot(xn_ref[...], w_ref[...], preferred_element_type=F32)
    u_ref[...] = jnp.square(jnp.maximum(acc, 0.0)).astype(u_ref.dtype)


def _up_proj(h2d, gain, w_up, *, tm, tn):
    M, D = h2d.shape
    N = w_up.shape[1]
    blocks = [_nbytes((tm, D), F32), _nbytes((D, tn), BF16), _nbytes((tm, tn), BF16)]
    return pl.pallas_call(
        _upproj_kernel,
        grid=(M // tm, N // tn),
        in_specs=[pl.BlockSpec((tm, D), lambda i, j: (i, 0)),
                  pl.BlockSpec((1, D), lambda i, j: (0, 0)),
                  pl.BlockSpec((D, tn), lambda i, j: (0, j))],
        out_specs=pl.BlockSpec((tm, tn), lambda i, j: (i, j)),
        out_shape=jax.ShapeDtypeStruct((M, N), BF16),
        scratch_shapes=[pltpu.VMEM((tm, D), BF16)],
        compiler_params=pltpu.CompilerParams(
            dimension_semantics=("arbitrary", "arbitrary"),
            vmem_limit_bytes=_vmem_limit(blocks, _nbytes((tm, D), BF16),
                                         _nbytes((tm, D), F32) + 2 * _nbytes((tm, tn), F32))),
        name="up_proj",
    )(h2d, gain, w_up)


def _downproj_kernel(u_ref, w_ref, h_ref, g_ref, y_ref):
    k = pl.program_id(1)

    @pl.when(k == 0)
    def _():
        y_ref[...] = h_ref[...]

    u = u_ref[...]
    for n in range(0, y_ref.shape[1], ACC_COLS):
        y_ref[:, n:n + ACC_COLS] += jnp.dot(u, w_ref[:, n:n + ACC_COLS], preferred_element_type=F32)

    @pl.when(k == pl.num_programs(1) - 1)
    def _():
        scale = _rms_scale(y_ref[...])
        y_ref[...] = y_ref[...] * scale * g_ref[...]


def _down_proj(u2d, w_down, h2d, gain, *, tm, tk):
    M, D = h2d.shape
    K = u2d.shape[1]
    blocks = [_nbytes((tm, tk), BF16), _nbytes((tk, D), BF16), 2 * _nbytes((tm, D), F32)]
    return pl.pallas_call(
        _downproj_kernel,
        grid=(M // tm, K // tk),
        in_specs=[pl.BlockSpec((tm, tk), lambda i, k: (i, k)),
                  pl.BlockSpec((tk, D), lambda i, k: (k, 0)),
                  pl.BlockSpec((tm, D), lambda i, k: (i, 0)),
                  pl.BlockSpec((1, D), lambda i, k: (0, 0))],
        out_specs=pl.BlockSpec((tm, D), lambda i, k: (i, 0)),
        out_shape=jax.ShapeDtypeStruct((M, D), F32),
        compiler_params=pltpu.CompilerParams(
            dimension_semantics=("arbitrary", "arbitrary"),
            vmem_limit_bytes=_vmem_limit(blocks, 0, _nbytes((tm, D), F32) + _nbytes((tm, ACC_COLS), F32))),
        name="down_proj",
    )(u2d, w_down, h2d, gain)


def _attn_sample_kernel(q_ref, kn_ref, vn_ref, k1_ref, k4_ref, k16_ref, v1_ref, v4_ref, v16_ref, o_ref):
    q = q_ref[...] * (HEAD_DIM ** -0.5)
    kn = kn_ref[...]
    vn = vn_ref[...]
    s0 = jnp.sum(q * kn, axis=-1, keepdims=True)

    ms, dens, nums = [], [], []
    for kt_ref, vt_ref in ((k1_ref, v1_ref), (k4_ref, v4_ref), (k16_ref, v16_ref)):
        kt = kt_ref[...]
        vt = vt_ref[...]
        s = jnp.sum(q[None] * kt, axis=-1, keepdims=True)
        m = jnp.maximum(jnp.max(s, axis=0), s0)
        p = jnp.exp(s - m[None])
        p0 = jnp.exp(s0 - m)
        dens.append(jnp.sum(p, axis=0) + p0)
        nums.append(jnp.sum(p * vt, axis=0) + p0 * vn)
        ms.append(m)
    m_all = jnp.maximum(jnp.maximum(ms[0], ms[1]), ms[2])
    num = jnp.zeros((ATT_HEADS, HEAD_DIM), F32)
    den = jnp.zeros((ATT_HEADS, 1), F32)
    for m, d, n in zip(ms, dens, nums):
        wgt = jnp.exp(m - m_all)
        num = num + wgt * n
        den = den + wgt * d
    o_ref[...] = (num / den).astype(o_ref.dtype)


def _attn_sample(qs, kns, vns, cache_k, cache_v):
    Bd, W, H, hd = cache_k.shape
    new_spec = pl.BlockSpec((None, H, hd), lambda b: (b, 0, 0))
    in_specs = [new_spec, new_spec, new_spec]
    views_k, views_v = [], []
    for (w, r) in DILATED_CONFIGS:
        assert w // r == Q_TILE and W % w == 0 and PAST_LEN >= W >= w
        views_k.append(cache_k.reshape(Bd, W // r, r, H, hd))
        views_v.append(cache_v.reshape(Bd, W // r, r, H, hd))
    for _ in range(2):
        for (w, r) in DILATED_CONFIGS:
            last = W // w - 1
            in_specs.append(pl.BlockSpec((None, Q_TILE, None, H, hd),
                                         lambda b, last=last: (b, last, 0, 0, 0)))
    blocks = [3 * _nbytes((H, hd), F32), 6 * _nbytes((Q_TILE, H, hd), F32)]
    return pl.pallas_call(
        _attn_sample_kernel,
        grid=(Bd,),
        in_specs=in_specs,
        out_specs=pl.BlockSpec((None, H, hd), lambda b: (b, 0, 0)),
        out_shape=jax.ShapeDtypeStruct((Bd, H, hd), BF16),
        compiler_params=pltpu.CompilerParams(
            dimension_semantics=("arbitrary",),
            vmem_limit_bytes=_vmem_limit(blocks, 0, 8 * _nbytes((Q_TILE, H, hd), F32))),
        name="attn_sample",
    )(qs, kns, vns, *views_k, *views_v)


def _split_bf16(a):
    hi = a.astype(BF16)
    return hi, (a - hi.astype(F32)).astype(BF16)


def _qk_sample_kernel(x_ref, g_ref, whi_ref, wlo_ref, o_ref, xhi_ref, xlo_ref):
    @pl.when(pl.program_id(0) == 0)
    def _():
        x = x_ref[...]
        xhi_ref[...], xlo_ref[...] = _split_bf16(x * _rms_scale(x) * g_ref[...])

    whi = whi_ref[...]
    xhi = xhi_ref[...]
    o_ref[...] = (lax.dot_general(xhi, whi, NT_DIMS, preferred_element_type=F32)
                  + lax.dot_general(xhi, wlo_ref[...], NT_DIMS, preferred_element_type=F32)
                  + lax.dot_general(xlo_ref[...], whi, NT_DIMS, preferred_element_type=F32))


def _qk_sample(x2d, gain, w_hi, w_lo, *, tn=512):
    Bd, D = x2d.shape
    n_out = w_lo.shape[0]
    first = 3 * ATT_WIDTH // tn
    blocks = [_nbytes((Bd, D), F32), 2 * _nbytes((tn, D), BF16), _nbytes((Bd, tn), F32)]
    return pl.pallas_call(
        _qk_sample_kernel,
        grid=(n_out // tn,),
        in_specs=[pl.BlockSpec((Bd, D), lambda j: (0, 0)),
                  pl.BlockSpec((1, D), lambda j: (0, 0)),
                  pl.BlockSpec((tn, D), lambda j: (first + j, 0)),
                  pl.BlockSpec((tn, D), lambda j: (j, 0))],
        out_specs=pl.BlockSpec((Bd, tn), lambda j: (0, j)),
        out_shape=jax.ShapeDtypeStruct((Bd, n_out), F32),
        scratch_shapes=[pltpu.VMEM((Bd, D), BF16), pltpu.VMEM((Bd, D), BF16)],
        compiler_params=pltpu.CompilerParams(
            dimension_semantics=("arbitrary",),
            vmem_limit_bytes=_vmem_limit(blocks, _nbytes((Bd, D), F32), 4 * _nbytes((tn, D), BF16))),
        name="qk_sample",
    )(x2d, gain, w_hi, w_lo)


def _mlstm_sample_kernel(q_ref, k_ref, v_ref, om_ref, g_ref, c0_ref, n0_ref, m0_ref, mhg_ref,
                         hn_ref, c_out, n_out, m_out):
    H = MLSTM_HEADS
    for h in range(H):
        qf = q_ref[h:h + 1, :] * (MLSTM_DQK ** -0.5)
        kf = k_ref[h:h + 1, :]
        vf = v_ref[h:h + 1, :]
        li = g_ref[:, h:h + 1]
        lf = _log_sigmoid(g_ref[:, H + h:H + h + 1])
        m0 = m0_ref[:, h:h + 1]
        c0 = c0_ref[h]
        n0 = n0_ref[h:h + 1, :]

        inter = lf + m0
        m_t = jnp.maximum(inter, li)
        qk = jnp.sum(qf * kf, axis=1, keepdims=True)
        sw = qk * jnp.exp(li - m_t)
        a = jnp.exp(inter - m_t)
        qhi, qlo = _split_bf16(jnp.broadcast_to(qf, (BF16_SUBLANES, MLSTM_DQK)))
        chi, clo = _split_bf16(c0)
        qc = (jnp.dot(qhi, chi, preferred_element_type=F32)
              + jnp.dot(qhi, clo, preferred_element_type=F32)
              + jnp.dot(qlo, chi, preferred_element_type=F32))[0:1, :]
        num = sw * vf + a * qc
        den = sw + a * jnp.sum(qf * n0, axis=1, keepdims=True)
        hh = num / jnp.maximum(jnp.abs(den), jnp.exp(-m_t))

        ws = jnp.exp(li - m_t)
        decay = jnp.exp(inter - m_t)
        kw = kf * ws
        kcol = jnp.broadcast_to(kw, (LANES, MLSTM_DQK)).T[:, 0:1]
        c_out[h] = decay * c0 + kcol * vf
        n_out[h:h + 1, :] = decay * n0 + kw
        m_out[:, h:h + 1] = m_t

        hn = hh * _rms_scale(hh) * mhg_ref[h:h + 1, :] * _sigmoid(om_ref[h:h + 1, :])
        hn_ref[h:h + 1, :] = hn.astype(hn_ref.dtype)


def _mlstm_sample(qs, ks, vs, oms, gs, c0, n0, m0, mhg):
    Bd = qs.shape[0]
    H = MLSTM_HEADS
    blocks = [2 * _nbytes((8, MLSTM_DQK), F32), 2 * _nbytes((8, MLSTM_DV), F32),
              2 * _nbytes((H, MLSTM_DQK, MLSTM_DV), F32), 4 * _nbytes((8, MLSTM_DV), F32)]
    return pl.pallas_call(
        _mlstm_sample_kernel,
        grid=(Bd,),
        in_specs=[
            pl.BlockSpec((None, H, MLSTM_DQK), lambda b: (b, 0, 0)),
            pl.BlockSpec((None, H, MLSTM_DQK), lambda b: (b, 0, 0)),
            pl.BlockSpec((None, H, MLSTM_DV), lambda b: (b, 0, 0)),
            pl.BlockSpec((None, H, MLSTM_DV), lambda b: (b, 0, 0)),
            pl.BlockSpec((None, 1, LANES), lambda b: (b, 0, 0)),
            pl.BlockSpec((None, H, MLSTM_DQK, MLSTM_DV), lambda b: (b, 0, 0, 0)),
            pl.BlockSpec((None, H, MLSTM_DQK), lambda b: (b, 0, 0)),
            pl.BlockSpec((None, 1, H), lambda b: (b, 0, 0)),
            pl.BlockSpec((H, MLSTM_DV), lambda b: (0, 0)),
        ],
        out_specs=[
            pl.BlockSpec((None, H, MLSTM_DV), lambda b: (b, 0, 0)),
            pl.BlockSpec((None, H, MLSTM_DQK, MLSTM_DV), lambda b: (b, 0, 0, 0)),
            pl.BlockSpec((None, H, MLSTM_DQK), lambda b: (b, 0, 0)),
            pl.BlockSpec((None, 1, H), lambda b: (b, 0, 0)),
        ],
        out_shape=[
            jax.ShapeDtypeStruct((Bd, H, MLSTM_DV), BF16),
            jax.ShapeDtypeStruct((Bd, H, MLSTM_DQK, MLSTM_DV), F32),
            jax.ShapeDtypeStruct((Bd, H, MLSTM_DQK), F32),
            jax.ShapeDtypeStruct((Bd, 1, H), F32),
        ],
        compiler_params=pltpu.CompilerParams(
            dimension_semantics=("arbitrary",),
            vmem_limit_bytes=_vmem_limit(blocks, 0, 4 * _nbytes((MLSTM_DQK, MLSTM_DV), F32))),
        name="mlstm_sample",
    )(qs, ks, vs, oms, gs, c0, n0, m0, mhg)


def _rope_tables(pos):
    half = HEAD_DIM // 2
    inv = ROPE_THETA ** (-jnp.arange(half, dtype=F32) / half)
    ang = pos.astype(F32)[:, None] * inv[None, :]
    cos = jnp.cos(ang)
    sin = jnp.sin(ang)
    return jnp.concatenate([cos, cos], axis=-1), jnp.concatenate([-sin, sin], axis=-1)


def _row_tile(M, cap):
    tm = min(M, cap)
    assert M % tm == 0
    return tm


def kernel(x_prompt, x_sample, cache_k, cache_v, state_C, state_n, state_m,
           norm1_g, w_in, b_if, mh_norm_g, w_out, norm2_g, w_up, w_down, final_g):
    B, T, D = x_prompt.shape
    Bd, Td, _ = x_sample.shape
    depth = w_in.shape[0]
    assert depth == 1 and Td == 1 and D == D_MODEL
    keep = min(WIN_MAX, T)
    H = MLSTM_HEADS

    w_main = jnp.transpose(w_in[0, :, :N_MAIN]).astype(BF16)
    w_gate = jnp.pad(jnp.transpose(w_in[0, :, N_MAIN:]), ((0, LANES - 2 * H), (0, 0))).astype(BF16)
    w_qk = w_in[0, :, 3 * ATT_WIDTH:3 * ATT_WIDTH + 2 * H * MLSTM_DQK]
    w_qk_lo = jnp.transpose((w_qk - w_qk.astype(BF16).astype(F32)).astype(BF16))
    bias = jnp.pad(b_if[0], (0, LANES - 2 * H)).reshape(1, LANES)
    g1 =norm1_g[0].reshape(1, D)
    g2 = norm2_g[0].reshape(1, D)
    gf = final_g.reshape(1, D)
    mhg = mh_norm_g[0]

    cos_p, sin_p = _rope_tables(jnp.arange(T, dtype=jnp.int32))
    cos_s, sin_s = _rope_tables(jnp.full((Bd,), PAST_LEN, dtype=jnp.int32))

    xp2 = x_prompt.reshape(B * T, D)
    tm = _row_tile(T, 512)
    tm_w = _row_tile(T, 1024)
    xn_p, gates = _norm_rows(xp2, g1, w_gate, bias, tm=tm)
    z, wo, wu = _in_proj_prompt(xn_p, w_main, cos_p, sin_p, [w_out[0], w_up[0]],
                                tm=tm_w, tn=1024, pos_tiles=T // tm_w)
    z3 = z.reshape(B, T, N_MAIN)
    att = _attn_prompt(z3)

    g3 = gates.reshape(B, T, LANES)
    li_row = jnp.swapaxes(g3[:, :, :H], 1, 2).reshape(B, H, 1, T)
    fg_row = jnp.swapaxes(g3[:, :, H:2 * H], 1, 2).reshape(B, H, 1, T)
    li_col = li_row.reshape(B, H, T, 1)
    fg_col = fg_row.reshape(B, H, T, 1)
    hn, c_p, n_p, m_p = _mlstm_prompt(z3, li_row, fg_row, li_col, fg_col, mhg.reshape(1, MLSTM_WIDTH),
                                      L=min(T, 256), heads=MLSTM_HEADS_PER_STEP)

    h_p = _out_proj_prompt(att.reshape(B * T, ATT_WIDTH), hn.reshape(B * T, MLSTM_WIDTH), wo, xp2,
                           tm=tm_w, tn=1024)
    hn2_p = _norm_rows(h_p, g2, tm=tm)
    u_p, wd = _up_proj_prompt(hn2_p, wu, [w_down[0]], tm=tm_w, tn=1024)
    y_p = _down_proj(u_p, wd, h_p, gf, tm=tm, tk=1024)

    k_prompt = z3[:, T - keep:, ATT_WIDTH:2 * ATT_WIDTH].reshape(1, B, keep, ATT_HEADS, HEAD_DIM)
    v_prompt = z3[:, T - keep:, 2 * ATT_WIDTH:3 * ATT_WIDTH].reshape(1, B, keep, ATT_HEADS, HEAD_DIM)

    xs2 = x_sample.reshape(Bd, D)
    zs, gates_s = _in_proj(xs2, g1, w_main, w_gate, bias, cos_s, sin_s, tm=Bd, tn=1024, pos_tiles=1)
    qa_s = zs[:, :ATT_WIDTH].reshape(Bd, ATT_HEADS, HEAD_DIM)
    ka_s = zs[:, ATT_WIDTH:2 * ATT_WIDTH].reshape(Bd, ATT_HEADS, HEAD_DIM)
    va_s = zs[:, 2 * ATT_WIDTH:3 * ATT_WIDTH].reshape(Bd, ATT_HEADS, HEAD_DIM)
    att_s = _attn_sample(qa_s, ka_s, va_s, cache_k[0], cache_v[0])

    qk_s = _qk_sample(xs2, g1, w_main, w_qk_lo)
    qs = qk_s[:, :H * MLSTM_DQK].reshape(Bd, H, MLSTM_DQK)
    ks = qk_s[:, H * MLSTM_DQK:].reshape(Bd, H, MLSTM_DQK)
    o = 3 * ATT_WIDTH + 2 * H * MLSTM_DQK
    vs = zs[:, o:o + MLSTM_WIDTH].reshape(Bd, H, MLSTM_DV)
    o += MLSTM_WIDTH
    oms = zs[:, o:o + MLSTM_WIDTH].reshape(Bd, H, MLSTM_DV)
    hn_s, c_s, n_s, m_s = _mlstm_sample(qs, ks, vs, oms, gates_s.reshape(Bd, 1, LANES),
                                        state_C[0], state_n[0], state_m[0].reshape(Bd, 1, H),
                                        mhg.reshape(H, MLSTM_DV))

    mix_s = jnp.concatenate([att_s.reshape(Bd, ATT_WIDTH), hn_s.reshape(Bd, MLSTM_WIDTH)], axis=-1)
    h_s = _out_proj(mix_s, wo, xs2, tm=Bd, tn=1024)
    u_s = _up_proj(h_s, g2, wu, tm=Bd, tn=1024)
    y_s = _down_proj(u_s, wd, h_s, gf, tm=Bd, tk=1024)

    k_sample = zs[:, ATT_WIDTH:2 * ATT_WIDTH].reshape(1, Bd, 1, ATT_HEADS, HEAD_DIM)
    v_sample = zs[:, 2 * ATT_WIDTH:3 * ATT_WIDTH].reshape(1, Bd, 1, ATT_HEADS, HEAD_DIM)

    return (y_p.reshape(B, T, D), y_s.reshape(Bd, 1, D),
            k_prompt, v_prompt,
            c_p[None], n_p.reshape(1, B, H, MLSTM_DQK), m_p.reshape(1, B, H),
            k_sample, v_sample,
            c_s[None], n_s[None], m_s.reshape(1, Bd, H))
```

```python
import functools

import jax
import jax.numpy as jnp
from jax import lax
from jax.experimental import pallas as pl
from jax.experimental.pallas import tpu as pltpu

F32 = jnp.float32
BF16 = jnp.bfloat16

D_MODEL = 4096
HEAD_DIM = 128
ATT_HEADS = 8
ATT_WIDTH = ATT_HEADS * HEAD_DIM
MLSTM_HEADS = 6
MLSTM_DQK = 256
MLSTM_DV = 512
MLSTM_WIDTH = MLSTM_HEADS * MLSTM_DV
N_MAIN = 3 * ATT_WIDTH + 2 * MLSTM_HEADS * MLSTM_DQK + 2 * MLSTM_WIDTH
GATE_CAP = 15.0
DILATED_CONFIGS = ((128, 1), (512, 4), (2048, 16))
WIN_MAX = 2048
ROPE_THETA = 10000.0
EPS = 1e-6
PAST_LEN = 8192
NEG_INF = float("-inf")

LANES = 128
V7X_VMEM_BYTES = 64 * 1024 * 1024
Q_TILE = 128
ACC_COLS = 512
ATT_GROUP = 8
NT_DIMS = (((1,), (1,)), ((), ()))
BF16_SUBLANES = 16
MLSTM_HEADS_PER_STEP = 3
MLSTM_CHUNK = 512


def _vmem_limit(block_bytes, scratch_bytes, temp_bytes):
    need = 2 * sum(block_bytes) + scratch_bytes + temp_bytes
    return int(min(max(need, 16 * 1024 * 1024), V7X_VMEM_BYTES - 4 * 1024 * 1024))


def _nbytes(shape, dtype):
    n = 1
    for s in shape:
        n *= s
    return n * jnp.dtype(dtype).itemsize


def _log_sigmoid(x):
    return jnp.minimum(x, 0.0) - jnp.log1p(jnp.exp(-jnp.abs(x)))


def _sigmoid(x):
    return 1.0 / (1.0 + jnp.exp(-x))


def _rms_scale(x):
    return lax.rsqrt(jnp.mean(x * x, axis=-1, keepdims=True) + EPS)


def _norm_gate_kernel(x_ref, g_ref, wg_ref, bias_ref, xn_ref, gate_ref):
    xn = (x_ref[...] * _rms_scale(x_ref[...]) * g_ref[...]).astype(BF16)
    xn_ref[...] = xn
    pre = lax.dot_general(xn, wg_ref[...], NT_DIMS, preferred_element_type=F32) + bias_ref[...]
    gate_ref[...] = GATE_CAP * jnp.tanh(pre / GATE_CAP)


def _norm_kernel(x_ref, g_ref, xn_ref):
    xn_ref[...] = (x_ref[...] * _rms_scale(x_ref[...]) * g_ref[...]).astype(BF16)


def _norm_rows(x2d, gain, w_gate=None, bias=None, *, tm):
    M, D = x2d.shape
    with_gates = w_gate is not None
    in_specs = [pl.BlockSpec((tm, D), lambda i: (i, 0)), pl.BlockSpec((1, D), lambda i: (0, 0))]
    out_specs = [pl.BlockSpec((tm, D), lambda i: (i, 0))]
    out_shape = [jax.ShapeDtypeStruct((M, D), BF16)]
    args = [x2d, gain]
    if with_gates:
        in_specs += [pl.BlockSpec((LANES, D), lambda i: (0, 0)), pl.BlockSpec((1, LANES), lambda i: (0, 0))]
        out_specs.append(pl.BlockSpec((tm, LANES), lambda i: (i, 0)))
        out_shape.append(jax.ShapeDtypeStruct((M, LANES), F32))
        args += [w_gate, bias]
    blocks = [_nbytes((tm, D), F32), _nbytes((tm, D), BF16), _nbytes((LANES, D), BF16), _nbytes((tm, LANES), F32)]
    out = pl.pallas_call(
        _norm_gate_kernel if with_gates else _norm_kernel,
        grid=(M // tm,),
        in_specs=in_specs,
        out_specs=out_specs,
        out_shape=out_shape,
        compiler_params=pltpu.CompilerParams(
            dimension_semantics=("arbitrary",),
            vmem_limit_bytes=_vmem_limit(blocks, 0, _nbytes((tm, D), F32))),
        name="norm_gate" if with_gates else "norm",
    )(*args)
    return out if with_gates else out[0]


def _cast_kernel(w_ref, o_ref):
    o_ref[...] = w_ref[...].astype(o_ref.dtype)


def _cast_rows_bf16(w, n_rows, *, tr=1024, tc=1024):
    _, C = w.shape
    assert n_rows % tr == 0 and C % tc == 0
    return pl.pallas_call(
        _cast_kernel,
        grid=(n_rows // tr, C // tc),
        in_specs=[pl.BlockSpec((tr, tc), lambda i, j: (i, j))],
        out_specs=pl.BlockSpec((tr, tc), lambda i, j: (i, j)),
        out_shape=jax.ShapeDtypeStruct((n_rows, C), BF16),
        compiler_params=pltpu.CompilerParams(
            dimension_semantics=("arbitrary", "arbitrary"),
            vmem_limit_bytes=_vmem_limit([_nbytes((tr, tc), F32), _nbytes((tr, tc), BF16)], 0, 0)),
        name="cast_bf16",
    )(w)


def _store_with_rope(product, z_ref, cos_ref, sin_ref, j, n_rope_tiles, tn):
    @pl.when(j < n_rope_tiles)
    def _():
        acc = product()
        c = cos_ref[...]
        s = sin_ref[...]
        for t in range(tn // HEAD_DIM):
            a = acc[:, t * HEAD_DIM:(t + 1) * HEAD_DIM]
            z_ref[:, t * HEAD_DIM:(t + 1) * HEAD_DIM] = a * c + pltpu.roll(a, HEAD_DIM // 2, 1) * s

    @pl.when(j >= n_rope_tiles)
    def _():
        z_ref[...] = product()


class _SideCast:
    def __init__(self, weights, n_steps, n_inner):
        rows = weights[0].shape[0]
        assert all(w.shape[0] == rows for w in weights)
        self.weights = weights
        self.n_inner = n_inner
        self.n_blocks = min(1 << (n_steps.bit_length() - 1), rows // BF16_SUBLANES)
        self.rows = rows // self.n_blocks

    def _index(self, a, b):
        return (jnp.minimum(a * self.n_inner + b, self.n_blocks - 1), 0)

    def specs(self):
        return [pl.BlockSpec((self.rows, w.shape[1]), self._index) for w in self.weights]

    def out_shapes(self):
        return [jax.ShapeDtypeStruct(w.shape, BF16) for w in self.weights]

    def block_bytes(self):
        return [_nbytes((self.rows, w.shape[1]), F32) + _nbytes((self.rows, w.shape[1]), BF16)
                for w in self.weights]

    def run(self, src_refs, dst_refs):
        step = pl.program_id(0) * self.n_inner + pl.program_id(1)

        @pl.when(step < self.n_blocks)
        def _():
            for src, dst in zip(src_refs, dst_refs):
                dst[...] = src[...].astype(dst.dtype)


def _inproj_prompt_kernel(xn_ref, w_ref, cos_ref, sin_ref, *refs, n_rope_tiles, tn, side):
    n_side = len(side.weights)
    side_in, z_ref, side_out = refs[:n_side], refs[n_side], refs[n_side + 1:]

    def product():
        return lax.dot_general(xn_ref[...], w_ref[...], NT_DIMS, preferred_element_type=F32)

    _store_with_rope(product, z_ref, cos_ref, sin_ref, pl.program_id(1), n_rope_tiles, tn)
    side.run(side_in, side_out)


def _in_proj_prompt(xn, w_t, cosf, sinf, side_weights, *, tm, tn, pos_tiles):
    M, D = xn.shape
    N = w_t.shape[0]
    grid = (M // tm, N // tn)
    side = _SideCast(side_weights, grid[0] * grid[1], grid[1])
    blocks = [_nbytes((tm, D), BF16) // 2, _nbytes((tn, D), BF16), 2 * _nbytes((tm, LANES), F32),
              _nbytes((tm, tn), F32)] + side.block_bytes()
    return pl.pallas_call(
        functools.partial(_inproj_prompt_kernel, n_rope_tiles=2 * ATT_WIDTH // tn, tn=tn, side=side),
        grid=grid,
        in_specs=[
            pl.BlockSpec((tm, D), lambda i, j: (i, 0), pipeline_mode=pl.Buffered(1)),
            pl.BlockSpec((tn, D), lambda i, j: (j, 0)),
            pl.BlockSpec((tm, LANES), lambda i, j: (i % pos_tiles, 0)),
            pl.BlockSpec((tm, LANES), lambda i, j: (i % pos_tiles, 0)),
        ] + side.specs(),
        out_specs=[pl.BlockSpec((tm, tn), lambda i, j: (i, j))] + side.specs(),
        out_shape=[jax.ShapeDtypeStruct((M, N), F32)] + side.out_shapes(),
        compiler_params=pltpu.CompilerParams(
            dimension_semantics=("arbitrary", "arbitrary"),
            vmem_limit_bytes=_vmem_limit(blocks, 0, 3 * _nbytes((tm, tn), F32))),
        name="in_proj_prompt",
    )(xn, w_t, cosf, sinf, *side_weights)


def _upproj_prompt_kernel(xn_ref, w_ref, *refs, side):
    n_side = len(side.weights)
    side_in, u_ref, side_out = refs[:n_side], refs[n_side], refs[n_side + 1:]
    acc = jnp.dot(xn_ref[...], w_ref[...], preferred_element_type=F32)
    u_ref[...] = jnp.square(jnp.maximum(acc, 0.0)).astype(u_ref.dtype)
    side.run(side_in, side_out)


def _up_proj_prompt(xn, w_up, side_weights, *, tm, tn):
    M, D = xn.shape
    N = w_up.shape[1]
    grid = (M // tm, N // tn)
    side = _SideCast(side_weights, grid[0] * grid[1], grid[1])
    blocks = [_nbytes((tm, D), BF16), _nbytes((D, tn), BF16), _nbytes((tm, tn), BF16)] + side.block_bytes()
    return pl.pallas_call(
        functools.partial(_upproj_prompt_kernel, side=side),
        grid=grid,
        in_specs=[pl.BlockSpec((tm, D), lambda i, j: (i, 0)),
                  pl.BlockSpec((D, tn), lambda i, j: (0, j))] + side.specs(),
        out_specs=[pl.BlockSpec((tm, tn), lambda i, j: (i, j))] + side.specs(),
        out_shape=[jax.ShapeDtypeStruct((M, N), BF16)] + side.out_shapes(),
        compiler_params=pltpu.CompilerParams(
            dimension_semantics=("arbitrary", "arbitrary"),
            vmem_limit_bytes=_vmem_limit(blocks, 0, 3 * _nbytes((tm, tn), F32))),
        name="up_proj_prompt",
    )(xn, w_up, *side_weights)


def _outproj_prompt_kernel(att_ref, hn_ref, w_ref, x_ref, h_ref):
    h_ref[...] = (x_ref[...]
                  + jnp.dot(att_ref[...], w_ref[:ATT_WIDTH, :], preferred_element_type=F32)
                  + jnp.dot(hn_ref[...], w_ref[ATT_WIDTH:, :], preferred_element_type=F32))


def _out_proj_prompt(att2d, hn2d, w, x2d, *, tm, tn):
    M, D = x2d.shape
    K = w.shape[0]
    blocks = [_nbytes((tm, K), BF16), _nbytes((K, tn), BF16), 2 * _nbytes((tm, tn), F32)]
    return pl.pallas_call(
        _outproj_prompt_kernel,
        grid=(M // tm, D // tn),
        in_specs=[pl.BlockSpec((tm, ATT_WIDTH), lambda i, j: (i, 0)),
                  pl.BlockSpec((tm, MLSTM_WIDTH), lambda i, j: (i, 0)),
                  pl.BlockSpec((K, tn), lambda i, j: (0, j)),
                  pl.BlockSpec((tm, tn), lambda i, j: (i, j))],
        out_specs=pl.BlockSpec((tm, tn), lambda i, j: (i, j)),
        out_shape=jax.ShapeDtypeStruct((M, D), F32),
        compiler_params=pltpu.CompilerParams(
            dimension_semantics=("arbitrary", "arbitrary"),
            vmem_limit_bytes=_vmem_limit(blocks, 0, 3 * _nbytes((tm, tn), F32))),
        name="out_proj_prompt",
    )(att2d, hn2d, w, x2d)


def _inproj_kernel(x_ref, g_ref, w_ref, wg_ref, bias_ref, cos_ref, sin_ref,
                   z_ref, gate_ref, xn_ref, *, n_rope_tiles, tn):
    j = pl.program_id(1)

    @pl.when(j == 0)
    def _():
        x = x_ref[...]
        xn = (x * _rms_scale(x) * g_ref[...]).astype(BF16)
        xn_ref[...] = xn
        pre = lax.dot_general(xn, wg_ref[...], NT_DIMS, preferred_element_type=F32) + bias_ref[...]
        gate_ref[...] = GATE_CAP * jnp.tanh(pre / GATE_CAP)

    def product():
        return lax.dot_general(xn_ref[...], w_ref[...], NT_DIMS, preferred_element_type=F32)

    _store_with_rope(product, z_ref, cos_ref, sin_ref, j, n_rope_tiles, tn)


def _in_proj(x2d, gain, w_main, w_gate, bias, cosf, sinf, *, tm, tn, pos_tiles):
    M, D = x2d.shape
    N = w_main.shape[0]
    blocks = [_nbytes((tm, D), F32), _nbytes((tn, D), BF16), _nbytes((LANES, D), BF16),
              2 * _nbytes((tm, LANES), F32), _nbytes((tm, tn), F32), _nbytes((tm, LANES), F32)]
    return pl.pallas_call(
        functools.partial(_inproj_kernel, n_rope_tiles=2 * ATT_WIDTH // tn, tn=tn),
        grid=(M // tm, N // tn),
        in_specs=[
            pl.BlockSpec((tm, D), lambda i, j: (i, 0)),
            pl.BlockSpec((1, D), lambda i, j: (0, 0)),
            pl.BlockSpec((tn, D), lambda i, j: (j, 0)),
            pl.BlockSpec((LANES, D), lambda i, j: (0, 0)),
            pl.BlockSpec((1, LANES), lambda i, j: (0, 0)),
            pl.BlockSpec((tm, LANES), lambda i, j: (i % pos_tiles, 0)),
            pl.BlockSpec((tm, LANES), lambda i, j: (i % pos_tiles, 0)),
        ],
        out_specs=[pl.BlockSpec((tm, tn), lambda i, j: (i, j)),
                   pl.BlockSpec((tm, LANES), lambda i, j: (i, 0))],
        out_shape=[jax.ShapeDtypeStruct((M, N), F32), jax.ShapeDtypeStruct((M, LANES), F32)],
        scratch_shapes=[pltpu.VMEM((tm, D), BF16)],
        compiler_params=pltpu.CompilerParams(
            dimension_semantics=("arbitrary", "arbitrary"),
            vmem_limit_bytes=_vmem_limit(blocks, _nbytes((tm, D), BF16),
                                         _nbytes((tm, D), F32) + 2 * _nbytes((tm, tn), F32))),
        name="in_proj",
    )(x2d, gain, w_main, w_gate, bias, cosf, sinf)


def _attn_prompt_kernel(q_ref, k_ref, v_ref, o_ref, m_s, l_s, acc_s, *, T):
    scale = HEAD_DIM ** -0.5
    G = ATT_GROUP
    row = lax.broadcasted_iota(jnp.int32, (G * Q_TILE, 2 * Q_TILE), 0) & (Q_TILE - 1)
    col = lax.broadcasted_iota(jnp.int32, (G * Q_TILE, 2 * Q_TILE), 1)
    band = jnp.logical_and(col >= row, col - Q_TILE <= row)
    prev_cols = lax.broadcasted_iota(jnp.int32, (Q_TILE, 2 * Q_TILE), 1) < Q_TILE
    nt = (((1,), (1,)), ((), ()))

    assert DILATED_CONFIGS[0][1] == 1
    for ci, (_, r) in enumerate(reversed(DILATED_CONFIGS)):
        tiles_per_class = T // (r * Q_TILE)
        run = min(G, tiles_per_class)
        runs_per_group = G // run
        runs_per_class = tiles_per_class // run
        shift = runs_per_class.bit_length() - 1
        n_groups = T // (Q_TILE * G)

        def rows(ref, start, n, r=r):
            if r == 1:
                return ref[pl.ds(start, n), :]
            return ref[pl.ds(start, n, stride=r), :]

        def put(ref, start, n, val, r=r):
            if r == 1:
                ref[pl.ds(start, n), :] = val
            else:
                ref[pl.ds(start, n, stride=r), :] = val

        def body(gi, carry, r=r, ci=ci, run=run, rpg=runs_per_group, rpc=runs_per_class,
                 shift=shift, rows=rows, put=put):
            n = run * Q_TILE
            starts, vxs, scores = [], [], []
            for j in range(rpg):
                ridx = gi * rpg + j
                cls = lax.shift_right_logical(ridx, shift)
                u0 = (ridx & (rpc - 1)) * n
                start = cls + r * u0
                pstart = cls + r * jnp.maximum(u0 - Q_TILE, 0)
                no_prev = u0 == 0
                q = (rows(q_ref, start, n) * scale).astype(BF16)
                kx = jnp.concatenate([rows(k_ref, pstart, Q_TILE), rows(k_ref, start, n)], axis=0).astype(BF16)
                vx = jnp.concatenate([rows(v_ref, pstart, Q_TILE), rows(v_ref, start, n)], axis=0).astype(BF16)
                for g in range(run):
                    s = lax.dot_general(q[g * Q_TILE:(g + 1) * Q_TILE], kx[g * Q_TILE:(g + 2) * Q_TILE], nt,
                                        preferred_element_type=F32)
                    if g == 0:
                        s = jnp.where(jnp.logical_and(prev_cols, no_prev), NEG_INF, s)
                    scores.append(s)
                starts.append(start)
                vxs.append(vx)
            s_all = jnp.where(band, jnp.concatenate(scores, axis=0), NEG_INF)
            mt = jnp.max(s_all, axis=1, keepdims=True)
            if ci == 0:
                m_new = jnp.broadcast_to(mt, (G * Q_TILE, LANES))
            else:
                m_old = jnp.concatenate([rows(m_s, st, n) for st in starts], axis=0)
                m_new = jnp.maximum(m_old, mt)
                alpha = jnp.exp(m_old - m_new)
            p = jnp.exp(s_all - jnp.concatenate([m_new, m_new], axis=1))
            lt = jnp.sum(p, axis=1, keepdims=True)
            pb = p.astype(BF16)
            outs = []
            for j in range(rpg):
                for g in range(run):
                    t = j * run + g
                    outs.append(jnp.dot(pb[t * Q_TILE:(t + 1) * Q_TILE], vxs[j][g * Q_TILE:(g + 2) * Q_TILE],
                                        preferred_element_type=F32))
            ot = jnp.concatenate(outs, axis=0)
            if ci == 0:
                l_new = jnp.broadcast_to(lt, (G * Q_TILE, LANES))
            else:
                l_new = alpha * jnp.concatenate([rows(l_s, st, n) for st in starts], axis=0) + lt
                ot = alpha * jnp.concatenate([rows(acc_s, st, n) for st in starts], axis=0) + ot
            if ci == len(DILATED_CONFIGS) - 1:
                out = (ot / l_new).astype(o_ref.dtype)
                for j, st in enumerate(starts):
                    o_ref[pl.ds(pl.multiple_of(st, Q_TILE), n), :] = out[j * n:(j + 1) * n]
            else:
                for j, st in enumerate(starts):
                    put(m_s, st, n, m_new[j * n:(j + 1) * n])
                    put(l_s, st, n, l_new[j * n:(j + 1) * n])
                    put(acc_s, st, n, ot[j * n:(j + 1) * n])
            return carry

        lax.fori_loop(0, n_groups, body, 0)


def _attn_prompt(z3):
    B, T, _ = z3.shape
    assert T % (DILATED_CONFIGS[-1][1] * Q_TILE) == 0
    blk = (None, T, HEAD_DIM)
    blocks = [3 * _nbytes((T, HEAD_DIM), F32), _nbytes((T, HEAD_DIM), BF16)]
    return pl.pallas_call(
        functools.partial(_attn_prompt_kernel, T=T),
        grid=(B, ATT_HEADS),
        in_specs=[pl.BlockSpec(blk, lambda b, h: (b, 0, h)),
                  pl.BlockSpec(blk, lambda b, h: (b, 0, ATT_HEADS + h)),
                  pl.BlockSpec(blk, lambda b, h: (b, 0, 2 * ATT_HEADS + h))],
        out_specs=pl.BlockSpec(blk, lambda b, h: (b, 0, h)),
        out_shape=jax.ShapeDtypeStruct((B, T, ATT_WIDTH), BF16),
        scratch_shapes=[pltpu.VMEM((T, LANES), F32)] * 3,
        compiler_params=pltpu.CompilerParams(
            dimension_semantics=("arbitrary", "arbitrary"),
            vmem_limit_bytes=_vmem_limit(blocks, 3 * _nbytes((T, LANES), F32),
                                         2 * _nbytes((T, LANES), F32)
                                         + 12 * _nbytes((ATT_GROUP * Q_TILE, 2 * Q_TILE), F32))),
        name="attn_prompt",
    )(z3, z3, z3)


def _mlstm_prompt_kernel(q_ref, k_ref, v_ref, om_ref, lir_ref, fgr_ref, lic_ref, fgc_ref, mhg_ref,
                         hn_ref, c_out, n_out, m_out, c_s, n_s, m_s, *, L, heads):
    c = pl.program_id(2)

    @pl.when(c == 0)
    def _():
        c_s[...] = jnp.zeros_like(c_s)
        n_s[...] = jnp.zeros_like(n_s)
        m_s[...] = jnp.zeros_like(m_s)

    r_idx = lax.broadcasted_iota(jnp.int32, (L, L), 0)
    c_idx = lax.broadcasted_iota(jnp.int32, (L, L), 1)
    causal = c_idx <= r_idx

    for hh in range(heads):
        qs = slice(hh * MLSTM_DQK, (hh + 1) * MLSTM_DQK)
        vs = slice(hh * MLSTM_DV, (hh + 1) * MLSTM_DV)
        li_row = lir_ref[hh]
        li_col = lic_ref[hh]
        lf_row = _log_sigmoid(fgr_ref[hh])
        lf_col = _log_sigmoid(fgc_ref[hh])
        b_col = jnp.sum(jnp.where(causal, lf_row, 0.0), axis=1, keepdims=True)
        b_row = jnp.sum(jnp.where(r_idx <= c_idx, lf_col, 0.0), axis=0, keepdims=True)

        m_prev = m_s[hh]
        dlog = jnp.where(causal, b_col - b_row + li_row, NEG_INF)
        inter = b_col + m_prev
        m_t = jnp.maximum(inter, jnp.max(dlog, axis=1, keepdims=True))
        dmat = jnp.exp(dlog - m_t)
        a = jnp.exp(inter - m_t)

        qf = q_ref[:, qs] * (MLSTM_DQK ** -0.5)
        qb = qf.astype(BF16)
        kf = k_ref[:, qs]
        kb = kf.astype(BF16)
        vb = v_ref[:, vs].astype(BF16)
        s = lax.dot_general(qb, kb, NT_DIMS, preferred_element_type=F32)
        sw = s * dmat
        c_prev = c_s[hh]
        num = (jnp.dot(sw.astype(BF16), vb, preferred_element_type=F32)
               + a * jnp.dot(qb, c_prev.astype(BF16), preferred_element_type=F32))
        den = (jnp.sum(sw, axis=1, keepdims=True)
               + a * jnp.sum(qf * n_s[hh], axis=1, keepdims=True))
        h = num / jnp.maximum(jnp.abs(den), jnp.exp(-m_t))

        m_new = m_t[L - 1:L, :]
        b_last = b_col[L - 1:L, :]
        ws = jnp.exp(b_last - b_col + li_col - m_new)
        decay = jnp.exp(b_last + m_prev - m_new)
        kw = kf * ws
        kwt = kw.T.astype(BF16)
        c_s[hh] = decay * c_prev + jnp.dot(kwt, vb, preferred_element_type=F32)
        n_s[hh] = decay * n_s[hh] + jnp.sum(kw, axis=0, keepdims=True)
        m_s[hh] = m_new

        hn = h * _rms_scale(h) * mhg_ref[:, vs] * _sigmoid(om_ref[:, vs])
        hn_ref[:, vs] = hn.astype(hn_ref.dtype)

    @pl.when(c == pl.num_programs(2) - 1)
    def _():
        c_out[...] = c_s[...]
        n_out[...] = n_s[...]
        m_out[...] = m_s[...]


def _mlstm_prompt(z3, li_row, fg_row, li_col, fg_col, mhg, *, L, heads):
    B, T, _ = z3.shape
    H = MLSTM_HEADS
    assert H % heads == 0
    wq, wv = heads * MLSTM_DQK, heads * MLSTM_DV
    q_off = 3 * ATT_WIDTH // wq
    k_off = q_off + H // heads
    v_off = (3 * ATT_WIDTH + 2 * H * MLSTM_DQK) // wv
    o_off = v_off + H // heads
    blocks = [2 * _nbytes((L, wq), F32), 2 * _nbytes((L, wv), F32),
              4 * heads * _nbytes((L, LANES), F32), _nbytes((L, wv), BF16),
              heads * _nbytes((MLSTM_DQK, MLSTM_DV), F32)]
    return pl.pallas_call(
        functools.partial(_mlstm_prompt_kernel, L=L, heads=heads),
        grid=(B, H // heads, T // L),
        in_specs=[
            pl.BlockSpec((None, L, wq), lambda b, h, c: (b, c, q_off + h)),
            pl.BlockSpec((None, L, wq), lambda b, h, c: (b, c, k_off + h)),
            pl.BlockSpec((None, L, wv), lambda b, h, c: (b, c, v_off + h)),
            pl.BlockSpec((None, L, wv), lambda b, h, c: (b, c, o_off + h)),
            pl.BlockSpec((None, heads, 1, L), lambda b, h, c: (b, h, 0, c)),
            pl.BlockSpec((None, heads, 1, L), lambda b, h, c: (b, h, 0, c)),
            pl.BlockSpec((None, heads, L, 1), lambda b, h, c: (b, h, c, 0)),
            pl.BlockSpec((None, heads, L, 1), lambda b, h, c: (b, h, c, 0)),
            pl.BlockSpec((1, wv), lambda b, h, c: (0, h)),
        ],
        out_specs=[
            pl.BlockSpec((None, L, wv), lambda b, h, c: (b, c, h)),
            pl.BlockSpec((None, heads, MLSTM_DQK, MLSTM_DV), lambda b, h, c: (b, h, 0, 0)),
            pl.BlockSpec((None, heads, 1, MLSTM_DQK), lambda b, h, c: (b, h, 0, 0)),
            pl.BlockSpec((None, heads, 1, 1), lambda b, h, c: (b, h, 0, 0)),
        ],
        out_shape=[
            jax.ShapeDtypeStruct((B, T, MLSTM_WIDTH), BF16),
            jax.ShapeDtypeStruct((B, H, MLSTM_DQK, MLSTM_DV), F32),
            jax.ShapeDtypeStruct((B, H, 1, MLSTM_DQK), F32),
            jax.ShapeDtypeStruct((B, H, 1, 1), F32),
        ],
        scratch_shapes=[pltpu.VMEM((heads, MLSTM_DQK, MLSTM_DV), F32),
                        pltpu.VMEM((heads, 1, MLSTM_DQK), F32),
                        pltpu.VMEM((heads, 1, 1), F32)],
        compiler_params=pltpu.CompilerParams(
            dimension_semantics=("arbitrary", "arbitrary", "arbitrary"),
            vmem_limit_bytes=_vmem_limit(blocks, heads * _nbytes((MLSTM_DQK, MLSTM_DV), F32),
                                         heads * (8 * _nbytes((L, MLSTM_DV), F32) + 8 * _nbytes((L, L), F32)))),
        name="mlstm_prompt",
    )(z3, z3, z3, z3, li_row, fg_row, li_col, fg_col, mhg)


def _outproj_kernel(mix_ref, w_ref, x_ref, h_ref):
    h_ref[...] = x_ref[...] + jnp.dot(mix_ref[...], w_ref[...], preferred_element_type=F32)


def _out_proj(mix2d, w, x2d, *, tm, tn):
    M, D = x2d.shape
    K = w.shape[0]
    blocks = [_nbytes((tm, K), BF16), _nbytes((K, tn), BF16), 2 * _nbytes((tm, tn), F32)]
    return pl.pallas_call(
        _outproj_kernel,
        grid=(M // tm, D // tn),
        in_specs=[
            pl.BlockSpec((tm, K), lambda i, j: (i, 0)),
            pl.BlockSpec((K, tn), lambda i, j: (0, j)),
            pl.BlockSpec((tm, tn), lambda i, j: (i, j)),
        ],
        out_specs=pl.BlockSpec((tm, tn), lambda i, j: (i, j)),
        out_shape=jax.ShapeDtypeStruct((M, D), F32),
        compiler_params=pltpu.CompilerParams(
            dimension_semantics=("arbitrary", "arbitrary"),
            vmem_limit_bytes=_vmem_limit(blocks, 0, 2 * _nbytes((tm, tn), F32))),
        name="out_proj",
    )(mix2d, w, x2d)


def _upproj_kernel(x_ref, g_ref, w_ref, u_ref, xn_ref):
    @pl.when(pl.program_id(1) == 0)
    def _():
        x = x_ref[...]
        xn_ref[...] = (x * _rms_scale(x) * g_ref[...]).astype(BF16)

    acc = jnp.dot(xn_ref[...], w_ref[...], preferred_element_type=F32)
    u_ref[...] = jnp.square(jnp.maximum(acc, 0.0)).astype(u_ref.dtype)


def _up_proj(h2d, gain, w_up, *, tm, tn):
    M, D = h2d.shape
    N = w_up.shape[1]
    blocks = [_nbytes((tm, D), F32), _nbytes((D, tn), BF16), _nbytes((tm, tn), BF16)]
    return pl.pallas_call(
        _upproj_kernel,
        grid=(M // tm, N // tn),
        in_specs=[pl.BlockSpec((tm, D), lambda i, j: (i, 0)),
                  pl.BlockSpec((1, D), lambda i, j: (0, 0)),
                  pl.BlockSpec((D, tn), lambda i, j: (0, j))],
        out_specs=pl.BlockSpec((tm, tn), lambda i, j: (i, j)),
        out_shape=jax.ShapeDtypeStruct((M, N), BF16),
        scratch_shapes=[pltpu.VMEM((tm, D), BF16)],
        compiler_params=pltpu.CompilerParams(
            dimension_semantics=("arbitrary", "arbitrary"),
            vmem_limit_bytes=_vmem_limit(blocks, _nbytes((tm, D), BF16),
                                         _nbytes((tm, D), F32) + 2 * _nbytes((tm, tn), F32))),
        name="up_proj",
    )(h2d, gain, w_up)


def _downproj_kernel(u_ref, w_ref, h_ref, g_ref, y_ref):
    k = pl.program_id(1)

    @pl.when(k == 0)
    def _():
        y_ref[...] = h_ref[...]

    u = u_ref[...]
    for n in range(0, y_ref.shape[1], ACC_COLS):
        y_ref[:, n:n + ACC_COLS] += jnp.dot(u, w_ref[:, n:n + ACC_COLS], preferred_element_type=F32)

    @pl.when(k == pl.num_programs(1) - 1)
    def _():
        scale = _rms_scale(y_ref[...])
        y_ref[...] = y_ref[...] * scale * g_ref[...]


def _down_proj(u2d, w_down, h2d, gain, *, tm, tk):
    M, D = h2d.shape
    K = u2d.shape[1]
    blocks = [_nbytes((tm, tk), BF16), _nbytes((tk, D), BF16), 2 * _nbytes((tm, D), F32)]
    return pl.pallas_call(
        _downproj_kernel,
        grid=(M // tm, K // tk),
        in_specs=[pl.BlockSpec((tm, tk), lambda i, k: (i, k)),
                  pl.BlockSpec((tk, D), lambda i, k: (k, 0)),
                  pl.BlockSpec((tm, D), lambda i, k: (i, 0)),
                  pl.BlockSpec((1, D), lambda i, k: (0, 0))],
        out_specs=pl.BlockSpec((tm, D), lambda i, k: (i, 0)),
        out_shape=jax.ShapeDtypeStruct((M, D), F32),
        compiler_params=pltpu.CompilerParams(
            dimension_semantics=("arbitrary", "arbitrary"),
            vmem_limit_bytes=_vmem_limit(blocks, 0, _nbytes((tm, D), F32) + _nbytes((tm, ACC_COLS), F32))),
        name="down_proj",
    )(u2d, w_down, h2d, gain)


def _attn_sample_kernel(q_ref, kn_ref, vn_ref, k1_ref, k4_ref, k16_ref, v1_ref, v4_ref, v16_ref, o_ref):
    q = q_ref[...] * (HEAD_DIM ** -0.5)
    kn = kn_ref[...]
    vn = vn_ref[...]
    s0 = jnp.sum(q * kn, axis=-1, keepdims=True)

    ms, dens, nums = [], [], []
    for kt_ref, vt_ref in ((k1_ref, v1_ref), (k4_ref, v4_ref), (k16_ref, v16_ref)):
        kt = kt_ref[...]
        vt = vt_ref[...]
        s = jnp.sum(q[None] * kt, axis=-1, keepdims=True)
        m = jnp.maximum(jnp.max(s, axis=0), s0)
        p = jnp.exp(s - m[None])
        p0 = jnp.exp(s0 - m)
        dens.append(jnp.sum(p, axis=0) + p0)
        nums.append(jnp.sum(p * vt, axis=0) + p0 * vn)
        ms.append(m)
    m_all = jnp.maximum(jnp.maximum(ms[0], ms[1]), ms[2])
    num = jnp.zeros((ATT_HEADS, HEAD_DIM), F32)
    den = jnp.zeros((ATT_HEADS, 1), F32)
    for m, d, n in zip(ms, dens, nums):
        wgt = jnp.exp(m - m_all)
        num = num + wgt * n
        den = den + wgt * d
    o_ref[...] = (num / den).astype(o_ref.dtype)


def _attn_sample(qs, kns, vns, cache_k, cache_v):
    Bd, W, H, hd = cache_k.shape
    new_spec = pl.BlockSpec((None, H, hd), lambda b: (b, 0, 0))
    in_specs = [new_spec, new_spec, new_spec]
    views_k, views_v = [], []
    for (w, r) in DILATED_CONFIGS:
        assert w // r == Q_TILE and W % w == 0 and PAST_LEN >= W >= w
        views_k.append(cache_k.reshape(Bd, W // r, r, H, hd))
        views_v.append(cache_v.reshape(Bd, W // r, r, H, hd))
    for _ in range(2):
        for (w, r) in DILATED_CONFIGS:
            last = W // w - 1
            in_specs.append(pl.BlockSpec((None, Q_TILE, None, H, hd),
                                         lambda b, last=last: (b, last, 0, 0, 0)))
    blocks = [3 * _nbytes((H, hd), F32), 6 * _nbytes((Q_TILE, H, hd), F32)]
    return pl.pallas_call(
        _attn_sample_kernel,
        grid=(Bd,),
        in_specs=in_specs,
        out_specs=pl.BlockSpec((None, H, hd), lambda b: (b, 0, 0)),
        out_shape=jax.ShapeDtypeStruct((Bd, H, hd), BF16),
        compiler_params=pltpu.CompilerParams(
            dimension_semantics=("arbitrary",),
            vmem_limit_bytes=_vmem_limit(blocks, 0, 8 * _nbytes((Q_TILE, H, hd), F32))),
        name="attn_sample",
    )(qs, kns, vns, *views_k, *views_v)


def _split_bf16(a):
    hi = a.astype(BF16)
    return hi, (a - hi.astype(F32)).astype(BF16)


def _qk_sample_kernel(x_ref, g_ref, w_ref, o_ref, xhi_ref, xlo_ref):
    @pl.when(pl.program_id(0) == 0)
    def _():
        x = x_ref[...]
        xhi_ref[...], xlo_ref[...] = _split_bf16(x * _rms_scale(x) * g_ref[...])

    whi, wlo = _split_bf16(w_ref[...])
    xhi = xhi_ref[...]
    o_ref[...] = (lax.dot_general(xhi, whi, NT_DIMS, preferred_element_type=F32)
                  + lax.dot_general(xhi, wlo, NT_DIMS, preferred_element_type=F32)
                  + lax.dot_general(xlo_ref[...], whi, NT_DIMS, preferred_element_type=F32))


def _qk_sample(x2d, gain, w_t, *, tn=512):
    Bd, D = x2d.shape
    n_out = 2 * MLSTM_HEADS * MLSTM_DQK
    first = 3 * ATT_WIDTH // tn
    blocks = [_nbytes((Bd, D), F32), _nbytes((tn, D), F32), _nbytes((Bd, tn), F32)]
    return pl.pallas_call(
        _qk_sample_kernel,
        grid=(n_out // tn,),
        in_specs=[pl.BlockSpec((Bd, D), lambda j: (0, 0)),
                  pl.BlockSpec((1, D), lambda j: (0, 0)),
                  pl.BlockSpec((tn, D), lambda j: (first + j, 0))],
        out_specs=pl.BlockSpec((Bd, tn), lambda j: (0, j)),
        out_shape=jax.ShapeDtypeStruct((Bd, n_out), F32),
        scratch_shapes=[pltpu.VMEM((Bd, D), BF16), pltpu.VMEM((Bd, D), BF16)],
        compiler_params=pltpu.CompilerParams(
            dimension_semantics=("arbitrary",),
            vmem_limit_bytes=_vmem_limit(blocks, _nbytes((Bd, D), F32), 4 * _nbytes((tn, D), BF16))),
        name="qk_sample",
    )(x2d, gain, w_t)


def _mlstm_sample_kernel(q_ref, k_ref, v_ref, om_ref, g_ref, c0_ref, n0_ref, m0_ref, mhg_ref,
                         hn_ref, c_out, n_out, m_out):
    H = MLSTM_HEADS
    for h in range(H):
        qf = q_ref[h:h + 1, :] * (MLSTM_DQK ** -0.5)
        kf = k_ref[h:h + 1, :]
        vf = v_ref[h:h + 1, :]
        li = g_ref[:, h:h + 1]
        lf = _log_sigmoid(g_ref[:, H + h:H + h + 1])
        m0 = m0_ref[:, h:h + 1]
        c0 = c0_ref[h]
        n0 = n0_ref[h:h + 1, :]

        inter = lf + m0
        m_t = jnp.maximum(inter, li)
        qk = jnp.sum(qf * kf, axis=1, keepdims=True)
        sw = qk * jnp.exp(li - m_t)
        a = jnp.exp(inter - m_t)
        qhi, qlo = _split_bf16(jnp.broadcast_to(qf, (BF16_SUBLANES, MLSTM_DQK)))
        chi, clo = _split_bf16(c0)
        qc = (jnp.dot(qhi, chi, preferred_element_type=F32)
              + jnp.dot(qhi, clo, preferred_element_type=F32)
              + jnp.dot(qlo, chi, preferred_element_type=F32))[0:1, :]
        num = sw * vf + a * qc
        den = sw + a * jnp.sum(qf * n0, axis=1, keepdims=True)
        hh = num / jnp.maximum(jnp.abs(den), jnp.exp(-m_t))

        ws = jnp.exp(li - m_t)
        decay = jnp.exp(inter - m_t)
        kw = kf * ws
        kcol = jnp.broadcast_to(kw, (LANES, MLSTM_DQK)).T[:, 0:1]
        c_out[h] = decay * c0 + kcol * vf
        n_out[h:h + 1, :] = decay * n0 + kw
        m_out[:, h:h + 1] = m_t

        hn = hh * _rms_scale(hh) * mhg_ref[h:h + 1, :] * _sigmoid(om_ref[h:h + 1, :])
        hn_ref[h:h + 1, :] = hn.astype(hn_ref.dtype)


def _mlstm_sample(qs, ks, vs, oms, gs, c0, n0, m0, mhg):
    Bd = qs.shape[0]
    H = MLSTM_HEADS
    blocks = [2 * _nbytes((8, MLSTM_DQK), F32), 2 * _nbytes((8, MLSTM_DV), F32),
              2 * _nbytes((H, MLSTM_DQK, MLSTM_DV), F32), 4 * _nbytes((8, MLSTM_DV), F32)]
    return pl.pallas_call(
        _mlstm_sample_kernel,
        grid=(Bd,),
        in_specs=[
            pl.BlockSpec((None, H, MLSTM_DQK), lambda b: (b, 0, 0)),
            pl.BlockSpec((None, H, MLSTM_DQK), lambda b: (b, 0, 0)),
            pl.BlockSpec((None, H, MLSTM_DV), lambda b: (b, 0, 0)),
            pl.BlockSpec((None, H, MLSTM_DV), lambda b: (b, 0, 0)),
            pl.BlockSpec((None, 1, LANES), lambda b: (b, 0, 0)),
            pl.BlockSpec((None, H, MLSTM_DQK, MLSTM_DV), lambda b: (b, 0, 0, 0)),
            pl.BlockSpec((None, H, MLSTM_DQK), lambda b: (b, 0, 0)),
            pl.BlockSpec((None, 1, H), lambda b: (b, 0, 0)),
            pl.BlockSpec((H, MLSTM_DV), lambda b: (0, 0)),
        ],
        out_specs=[
            pl.BlockSpec((None, H, MLSTM_DV), lambda b: (b, 0, 0)),
            pl.BlockSpec((None, H, MLSTM_DQK, MLSTM_DV), lambda b: (b, 0, 0, 0)),
            pl.BlockSpec((None, H, MLSTM_DQK), lambda b: (b, 0, 0)),
            pl.BlockSpec((None, 1, H), lambda b: (b, 0, 0)),
        ],
        out_shape=[
            jax.ShapeDtypeStruct((Bd, H, MLSTM_DV), BF16),
            jax.ShapeDtypeStruct((Bd, H, MLSTM_DQK, MLSTM_DV), F32),
            jax.ShapeDtypeStruct((Bd, H, MLSTM_DQK), F32),
            jax.ShapeDtypeStruct((Bd, 1, H), F32),
        ],
        compiler_params=pltpu.CompilerParams(
            dimension_semantics=("arbitrary",),
            vmem_limit_bytes=_vmem_limit(blocks, 0, 4 * _nbytes((MLSTM_DQK, MLSTM_DV), F32))),
        name="mlstm_sample",
    )(qs, ks, vs, oms, gs, c0, n0, m0, mhg)


def _rope_tables(pos):
    half = HEAD_DIM // 2
    inv = ROPE_THETA ** (-jnp.arange(half, dtype=F32) / half)
    ang = pos.astype(F32)[:, None] * inv[None, :]
    cos = jnp.cos(ang)
    sin = jnp.sin(ang)
    return jnp.concatenate([cos, cos], axis=-1), jnp.concatenate([-sin, sin], axis=-1)


def _row_tile(M, cap):
    tm = min(M, cap)
    assert M % tm == 0
    return tm


def kernel(x_prompt, x_sample, cache_k, cache_v, state_C, state_n, state_m,
           norm1_g, w_in, b_if, mh_norm_g, w_out, norm2_g, w_up, w_down, final_g):
    B, T, D = x_prompt.shape
    Bd, Td, _ = x_sample.shape
    depth = w_in.shape[0]
    assert depth == 1 and Td == 1 and D == D_MODEL
    keep = min(WIN_MAX, T)
    H = MLSTM_HEADS

    w_in_t = jnp.transpose(w_in[0])
    w_gate = jnp.pad(w_in_t[N_MAIN:], ((0, LANES - 2 * H), (0, 0))).astype(BF16)
    bias = jnp.pad(b_if[0], (0, LANES - 2 * H)).reshape(1, LANES)
    g1 =norm1_g[0].reshape(1, D)
    g2 = norm2_g[0].reshape(1, D)
    gf = final_g.reshape(1, D)
    mhg = mh_norm_g[0]

    cos_p, sin_p = _rope_tables(jnp.arange(T, dtype=jnp.int32))
    cos_s, sin_s = _rope_tables(jnp.full((Bd,), PAST_LEN, dtype=jnp.int32))

    xp2 = x_prompt.reshape(B * T, D)
    tm = _row_tile(T, 512)
    tm_w = _row_tile(T, 1024)
    xn_p, gates = _norm_rows(xp2, g1, w_gate, bias, tm=tm)
    w_main = _cast_rows_bf16(w_in_t, N_MAIN)
    z, wo, wu = _in_proj_prompt(xn_p, w_main, cos_p, sin_p, [w_out[0], w_up[0]],
                                tm=tm_w, tn=1024, pos_tiles=T // tm_w)
    z3 = z.reshape(B, T, N_MAIN)
    att = _attn_prompt(z3)

    g3 = gates.reshape(B, T, LANES)
    li_row = jnp.swapaxes(g3[:, :, :H], 1, 2).reshape(B, H, 1, T)
    fg_row = jnp.swapaxes(g3[:, :, H:2 * H], 1, 2).reshape(B, H, 1, T)
    li_col = li_row.reshape(B, H, T, 1)
    fg_col = fg_row.reshape(B, H, T, 1)
    hn, c_p, n_p, m_p = _mlstm_prompt(z3, li_row, fg_row, li_col, fg_col, mhg.reshape(1, MLSTM_WIDTH),
                                      L=min(T, MLSTM_CHUNK), heads=MLSTM_HEADS_PER_STEP)

    h_p = _out_proj_prompt(att.reshape(B * T, ATT_WIDTH), hn.reshape(B * T, MLSTM_WIDTH), wo, xp2,
                           tm=tm_w, tn=1024)
    hn2_p = _norm_rows(h_p, g2, tm=tm)
    u_p, wd = _up_proj_prompt(hn2_p, wu, [w_down[0]], tm=tm_w, tn=1024)
    y_p = _down_proj(u_p, wd, h_p, gf, tm=tm, tk=1024)

    k_prompt = z3[:, T - keep:, ATT_WIDTH:2 * ATT_WIDTH].reshape(1, B, keep, ATT_HEADS, HEAD_DIM)
    v_prompt = z3[:, T - keep:, 2 * ATT_WIDTH:3 * ATT_WIDTH].reshape(1, B, keep, ATT_HEADS, HEAD_DIM)

    xs2 = x_sample.reshape(Bd, D)
    zs, gates_s = _in_proj(xs2, g1, w_main, w_gate, bias, cos_s, sin_s, tm=Bd, tn=1024, pos_tiles=1)
    qa_s = zs[:, :ATT_WIDTH].reshape(Bd, ATT_HEADS, HEAD_DIM)
    ka_s = zs[:, ATT_WIDTH:2 * ATT_WIDTH].reshape(Bd, ATT_HEADS, HEAD_DIM)
    va_s = zs[:, 2 * ATT_WIDTH:3 * ATT_WIDTH].reshape(Bd, ATT_HEADS, HEAD_DIM)
    att_s = _attn_sample(qa_s, ka_s, va_s, cache_k[0], cache_v[0])

    qk_s = _qk_sample(xs2, g1, w_in_t)
    qs = qk_s[:, :H * MLSTM_DQK].reshape(Bd, H, MLSTM_DQK)
    ks = qk_s[:, H * MLSTM_DQK:].reshape(Bd, H, MLSTM_DQK)
    o = 3 * ATT_WIDTH + 2 * H * MLSTM_DQK
    vs = zs[:, o:o + MLSTM_WIDTH].reshape(Bd, H, MLSTM_DV)
    o += MLSTM_WIDTH
    oms = zs[:, o:o + MLSTM_WIDTH].reshape(Bd, H, MLSTM_DV)
    hn_s, c_s, n_s, m_s = _mlstm_sample(qs, ks, vs, oms, gates_s.reshape(Bd, 1, LANES),
                                        state_C[0], state_n[0], state_m[0].reshape(Bd, 1, H),
                                        mhg.reshape(H, MLSTM_DV))

    mix_s = jnp.concatenate([att_s.reshape(Bd, ATT_WIDTH), hn_s.reshape(Bd, MLSTM_WIDTH)], axis=-1)
    h_s = _out_proj(mix_s, wo, xs2, tm=Bd, tn=1024)
    u_s = _up_proj(h_s, g2, wu, tm=Bd, tn=1024)
    y_s = _down_proj(u_s, wd, h_s, gf, tm=Bd, tk=1024)

    k_sample = zs[:, ATT_WIDTH:2 * ATT_WIDTH].reshape(1, Bd, 1, ATT_HEADS, HEAD_DIM)
    v_sample = zs[:, 2 * ATT_WIDTH:3 * ATT_WIDTH].reshape(1, Bd, 1, ATT_HEADS, HEAD_DIM)

    return (y_p.reshape(B, T, D), y_s.reshape(Bd, 1, D),
            k_prompt, v_prompt,
            c_p[None], n_p.reshape(1, B, H, MLSTM_DQK), m_p.reshape(1, B, H),
            k_sample, v_sample,
            c_s[None], n_s[None], m_s.reshape(1, Bd, H))
```
